```python
import jax, jax.numpy as jnp
from jax import lax
import numpy as np

D_MODEL = 4096
BATCH = 2
SEQ = 8192
DEPTH = 1

HEAD_DIM = 128
MOBA_HEADS = 16
MOBA_WIDTH = MOBA_HEADS * HEAD_DIM
MOBA_BLOCK = 256
MOBA_TOPK = 3
MOBA_QCHUNK = 32
SGU_CHUNK = 128
SGU_WIDTH = D_MODEL // 2
SGU_GROUPS = 16
SGU_GROUP_DIM = SGU_WIDTH // SGU_GROUPS
N_MEM = 256
XATTN_HEADS = 4
XATTN_WIDTH = XATTN_HEADS * HEAD_DIM
D_FF = -(-8 * D_MODEL // (3 * 256)) * 256
IN_COLS = 3 * MOBA_WIDTH + 2 * SGU_WIDTH + 2 * D_MODEL
RMS_EPS = 1e-6
LN_EPS = 1e-5
ROPE_THETA = 10000.0
NEG_INF = -1e30

kernel_name = 'moba_sgu_gated_hybrid_block'


def rms_norm(x, g):
    xf = x.astype(jnp.float32)
    y = xf * lax.rsqrt(jnp.mean(xf * xf, axis=-1, keepdims=True) + RMS_EPS)
    return (y * g.astype(jnp.float32)).astype(x.dtype)


def layer_norm(x, g, b):
    xf = x.astype(jnp.float32)
    mu = jnp.mean(xf, axis=-1, keepdims=True)
    var = jnp.mean(jnp.square(xf - mu), axis=-1, keepdims=True)
    y = (xf - mu) * lax.rsqrt(var + LN_EPS)
    return (y * g.astype(jnp.float32) + b.astype(jnp.float32)).astype(x.dtype)


def rotary(x, pos):
    dh = x.shape[-1]
    half = dh // 2
    inv_freq = jnp.power(ROPE_THETA, -(jnp.arange(half, dtype=jnp.float32) * 2.0 / dh))
    ang = pos.astype(jnp.float32)[:, None] * inv_freq[None, :]
    cos, sin = jnp.cos(ang), jnp.sin(ang)
    xf = x.astype(jnp.float32)
    x1, x2 = xf[..., :half], xf[..., half:]
    return jnp.concatenate([x1 * cos - x2 * sin, x2 * cos + x1 * sin], axis=-1).astype(x.dtype)


def moba_attention(q, k, v):
    bsz, n_heads, seq, dh = q.shape
    pad = (-seq) % MOBA_BLOCK
    padw = ((0, 0), (0, 0), (0, pad), (0, 0))
    q = jnp.pad(q, padw)
    k = jnp.pad(k, padw)
    v = jnp.pad(v, padw)
    s_pad = seq + pad
    n_blocks = s_pad // MOBA_BLOCK
    top_k = min(MOBA_TOPK, n_blocks)
    n_sel = top_k * MOBA_BLOCK
    kb = k.reshape(bsz, n_heads, n_blocks, MOBA_BLOCK, dh)
    vb = v.reshape(bsz, n_heads, n_blocks, MOBA_BLOCK, dh)
    k_mean = jnp.mean(kb.astype(jnp.float32), axis=3)
    scale = dh ** -0.5
    b_idx = jnp.arange(bsz)[:, None, None, None]
    h_idx = jnp.arange(n_heads)[None, :, None, None]

    def one_chunk(ci):
        start = ci * MOBA_QCHUNK
        cur = start // MOBA_BLOCK
        qc = lax.dynamic_slice_in_dim(q, start, MOBA_QCHUNK, axis=2).astype(jnp.float32)
        gate = jnp.einsum('bhqd,bhnd->bhqn', qc, k_mean)
        gate = jnp.where(jnp.arange(n_blocks) < cur, gate, NEG_INF)
        _, sel = lax.top_k(gate, top_k)
        valid = jnp.arange(top_k) < cur
        k_sel = kb[b_idx, h_idx, sel].astype(jnp.float32)
        v_sel = vb[b_idx, h_idx, sel].astype(jnp.float32)
        s_sel = jnp.einsum('bhqd,bhqnkd->bhqnk', qc, k_sel) * scale
        s_sel = jnp.where(valid[:, None], s_sel, NEG_INF)
        k_own = lax.dynamic_slice_in_dim(k, cur * MOBA_BLOCK, MOBA_BLOCK, axis=2).astype(jnp.float32)
        v_own = lax.dynamic_slice_in_dim(v, cur * MOBA_BLOCK, MOBA_BLOCK, axis=2).astype(jnp.float32)
        s_own = jnp.einsum('bhqd,bhkd->bhqk', qc, k_own) * scale
        q_pos = start + jnp.arange(MOBA_QCHUNK)
        k_pos = cur * MOBA_BLOCK + jnp.arange(MOBA_BLOCK)
        s_own = jnp.where(k_pos[None, :] <= q_pos[:, None], s_own, NEG_INF)
        scores = jnp.concatenate([s_sel.reshape(bsz, n_heads, MOBA_QCHUNK, n_sel), s_own], axis=-1)
        p = jax.nn.softmax(scores, axis=-1)
        p_sel = p[..., :n_sel].reshape(bsz, n_heads, MOBA_QCHUNK, top_k, MOBA_BLOCK)
        out = (jnp.einsum('bhqnk,bhqnkd->bhqd', p_sel, v_sel)
               + jnp.einsum('bhqk,bhkd->bhqd', p[..., n_sel:], v_own))
        return out.astype(v.dtype)

    outs = lax.map(one_chunk, jnp.arange(s_pad // MOBA_QCHUNK))
    out = jnp.transpose(outs, (1, 2, 0, 3, 4)).reshape(bsz, n_heads, s_pad, dh)
    return out[:, :, :seq]


def spatial_gating(u, v, ln_g, ln_b, w_s, b_s):
    bsz, seq, _ = v.shape
    v = layer_norm(v, ln_g, ln_b)
    vc = v.reshape(bsz, seq // SGU_CHUNK, SGU_CHUNK, SGU_GROUPS, SGU_GROUP_DIM)
    causal = jnp.tril(jnp.ones((SGU_CHUNK, SGU_CHUNK), dtype=w_s.dtype))
    w = w_s * causal[None]
    mixed = jnp.einsum('gts,bcsgd->bctgd', w, vc) + jnp.transpose(b_s)[None, None, :, :, None]
    return u * mixed.reshape(bsz, seq, SGU_WIDTH)


def hybrid_mixer(h, w_in, sgu_ln_g, sgu_ln_b, w_sgu, b_sgu, w_branch_a, w_branch_b, w_out):
    bsz, seq, _ = h.shape
    z = h @ w_in
    cuts = np.cumsum([MOBA_WIDTH, MOBA_WIDTH, MOBA_WIDTH, SGU_WIDTH, SGU_WIDTH, D_MODEL]).tolist()
    q, k, v, u, vs, g_a, g_b = jnp.split(z, cuts, axis=-1)

    def heads(t):
        return jnp.transpose(t.reshape(bsz, seq, MOBA_HEADS, HEAD_DIM), (0, 2, 1, 3))

    pos = jnp.arange(seq)
    o_a = moba_attention(rotary(heads(q), pos), rotary(heads(k), pos), heads(v))
    o_a = jnp.transpose(o_a, (0, 2, 1, 3)).reshape(bsz, seq, MOBA_WIDTH)
    o_b = spatial_gating(jax.nn.gelu(u), jax.nn.gelu(vs), sgu_ln_g, sgu_ln_b, w_sgu, b_sgu)
    merged = jax.nn.sigmoid(g_a) * (o_a @ w_branch_a) + jax.nn.sigmoid(g_b) * (o_b @ w_branch_b)
    return merged @ w_out


def memory_cross_attention(h, mem_n, w_xq, w_xkv, w_xo):
    bsz, seq, _ = h.shape
    n_mem = mem_n.shape[1]
    q = (h @ w_xq).reshape(bsz, seq, XATTN_HEADS, HEAD_DIM)
    k, v = jnp.split(mem_n @ w_xkv, 2, axis=-1)
    k = k.reshape(bsz, n_mem, XATTN_HEADS, HEAD_DIM)
    v = v.reshape(bsz, n_mem, XATTN_HEADS, HEAD_DIM)
    s = jnp.einsum('bshd,bmhd->bhsm', q, k).astype(jnp.float32) * (HEAD_DIM ** -0.5)
    p = jax.nn.softmax(s, axis=-1).astype(v.dtype)
    o = jnp.einsum('bhsm,bmhd->bshd', p, v).reshape(bsz, seq, XATTN_WIDTH)
    return o @ w_xo


def swiglu(h, w_gate, w_up, w_down):
    return (jax.nn.silu(h @ w_gate) * (h @ w_up)) @ w_down


def setup_inputs(seed: int = 0) -> dict:
    key = jax.random.key(seed)
    ks = jax.random.split(key, 24)
    f32 = jnp.float32

    def nrm(k, shape, scale):
        return jax.random.normal(k, shape, f32) * scale

    def gain(k, shape):
        return 1.0 + 0.05 * jax.random.normal(k, shape, f32)

    L = DEPTH
    return {
        'x': nrm(ks[0], (BATCH, SEQ, D_MODEL), 1.0),
        'mem': nrm(ks[1], (BATCH, N_MEM, D_MODEL), 1.0),
        'norm_mix_g': gain(ks[2], (L, D_MODEL)),
        'w_in': nrm(ks[3], (L, D_MODEL, IN_COLS), D_MODEL ** -0.5),
        'sgu_ln_g': gain(ks[4], (L, SGU_WIDTH)),
        'sgu_ln_b': nrm(ks[5], (L, SGU_WIDTH), 0.02),
        'w_sgu': nrm(ks[6], (L, SGU_GROUPS, SGU_CHUNK, SGU_CHUNK), SGU_CHUNK ** -0.5),
        'b_sgu': gain(ks[7], (L, SGU_GROUPS, SGU_CHUNK)),
        'w_branch_a': nrm(ks[8], (L, MOBA_WIDTH, D_MODEL), MOBA_WIDTH ** -0.5),
        'w_branch_b': nrm(ks[9], (L, SGU_WIDTH, D_MODEL), SGU_WIDTH ** -0.5),
        'w_out': nrm(ks[10], (L, D_MODEL, D_MODEL), D_MODEL ** -0.5),
        'norm_xattn_g': gain(ks[11], (L, D_MODEL)),
        'norm_mem_g': gain(ks[12], (L, D_MODEL)),
        'w_xq': nrm(ks[13], (L, D_MODEL, XATTN_WIDTH), D_MODEL ** -0.5),
        'w_xkv': nrm(ks[14], (L, D_MODEL, 2 * XATTN_WIDTH), D_MODEL ** -0.5),
        'w_xo': nrm(ks[15], (L, XATTN_WIDTH, D_MODEL), XATTN_WIDTH ** -0.5),
        'norm_ffn_g': gain(ks[16], (L, D_MODEL)),
        'w_ff_gate': nrm(ks[17], (L, D_MODEL, D_FF), D_MODEL ** -0.5),
        'w_ff_up': nrm(ks[18], (L, D_MODEL, D_FF), D_MODEL ** -0.5),
        'w_ff_down': nrm(ks[19], (L, D_FF, D_MODEL), D_FF ** -0.5),
        'norm_final_g': gain(ks[20], (D_MODEL,)),
    }


def reference(x, mem, norm_mix_g, w_in, sgu_ln_g, sgu_ln_b, w_sgu, b_sgu, w_branch_a,
              w_branch_b, w_out, norm_xattn_g, norm_mem_g, w_xq, w_xkv, w_xo, norm_ffn_g,
              w_ff_gate, w_ff_up, w_ff_down, norm_final_g):
    for l in range(DEPTH):
        h = rms_norm(x, norm_mix_g[l])
        x = x + hybrid_mixer(h, w_in[l], sgu_ln_g[l], sgu_ln_b[l], w_sgu[l], b_sgu[l],
                             w_branch_a[l], w_branch_b[l], w_out[l])
        h = rms_norm(x, norm_xattn_g[l])
        mem_n = rms_norm(mem, norm_mem_g[l])
        x = x + memory_cross_attention(h, mem_n, w_xq[l], w_xkv[l], w_xo[l])
        h = rms_norm(x, norm_ffn_g[l])
        x = x + swiglu(h, w_ff_gate[l], w_ff_up[l], w_ff_down[l])
    return rms_norm(x, norm_final_g)
```

```python
import functools
import math

import jax
import jax.numpy as jnp
from jax import lax
from jax.experimental import pallas as pl
from jax.experimental.pallas import tpu as pltpu

F32 = jnp.float32
BF16 = jnp.bfloat16

HEAD_DIM = 128
MOBA_HEADS = 16
MOBA_WIDTH = MOBA_HEADS * HEAD_DIM
MOBA_BLOCK = 256
MOBA_TOPK = 3
SGU_CHUNK = 128
SGU_GROUPS = 16
XATTN_HEADS = 4
XATTN_WIDTH = XATTN_HEADS * HEAD_DIM
RMS_EPS = 1e-6
LN_EPS = 1e-5
ROPE_THETA = 10000.0
NEG_INF = -1e30
REMOVED = -3e38

LANES = 128
FF_PAD = 512
VMEM_LIMIT = 56 * 1024 * 1024


def _cparams(sem):
    return pltpu.CompilerParams(dimension_semantics=sem, vmem_limit_bytes=VMEM_LIMIT)


def _rmsnorm_rows(x, g):
    ms = jnp.mean(x * x, axis=-1, keepdims=True)
    return x * lax.rsqrt(ms + RMS_EPS) * g


def _rmsnorm_kernel(x_ref, g_ref, o_ref):
    o_ref[...] = _rmsnorm_rows(x_ref[...], g_ref[...]).astype(o_ref.dtype)


def _rmsnorm(x, g, out_dtype, bm=256):
    t, d = x.shape
    return pl.pallas_call(
        _rmsnorm_kernel,
        grid=(t // bm,),
        in_specs=[pl.BlockSpec((bm, d), lambda i: (i, 0)),
                  pl.BlockSpec((1, d), lambda i: (0, 0))],
        out_specs=pl.BlockSpec((bm, d), lambda i: (i, 0)),
        out_shape=jax.ShapeDtypeStruct((t, d), out_dtype),
        compiler_params=_cparams(("arbitrary",)),
        name="rmsnorm",
    )(x, g.reshape(1, d))


def _rope_kernel(freq_ref, cos_ref, sin_ref, *, rows):
    i = pl.program_id(0)
    pos = (i * rows + lax.broadcasted_iota(jnp.int32, (rows, LANES), 0)).astype(F32)
    lane = lax.broadcasted_iota(jnp.int32, (rows, LANES), 1)
    ang = pos * freq_ref[...]
    cos_ref[...] = jnp.cos(ang)
    s = jnp.sin(ang)
    sin_ref[...] = jnp.where(lane < HEAD_DIM // 2, -s, s)


def _rope_table(seq, rows=1024):
    half = HEAD_DIM // 2
    inv_freq = jnp.power(ROPE_THETA, -(jnp.arange(half, dtype=F32) * 2.0 / HEAD_DIM))
    freq = jnp.concatenate([inv_freq, inv_freq]).reshape(1, HEAD_DIM)
    return pl.pallas_call(
        functools.partial(_rope_kernel, rows=rows),
        grid=(seq // rows,),
        in_specs=[pl.BlockSpec((1, HEAD_DIM), lambda i: (0, 0))],
        out_specs=[pl.BlockSpec((rows, HEAD_DIM), lambda i: (i, 0))] * 2,
        out_shape=[jax.ShapeDtypeStruct((seq, HEAD_DIM), F32)] * 2,
        compiler_params=_cparams(("arbitrary",)),
        name="rope_table",
    )(freq)


def _gelu_tanh(x):
    c = math.sqrt(2.0 / math.pi)
    return x * (0.5 * (1.0 + jnp.tanh(c * (x + 0.044715 * (x * x * x)))))


def _sigmoid(x):
    return 1.0 / (1.0 + jnp.exp(-x))


def _inproj_kernel(h_ref, w_ref, cos_ref, sin_ref, z_ref, acc_ref, *, bn, n_q, n_k, n_v, n_act):
    n = pl.program_id(1)
    acc_ref[...] = jnp.dot(h_ref[...], w_ref[...], preferred_element_type=F32)

    @pl.when(n < n_k)
    def _rotary():
        scale = jnp.where(n < n_q, HEAD_DIM ** -0.5, 1.0).astype(F32)
        cos = cos_ref[...] * scale
        sin = sin_ref[...] * scale
        for hh in range(bn // HEAD_DIM):
            blk = acc_ref[:, hh * HEAD_DIM:(hh + 1) * HEAD_DIM]
            rot = pltpu.roll(blk, HEAD_DIM // 2, axis=1)
            z_ref[:, hh * HEAD_DIM:(hh + 1) * HEAD_DIM] = (blk * cos + rot * sin).astype(z_ref.dtype)

    @pl.when((n >= n_k) & (n < n_v))
    def _plain():
        z_ref[...] = acc_ref[...].astype(z_ref.dtype)

    @pl.when((n >= n_v) & (n < n_act))
    def _gelu():
        z_ref[...] = _gelu_tanh(acc_ref[...]).astype(z_ref.dtype)

    @pl.when(n >= n_act)
    def _gate():
        z_ref[...] = _sigmoid(acc_ref[...]).astype(z_ref.dtype)


def _inproj(h, w, cos, sin, *, seq, sgu_width, bm=1024, bn=1024):
    t, d = h.shape
    cols = w.shape[1]
    seq_blocks = seq // bm
    n_q = MOBA_WIDTH // bn
    n_k = 2 * MOBA_WIDTH // bn
    n_v = 3 * MOBA_WIDTH // bn
    n_act = (3 * MOBA_WIDTH + 2 * sgu_width) // bn
    kern = functools.partial(_inproj_kernel, bn=bn, n_q=n_q, n_k=n_k, n_v=n_v, n_act=n_act)
    return pl.pallas_call(
        kern,
        grid=(t // bm, cols // bn),
        in_specs=[pl.BlockSpec((bm, d), lambda m, n: (m, 0)),
                  pl.BlockSpec((d, bn), lambda m, n: (0, n)),
                  pl.BlockSpec((bm, HEAD_DIM), lambda m, n: (m % seq_blocks, 0)),
                  pl.BlockSpec((bm, HEAD_DIM), lambda m, n: (m % seq_blocks, 0))],
        out_specs=pl.BlockSpec((bm, bn), lambda m, n: (m, n)),
        out_shape=jax.ShapeDtypeStruct((t, cols), BF16),
        scratch_shapes=[pltpu.VMEM((bm, bn), F32)],
        compiler_params=_cparams(("arbitrary", "arbitrary")),
        name="in_proj",
    )(h, w, cos, sin)


def _moba_kernel(q_ref, k_ref, v_ref, o_ref, kext_ref, kmean_ref, *, tile, n_blocks):
    qi = pl.program_id(2)
    blocks_per_tile = tile // MOBA_BLOCK
    contract_last = (((1,), (1,)), ((), ()))

    @pl.when(qi == 0)
    def _prepare_keys():
        kmean_ref[...] = jnp.zeros_like(kmean_ref)
        lane = lax.broadcasted_iota(jnp.int32, (MOBA_BLOCK, LANES), 1)
        for j in range(n_blocks):
            rows = pl.ds(j * MOBA_BLOCK, MOBA_BLOCK)
            kb = k_ref[rows, :]
            kext_ref[rows, :HEAD_DIM] = kb
            kext_ref[rows, HEAD_DIM:] = (lane == j).astype(BF16)
            kmean_ref[j:j + 1, :] = jnp.mean(kb.astype(F32), axis=0, keepdims=True)

    q = q_ref[...]
    gate = lax.dot_general(q, kmean_ref[...].astype(BF16), contract_last, preferred_element_type=F32)
    lane = lax.broadcasted_iota(jnp.int32, (tile, LANES), 1)
    row = lax.broadcasted_iota(jnp.int32, (tile, LANES), 0)
    own_block = qi * blocks_per_tile + row // MOBA_BLOCK
    gate = jnp.where(lane < own_block, gate, NEG_INF)
    allowed = lane == own_block
    for _ in range(MOBA_TOPK):
        best = jnp.max(gate, axis=1, keepdims=True)
        first = jnp.min(jnp.where(gate == best, lane, LANES), axis=1, keepdims=True)
        hit = lane == first
        allowed = allowed | (hit & (best > 0.5 * NEG_INF))
        gate = jnp.where(hit, REMOVED, gate)
    bias = jnp.where(allowed, 0.0, NEG_INF).astype(BF16)
    q_ext = jnp.concatenate([q, bias], axis=1)

    def scores(j):
        keys = kext_ref[pl.ds(pl.multiple_of(j * tile, tile), tile), :]
        return lax.dot_general(q_ext, keys, contract_last, preferred_element_type=F32)

    def values(j):
        return v_ref[pl.ds(pl.multiple_of(j * tile, tile), tile), :]

    r = lax.broadcasted_iota(jnp.int32, (tile, tile), 0)
    c = lax.broadcasted_iota(jnp.int32, (tile, tile), 1)
    s = jnp.where(c <= r, scores(qi), NEG_INF)
    m0 = jnp.max(s, axis=1, keepdims=True)
    p = jnp.exp(s - m0)
    l0 = jnp.sum(p, axis=1, keepdims=True)
    acc0 = jnp.dot(p.astype(BF16), values(qi), preferred_element_type=F32)

    def body(j, carry):
        m, l, acc = carry
        s = scores(j)
        m_new = jnp.maximum(m, jnp.max(s, axis=1, keepdims=True))
        p = jnp.exp(s - m_new)
        alpha = jnp.exp(m - m_new)
        l = alpha * l + jnp.sum(p, axis=1, keepdims=True)
        acc = alpha * acc + jnp.dot(p.astype(BF16), values(j), preferred_element_type=F32)
        return m_new, l, acc

    _, l, acc = lax.fori_loop(0, qi, body, (m0, l0, acc0))
    o_ref[...] = (acc / l).astype(o_ref.dtype)


def _moba(z, *, batch, seq, tile=512):
    t = z.shape[0]
    n_blocks = seq // MOBA_BLOCK
    tiles = seq // tile
    kern = functools.partial(_moba_kernel, tile=tile, n_blocks=n_blocks)
    return pl.pallas_call(
        kern,
        grid=(batch, MOBA_HEADS, tiles),
        in_specs=[pl.BlockSpec((tile, HEAD_DIM), lambda b, h, i: (b * tiles + i, h)),
                  pl.BlockSpec((seq, HEAD_DIM), lambda b, h, i: (b, MOBA_HEADS + h)),
                  pl.BlockSpec((seq, HEAD_DIM), lambda b, h, i: (b, 2 * MOBA_HEADS + h))],
        out_specs=pl.BlockSpec((tile, HEAD_DIM), lambda b, h, i: (b * tiles + i, h)),
        out_shape=jax.ShapeDtypeStruct((t, MOBA_WIDTH), BF16),
        scratch_shapes=[pltpu.VMEM((seq, 2 * HEAD_DIM), BF16),
                        pltpu.VMEM((LANES, HEAD_DIM), F32)],
        compiler_params=_cparams(("arbitrary", "arbitrary", "arbitrary")),
        name="moba_attention",
    )(z, z, z)


def _sgu_kernel(u_ref, v_ref, g_ref, b_ref, w_ref, bs_ref, o_ref, *, rows):
    v = v_ref[...].astype(F32)
    mu = jnp.mean(v, axis=-1, keepdims=True)
    var = jnp.mean(jnp.square(v - mu), axis=-1, keepdims=True)
    vln = ((v - mu) * lax.rsqrt(var + LN_EPS) * g_ref[...] + b_ref[...]).astype(BF16)
    t_idx = lax.broadcasted_iota(jnp.int32, (SGU_CHUNK, SGU_CHUNK), 0)
    s_idx = lax.broadcasted_iota(jnp.int32, (SGU_CHUNK, SGU_CHUNK), 1)
    causal = s_idx <= t_idx
    for g in range(SGU_GROUPS):
        w = jnp.where(causal, w_ref[g], 0.0).astype(BF16)
        bias = bs_ref[:, g:g + 1]
        cols = slice(g * LANES, (g + 1) * LANES)
        for ch in range(rows // SGU_CHUNK):
            rws = slice(ch * SGU_CHUNK, (ch + 1) * SGU_CHUNK)
            mixed = jnp.dot(w, vln[rws, cols], preferred_element_type=F32) + bias
            o_ref[rws, cols] = (u_ref[rws, cols].astype(F32) * mixed).astype(o_ref.dtype)


def _sgu(z, ln_g, ln_b, w_s, b_s_t, *, width, rows=512):
    t = z.shape[0]
    u_blk = 3 * MOBA_WIDTH // width
    kern = functools.partial(_sgu_kernel, rows=rows)
    return pl.pallas_call(
        kern,
        grid=(t // rows,),
        in_specs=[pl.BlockSpec((rows, width), lambda i: (i, u_blk)),
                  pl.BlockSpec((rows, width), lambda i: (i, u_blk + 1)),
                  pl.BlockSpec((1, width), lambda i: (0, 0)),
                  pl.BlockSpec((1, width), lambda i: (0, 0)),
                  pl.BlockSpec((SGU_GROUPS, SGU_CHUNK, SGU_CHUNK), lambda i: (0, 0, 0)),
                  pl.BlockSpec((SGU_CHUNK, SGU_GROUPS), lambda i: (0, 0))],
        out_specs=pl.BlockSpec((rows, width), lambda i: (i, 0)),
        out_shape=jax.ShapeDtypeStruct((t, width), BF16),
        compiler_params=_cparams(("arbitrary",)),
        name="spatial_gating",
    )(z, z, ln_g.reshape(1, width), ln_b.reshape(1, width), w_s, b_s_t)


def _merge_kernel(oa_ref, ob_ref, pa_ref, pb_ref, ga_ref, gb_ref, o_ref):
    a = jnp.dot(oa_ref[...], pa_ref[...], preferred_element_type=F32)
    b = jnp.dot(ob_ref[...], pb_ref[...], preferred_element_type=F32)
    o_ref[...] = (ga_ref[...].astype(F32) * a + gb_ref[...].astype(F32) * b).astype(o_ref.dtype)


def _merge(o_a, o_b, p_a, p_b, z, *, gate_col, bm=1024, bn=1024):
    t = o_a.shape[0]
    d = p_a.shape[1]
    ga_blk = gate_col // bn
    gb_blk = (gate_col + d) // bn
    return pl.pallas_call(
        _merge_kernel,
        grid=(t // bm, d // bn),
        in_specs=[pl.BlockSpec((bm, o_a.shape[1]), lambda m, n: (m, 0)),
                  pl.BlockSpec((bm, o_b.shape[1]), lambda m, n: (m, 0)),
                  pl.BlockSpec((p_a.shape[0], bn), lambda m, n: (0, n)),
                  pl.BlockSpec((p_b.shape[0], bn), lambda m, n: (0, n)),
                  pl.BlockSpec((bm, bn), lambda m, n: (m, ga_blk + n)),
                  pl.BlockSpec((bm, bn), lambda m, n: (m, gb_blk + n))],
        out_specs=pl.BlockSpec((bm, bn), lambda m, n: (m, n)),
        out_shape=jax.ShapeDtypeStruct((t, d), BF16),
        compiler_params=_cparams(("arbitrary", "arbitrary")),
        name="branch_merge",
    )(o_a, o_b, p_a, p_b, z, z)


def _outproj_kernel(a_ref, w_ref, x_ref, o_ref):
    o_ref[...] = x_ref[...] + jnp.dot(a_ref[...], w_ref[...], preferred_element_type=F32)


def _outproj(a, w, x, *, bm=1024, bn=512):
    t, k = a.shape
    d = w.shape[1]
    return pl.pallas_call(
        _outproj_kernel,
        grid=(t // bm, d // bn),
        in_specs=[pl.BlockSpec((bm, k), lambda m, n: (m, 0)),
                  pl.BlockSpec((k, bn), lambda m, n: (0, n)),
                  pl.BlockSpec((bm, bn), lambda m, n: (m, n))],
        out_specs=pl.BlockSpec((bm, bn), lambda m, n: (m, n)),
        out_shape=jax.ShapeDtypeStruct((t, d), F32),
        compiler_params=_cparams(("arbitrary", "arbitrary")),
        name="out_proj",
    )(a, w, x)


def _memkv_kernel(mem_ref, g_ref, w_ref, o_ref):
    mem_n = _rmsnorm_rows(mem_ref[...], g_ref[...]).astype(BF16)
    o_ref[...] = jnp.dot(mem_n, w_ref[...], preferred_element_type=F32).astype(o_ref.dtype)


def _memkv(mem, g, w, *, n_mem):
    t, d = mem.shape
    cols = w.shape[1]
    return pl.pallas_call(
        _memkv_kernel,
        grid=(t // n_mem,),
        in_specs=[pl.BlockSpec((n_mem, d), lambda i: (i, 0)),
                  pl.BlockSpec((1, d), lambda i: (0, 0)),
                  pl.BlockSpec((d, cols), lambda i: (0, 0))],
        out_specs=pl.BlockSpec((n_mem, cols), lambda i: (i, 0)),
        out_shape=jax.ShapeDtypeStruct((t, cols), BF16),
        compiler_params=_cparams(("arbitrary",)),
        name="mem_kv",
    )(mem, g.reshape(1, d), w)


def _xattn_kernel(x_ref, gx_ref, wq_ref, kv_ref, wo_ref, gf_ref, x_out_ref, h_out_ref):
    x = x_ref[...]
    h = _rmsnorm_rows(x, gx_ref[...]).astype(BF16)
    q = jnp.dot(h, wq_ref[...], preferred_element_type=F32) * (HEAD_DIM ** -0.5)
    q = q.astype(BF16)
    outs = []
    for hh in range(XATTN_HEADS):
        cols = slice(hh * HEAD_DIM, (hh + 1) * HEAD_DIM)
        k = kv_ref[:, hh * HEAD_DIM:(hh + 1) * HEAD_DIM]
        v = kv_ref[:, XATTN_WIDTH + hh * HEAD_DIM:XATTN_WIDTH + (hh + 1) * HEAD_DIM]
        s = lax.dot_general(q[:, cols], k, (((1,), (1,)), ((), ())), preferred_element_type=F32)
        s = s - jnp.max(s, axis=-1, keepdims=True)
        p = jnp.exp(s)
        p = p / jnp.sum(p, axis=-1, keepdims=True)
        outs.append(jnp.dot(p.astype(BF16), v, preferred_element_type=F32).astype(BF16))
    o = jnp.concatenate(outs, axis=1)
    x2 = x + jnp.dot(o, wo_ref[...], preferred_element_type=F32)
    x_out_ref[...] = x2
    h_out_ref[...] = _rmsnorm_rows(x2, gf_ref[...]).astype(h_out_ref.dtype)


def _xattn(x, g_x, w_q, kv, w_o, g_ffn, *, seq, n_mem, bm=256):
    t, d = x.shape
    per_batch = seq // bm
    return pl.pallas_call(
        _xattn_kernel,
        grid=(t // bm,),
        in_specs=[pl.BlockSpec((bm, d), lambda i: (i, 0)),
                  pl.BlockSpec((1, d), lambda i: (0, 0)),
                  pl.BlockSpec(w_q.shape, lambda i: (0, 0)),
                  pl.BlockSpec((n_mem, kv.shape[1]), lambda i: (i // per_batch, 0)),
                  pl.BlockSpec(w_o.shape, lambda i: (0, 0)),
                  pl.BlockSpec((1, d), lambda i: (0, 0))],
        out_specs=[pl.BlockSpec((bm, d), lambda i: (i, 0)),
                   pl.BlockSpec((bm, d), lambda i: (i, 0))],
        out_shape=[jax.ShapeDtypeStruct((t, d), F32),
                   jax.ShapeDtypeStruct((t, d), BF16)],
        compiler_params=_cparams(("arbitrary",)),
        name="mem_cross_attention",
    )(x, g_x.reshape(1, d), w_q, kv, w_o, g_ffn.reshape(1, d))


def _ffn_up_kernel(h_ref, wg_ref, wu_ref, o_ref):
    h = h_ref[...]
    g = jnp.dot(h, wg_ref[...], preferred_element_type=F32)
    u = jnp.dot(h, wu_ref[...], preferred_element_type=F32)
    o_ref[...] = (g * _sigmoid(g) * u).astype(o_ref.dtype)


def _ffn_up(h, w_g, w_u, *, bm=1024, bn=512):
    t, d = h.shape
    f = w_g.shape[1]
    return pl.pallas_call(
        _ffn_up_kernel,
        grid=(t // bm, f // bn),
        in_specs=[pl.BlockSpec((bm, d), lambda m, n: (m, 0)),
                  pl.BlockSpec((d, bn), lambda m, n: (0, n)),
                  pl.BlockSpec((d, bn), lambda m, n: (0, n))],
        out_specs=pl.BlockSpec((bm, bn), lambda m, n: (m, n)),
        out_shape=jax.ShapeDtypeStruct((t, f), BF16),
        compiler_params=_cparams(("arbitrary", "arbitrary")),
        name="ffn_up",
    )(h, w_g, w_u)


def _ffn_down_kernel(a_ref, w_ref, x_ref, g_ref, o_ref, *, final_norm):
    k = pl.program_id(1)
    part = jnp.dot(a_ref[...], w_ref[...], preferred_element_type=F32)

    @pl.when(k == 0)
    def _first():
        o_ref[...] = x_ref[...] + part

    @pl.when(k > 0)
    def _rest():
        o_ref[...] += part

    if final_norm:
        @pl.when(k == pl.num_programs(1) - 1)
        def _final_norm():
            o_ref[...] = _rmsnorm_rows(o_ref[...], g_ref[...])


def _ffn_down(a, w, x, g, *, final_norm, bm=256, bk=1024):
    t, f = a.shape
    d = w.shape[1]
    return pl.pallas_call(
        functools.partial(_ffn_down_kernel, final_norm=final_norm),
        grid=(t // bm, f // bk),
        in_specs=[pl.BlockSpec((bm, bk), lambda m, k: (m, k)),
                  pl.BlockSpec((bk, d), lambda m, k: (k, 0)),
                  pl.BlockSpec((bm, d), lambda m, k: (m, 0)),
                  pl.BlockSpec((1, d), lambda m, k: (0, 0))],
        out_specs=pl.BlockSpec((bm, d), lambda m, k: (m, 0)),
        out_shape=jax.ShapeDtypeStruct((t, d), F32),
        compiler_params=_cparams(("arbitrary", "arbitrary")),
        name="ffn_down",
    )(a, w, x, g.reshape(1, d))


@jax.jit
def kernel(x, mem, norm_mix_g, w_in, sgu_ln_g, sgu_ln_b, w_sgu, b_sgu, w_branch_a, w_branch_b, w_out, norm_xattn_g, norm_mem_g, w_xq, w_xkv, w_xo, norm_ffn_g, w_ff_gate, w_ff_up, w_ff_down, norm_final_g):
    batch, seq, d = x.shape
    n_mem = mem.shape[1]
    depth = w_in.shape[0]
    sgu_width = sgu_ln_g.shape[1]
    d_ff = w_ff_gate.shape[2]
    ff_pad = (-d_ff) % FF_PAD
    gate_col = 3 * MOBA_WIDTH + 2 * sgu_width

    xt = x.reshape(batch * seq, d)
    memt = mem.reshape(batch * n_mem, d)
    cos, sin = _rope_table(seq)

    for l in range(depth):
        h = _rmsnorm(xt, norm_mix_g[l], BF16)
        z = _inproj(h, w_in[l].astype(BF16), cos, sin, seq=seq, sgu_width=sgu_width)
        o_a = _moba(z, batch=batch, seq=seq)
        o_b = _sgu(z, sgu_ln_g[l], sgu_ln_b[l], w_sgu[l], jnp.transpose(b_sgu[l]), width=sgu_width)
        merged = _merge(o_a, o_b, w_branch_a[l].astype(BF16), w_branch_b[l].astype(BF16), z,
                        gate_col=gate_col)
        xt = _outproj(merged, w_out[l].astype(BF16), xt)

        kv = _memkv(memt, norm_mem_g[l], w_xkv[l].astype(BF16), n_mem=n_mem)
        xt, h = _xattn(xt, norm_xattn_g[l], w_xq[l].astype(BF16), kv, w_xo[l].astype(BF16),
                       norm_ffn_g[l], seq=seq, n_mem=n_mem)

        w_g = jnp.pad(w_ff_gate[l].astype(BF16), ((0, 0), (0, ff_pad)))
        w_u = jnp.pad(w_ff_up[l].astype(BF16), ((0, 0), (0, ff_pad)))
        w_d = jnp.pad(w_ff_down[l].astype(BF16), ((0, ff_pad), (0, 0)))
        a = _ffn_up(h, w_g, w_u)
        xt = _ffn_down(a, w_d, xt, norm_final_g, final_norm=(l == depth - 1))
    return xt.reshape(batch, seq, d)
```

```python
import functools
import math

import jax
import jax.numpy as jnp
from jax import lax
from jax.experimental import pallas as pl
from jax.experimental.pallas import tpu as pltpu

F32 = jnp.float32
BF16 = jnp.bfloat16

HEAD_DIM = 128
MOBA_HEADS = 16
MOBA_WIDTH = MOBA_HEADS * HEAD_DIM
MOBA_BLOCK = 256
MOBA_TOPK = 3
SGU_CHUNK = 128
SGU_GROUPS = 16
XATTN_HEADS = 4
XATTN_WIDTH = XATTN_HEADS * HEAD_DIM
RMS_EPS = 1e-6
LN_EPS = 1e-5
ROPE_THETA = 10000.0
NEG_INF = -1e30
REMOVED = -3e38
Q_SCALE = HEAD_DIM ** -0.5 * math.log2(math.e)

LANES = 128
FF_PAD = 512
VMEM_LIMIT = 56 * 1024 * 1024


def _cparams(sem):
    return pltpu.CompilerParams(dimension_semantics=sem, vmem_limit_bytes=VMEM_LIMIT)


def _rmsnorm_rows(x, g):
    ms = jnp.mean(x * x, axis=-1, keepdims=True)
    return x * lax.rsqrt(ms + RMS_EPS) * g


def _rmsnorm_kernel(x_ref, g_ref, o_ref):
    o_ref[...] = _rmsnorm_rows(x_ref[...], g_ref[...]).astype(o_ref.dtype)


def _rmsnorm(x, g, out_dtype, bm=256):
    t, d = x.shape
    return pl.pallas_call(
        _rmsnorm_kernel,
        grid=(t // bm,),
        in_specs=[pl.BlockSpec((bm, d), lambda i: (i, 0)),
                  pl.BlockSpec((1, d), lambda i: (0, 0))],
        out_specs=pl.BlockSpec((bm, d), lambda i: (i, 0)),
        out_shape=jax.ShapeDtypeStruct((t, d), out_dtype),
        compiler_params=_cparams(("arbitrary",)),
        name="rmsnorm",
    )(x, g.reshape(1, d))


def _rope_kernel(freq_ref, cos_ref, sin_ref, *, rows):
    i = pl.program_id(0)
    pos = (i * rows + lax.broadcasted_iota(jnp.int32, (rows, LANES), 0)).astype(F32)
    lane = lax.broadcasted_iota(jnp.int32, (rows, LANES), 1)
    ang = pos * freq_ref[...]
    cos_ref[...] = jnp.cos(ang)
    s = jnp.sin(ang)
    sin_ref[...] = jnp.where(lane < HEAD_DIM // 2, -s, s)


def _rope_table(seq, rows=1024):
    half = HEAD_DIM // 2
    inv_freq = jnp.power(ROPE_THETA, -(jnp.arange(half, dtype=F32) * 2.0 / HEAD_DIM))
    freq = jnp.concatenate([inv_freq, inv_freq]).reshape(1, HEAD_DIM)
    return pl.pallas_call(
        functools.partial(_rope_kernel, rows=rows),
        grid=(seq // rows,),
        in_specs=[pl.BlockSpec((1, HEAD_DIM), lambda i: (0, 0))],
        out_specs=[pl.BlockSpec((rows, HEAD_DIM), lambda i: (i, 0))] * 2,
        out_shape=[jax.ShapeDtypeStruct((seq, HEAD_DIM), F32)] * 2,
        compiler_params=_cparams(("arbitrary",)),
        name="rope_table",
    )(freq)


def _gelu_tanh(x):
    c = math.sqrt(2.0 / math.pi)
    return x * (0.5 * (1.0 + jnp.tanh(c * (x + 0.044715 * (x * x * x)))))


def _sigmoid(x):
    return 1.0 / (1.0 + jnp.exp(-x))


def _inproj_kernel(h_ref, w_ref, cos_ref, sin_ref, z_ref, acc_ref, *, bn, n_q, n_k, n_v, n_act):
    n = pl.program_id(1)
    acc_ref[...] = jnp.dot(h_ref[...], w_ref[...], preferred_element_type=F32)

    @pl.when(n < n_k)
    def _rotary():
        scale = jnp.where(n < n_q, Q_SCALE, 1.0).astype(F32)
        cos = cos_ref[...] * scale
        sin = sin_ref[...] * scale
        for hh in range(bn // HEAD_DIM):
            blk = acc_ref[:, hh * HEAD_DIM:(hh + 1) * HEAD_DIM]
            rot = pltpu.roll(blk, HEAD_DIM // 2, axis=1)
            z_ref[:, hh * HEAD_DIM:(hh + 1) * HEAD_DIM] = (blk * cos + rot * sin).astype(z_ref.dtype)

    @pl.when((n >= n_k) & (n < n_v))
    def _plain():
        z_ref[...] = acc_ref[...].astype(z_ref.dtype)

    @pl.when((n >= n_v) & (n < n_act))
    def _gelu():
        z_ref[...] = _gelu_tanh(acc_ref[...]).astype(z_ref.dtype)

    @pl.when(n >= n_act)
    def _gate():
        z_ref[...] = _sigmoid(acc_ref[...]).astype(z_ref.dtype)


def _inproj(h, w, cos, sin, *, seq, sgu_width, bm=1024, bn=1024):
    t, d = h.shape
    cols = w.shape[1]
    seq_blocks = seq // bm
    n_q = MOBA_WIDTH // bn
    n_k = 2 * MOBA_WIDTH // bn
    n_v = 3 * MOBA_WIDTH // bn
    n_act = (3 * MOBA_WIDTH + 2 * sgu_width) // bn
    kern = functools.partial(_inproj_kernel, bn=bn, n_q=n_q, n_k=n_k, n_v=n_v, n_act=n_act)
    return pl.pallas_call(
        kern,
        grid=(t // bm, cols // bn),
        in_specs=[pl.BlockSpec((bm, d), lambda m, n: (m, 0)),
                  pl.BlockSpec((d, bn), lambda m, n: (0, n)),
                  pl.BlockSpec((bm, HEAD_DIM), lambda m, n: (m % seq_blocks, 0)),
                  pl.BlockSpec((bm, HEAD_DIM), lambda m, n: (m % seq_blocks, 0))],
        out_specs=pl.BlockSpec((bm, bn), lambda m, n: (m, n)),
        out_shape=jax.ShapeDtypeStruct((t, cols), BF16),
        scratch_shapes=[pltpu.VMEM((bm, bn), F32)],
        compiler_params=_cparams(("arbitrary", "arbitrary")),
        name="in_proj",
    )(h, w, cos, sin)


def _moba_kernel(q_ref, k_ref, v_ref, o_ref, kext_ref, vext_ref, kmean_ref, *, tile, n_blocks, n_sub):
    qi = pl.program_id(2)
    blocks_per_tile = tile // MOBA_BLOCK
    contract_last = (((1,), (1,)), ((), ()))
    sub = tile // n_sub

    @pl.when(qi == 0)
    def _prepare_keys():
        kmean_ref[...] = jnp.zeros_like(kmean_ref)
        lane = lax.broadcasted_iota(jnp.int32, (MOBA_BLOCK, LANES), 1)
        ones_col = (lane == 0).astype(BF16)
        for j in range(n_blocks):
            rows = pl.ds(j * MOBA_BLOCK, MOBA_BLOCK)
            kb = k_ref[rows, :]
            kext_ref[rows, :HEAD_DIM] = kb
            kext_ref[rows, HEAD_DIM:] = (lane == j).astype(BF16)
            vext_ref[rows, :HEAD_DIM] = v_ref[rows, :]
            vext_ref[rows, HEAD_DIM:] = ones_col
            kmean_ref[j:j + 1, :] = jnp.mean(kb.astype(F32), axis=0, keepdims=True)

    q = q_ref[...]
    gate = lax.dot_general(kmean_ref[...].astype(BF16), q, contract_last,
                           preferred_element_type=F32)[:n_blocks, :]
    blk = lax.broadcasted_iota(jnp.int32, (n_blocks, tile), 0)
    col = lax.broadcasted_iota(jnp.int32, (n_blocks, tile), 1)
    own_block = qi * blocks_per_tile + col // MOBA_BLOCK
    gate = jnp.where(blk < own_block, gate, NEG_INF)
    allowed = blk == own_block
    for _ in range(MOBA_TOPK):
        best = jnp.max(gate, axis=0, keepdims=True)
        first = jnp.min(jnp.where(gate == best, blk, n_blocks), axis=0, keepdims=True)
        hit = blk == first
        allowed = allowed | (hit & (best > 0.5 * NEG_INF))
        gate = jnp.where(hit, REMOVED, gate)
    bias_t = jnp.where(allowed, 0.0, NEG_INF)
    bias_t = jnp.concatenate([bias_t, jnp.zeros((LANES - n_blocks, tile), F32)], axis=0)
    bias = jnp.transpose(bias_t).astype(BF16)
    q_ext = jnp.concatenate([q, bias], axis=1)

    def keys(j):
        return kext_ref[pl.ds(pl.multiple_of(j * tile, tile), tile), :]

    def values(j):
        return vext_ref[pl.ds(pl.multiple_of(j * tile, tile), tile), :]

    def scores(i, k_tile):
        return lax.dot_general(q_ext[i * sub:(i + 1) * sub], k_tile, contract_last,
                               preferred_element_type=F32)

    k_tile, v_tile = keys(qi), values(qi)
    carry0 = []
    for i in range(n_sub):
        r = i * sub + lax.broadcasted_iota(jnp.int32, (sub, tile), 0)
        c = lax.broadcasted_iota(jnp.int32, (sub, tile), 1)
        s = jnp.where(c <= r, scores(i, k_tile), NEG_INF)
        m0 = jnp.max(s, axis=1, keepdims=True)
        p = jnp.exp2(s - m0)
        carry0 += [m0, jnp.dot(p.astype(BF16), v_tile, preferred_element_type=F32)]

    def body(j, carry):
        k_tile, v_tile = keys(j), values(j)
        out = []
        for i in range(n_sub):
            m, acc = carry[2 * i], carry[2 * i + 1]
            s = scores(i, k_tile)
            m_new = jnp.maximum(m, jnp.max(s, axis=1, keepdims=True))
            p = jnp.exp2(s - m_new)
            acc = jnp.exp2(m - m_new) * acc + jnp.dot(p.astype(BF16), v_tile, preferred_element_type=F32)
            out += [m_new, acc]
        return tuple(out)

    carry = lax.fori_loop(0, qi, body, tuple(carry0))
    for i in range(n_sub):
        acc = carry[2 * i + 1]
        o_ref[i * sub:(i + 1) * sub, :] = (acc[:, :HEAD_DIM] / acc[:, HEAD_DIM:HEAD_DIM + 1]).astype(o_ref.dtype)


def _moba(z, *, batch, seq, tile=1024, n_sub=4):
    t = z.shape[0]
    n_blocks = seq // MOBA_BLOCK
    tiles = seq // tile
    kern = functools.partial(_moba_kernel, tile=tile, n_blocks=n_blocks, n_sub=n_sub)
    return pl.pallas_call(
        kern,
        grid=(batch, MOBA_HEADS, tiles),
        in_specs=[pl.BlockSpec((tile, HEAD_DIM), lambda b, h, i: (b * tiles + i, h)),
                  pl.BlockSpec((seq, HEAD_DIM), lambda b, h, i: (b, MOBA_HEADS + h)),
                  pl.BlockSpec((seq, HEAD_DIM), lambda b, h, i: (b, 2 * MOBA_HEADS + h))],
        out_specs=pl.BlockSpec((tile, HEAD_DIM), lambda b, h, i: (b * tiles + i, h)),
        out_shape=jax.ShapeDtypeStruct((t, MOBA_WIDTH), BF16),
        scratch_shapes=[pltpu.VMEM((seq, 2 * HEAD_DIM), BF16),
                        pltpu.VMEM((seq, 2 * HEAD_DIM), BF16),
                        pltpu.VMEM((LANES, HEAD_DIM), F32)],
        compiler_params=_cparams(("arbitrary", "arbitrary", "arbitrary")),
        name="moba_attention",
    )(z, z, z)


def _sgu_kernel(u_ref, v_ref, g_ref, b_ref, w_ref, bs_ref, o_ref, *, rows):
    v = v_ref[...].astype(F32)
    mu = jnp.mean(v, axis=-1, keepdims=True)
    var = jnp.mean(jnp.square(v - mu), axis=-1, keepdims=True)
    vln = ((v - mu) * lax.rsqrt(var + LN_EPS) * g_ref[...] + b_ref[...]).astype(BF16)
    t_idx = lax.broadcasted_iota(jnp.int32, (SGU_CHUNK, SGU_CHUNK), 0)
    s_idx = lax.broadcasted_iota(jnp.int32, (SGU_CHUNK, SGU_CHUNK), 1)
    causal = s_idx <= t_idx
    for g in range(SGU_GROUPS):
        w = jnp.where(causal, w_ref[g], 0.0).astype(BF16)
        bias = bs_ref[:, g:g + 1]
        cols = slice(g * LANES, (g + 1) * LANES)
        for ch in range(rows // SGU_CHUNK):
            rws = slice(ch * SGU_CHUNK, (ch + 1) * SGU_CHUNK)
            mixed = jnp.dot(w, vln[rws, cols], preferred_element_type=F32) + bias
            o_ref[rws, cols] = (u_ref[rws, cols].astype(F32) * mixed).astype(o_ref.dtype)


def _sgu(z, ln_g, ln_b, w_s, b_s_t, *, width, rows=512):
    t = z.shape[0]
    u_blk = 3 * MOBA_WIDTH // width
    kern = functools.partial(_sgu_kernel, rows=rows)
    return pl.pallas_call(
        kern,
        grid=(t // rows,),
        in_specs=[pl.BlockSpec((rows, width), lambda i: (i, u_blk)),
                  pl.BlockSpec((rows, width), lambda i: (i, u_blk + 1)),
                  pl.BlockSpec((1, width), lambda i: (0, 0)),
                  pl.BlockSpec((1, width), lambda i: (0, 0)),
                  pl.BlockSpec((SGU_GROUPS, SGU_CHUNK, SGU_CHUNK), lambda i: (0, 0, 0)),
                  pl.BlockSpec((SGU_CHUNK, SGU_GROUPS), lambda i: (0, 0))],
        out_specs=pl.BlockSpec((rows, width), lambda i: (i, 0)),
        out_shape=jax.ShapeDtypeStruct((t, width), BF16),
        compiler_params=_cparams(("arbitrary",)),
        name="spatial_gating",
    )(z, z, ln_g.reshape(1, width), ln_b.reshape(1, width), w_s, b_s_t)


def _merge_kernel(oa_ref, ob_ref, pa_ref, pb_ref, ga_ref, gb_ref, o_ref):
    a = jnp.dot(oa_ref[...], pa_ref[...], preferred_element_type=F32)
    b = jnp.dot(ob_ref[...], pb_ref[...], preferred_element_type=F32)
    o_ref[...] = (ga_ref[...].astype(F32) * a + gb_ref[...].astype(F32) * b).astype(o_ref.dtype)


def _merge(o_a, o_b, p_a, p_b, z, *, gate_col, bm=1024, bn=1024):
    t = o_a.shape[0]
    d = p_a.shape[1]
    ga_blk = gate_col // bn
    gb_blk = (gate_col + d) // bn
    return pl.pallas_call(
        _merge_kernel,
        grid=(t // bm, d // bn),
        in_specs=[pl.BlockSpec((bm, o_a.shape[1]), lambda m, n: (m, 0)),
                  pl.BlockSpec((bm, o_b.shape[1]), lambda m, n: (m, 0)),
                  pl.BlockSpec((p_a.shape[0], bn), lambda m, n: (0, n)),
                  pl.BlockSpec((p_b.shape[0], bn), lambda m, n: (0, n)),
                  pl.BlockSpec((bm, bn), lambda m, n: (m, ga_blk + n)),
                  pl.BlockSpec((bm, bn), lambda m, n: (m, gb_blk + n))],
        out_specs=pl.BlockSpec((bm, bn), lambda m, n: (m, n)),
        out_shape=jax.ShapeDtypeStruct((t, d), BF16),
        compiler_params=_cparams(("arbitrary", "arbitrary")),
        name="branch_merge",
    )(o_a, o_b, p_a, p_b, z, z)


def _outproj_kernel(a_ref, w_ref, x_ref, o_ref):
    o_ref[...] = x_ref[...] + jnp.dot(a_ref[...], w_ref[...], preferred_element_type=F32)


def _outproj(a, w, x, *, bm=1024, bn=512):
    t, k = a.shape
    d = w.shape[1]
    return pl.pallas_call(
        _outproj_kernel,
        grid=(t // bm, d // bn),
        in_specs=[pl.BlockSpec((bm, k), lambda m, n: (m, 0)),
                  pl.BlockSpec((k, bn), lambda m, n: (0, n)),
                  pl.BlockSpec((bm, bn), lambda m, n: (m, n))],
        out_specs=pl.BlockSpec((bm, bn), lambda m, n: (m, n)),
        out_shape=jax.ShapeDtypeStruct((t, d), F32),
        compiler_params=_cparams(("arbitrary", "arbitrary")),
        name="out_proj",
    )(a, w, x)


def _memkv_kernel(mem_ref, g_ref, w_ref, o_ref):
    mem_n = _rmsnorm_rows(mem_ref[...], g_ref[...]).astype(BF16)
    o_ref[...] = jnp.dot(mem_n, w_ref[...], preferred_element_type=F32).astype(o_ref.dtype)


def _memkv(mem, g, w, *, n_mem):
    t, d = mem.shape
    cols = w.shape[1]
    return pl.pallas_call(
        _memkv_kernel,
        grid=(t // n_mem,),
        in_specs=[pl.BlockSpec((n_mem, d), lambda i: (i, 0)),
                  pl.BlockSpec((1, d), lambda i: (0, 0)),
                  pl.BlockSpec((d, cols), lambda i: (0, 0))],
        out_specs=pl.BlockSpec((n_mem, cols), lambda i: (i, 0)),
        out_shape=jax.ShapeDtypeStruct((t, cols), BF16),
        compiler_params=_cparams(("arbitrary",)),
        name="mem_kv",
    )(mem, g.reshape(1, d), w)


def _xattn_kernel(x_ref, gx_ref, wq_ref, kv_ref, wo_ref, gf_ref, x_out_ref, h_out_ref):
    x = x_ref[...]
    h = _rmsnorm_rows(x, gx_ref[...]).astype(BF16)
    q = jnp.dot(h, wq_ref[...], preferred_element_type=F32) * (HEAD_DIM ** -0.5)
    q = q.astype(BF16)
    outs = []
    for hh in range(XATTN_HEADS):
        cols = slice(hh * HEAD_DIM, (hh + 1) * HEAD_DIM)
        k = kv_ref[:, hh * HEAD_DIM:(hh + 1) * HEAD_DIM]
        v = kv_ref[:, XATTN_WIDTH + hh * HEAD_DIM:XATTN_WIDTH + (hh + 1) * HEAD_DIM]
        s = lax.dot_general(q[:, cols], k, (((1,), (1,)), ((), ())), preferred_element_type=F32)
        s = s - jnp.max(s, axis=-1, keepdims=True)
        p = jnp.exp(s)
        p = p / jnp.sum(p, axis=-1, keepdims=True)
        outs.append(jnp.dot(p.astype(BF16), v, preferred_element_type=F32).astype(BF16))
    o = jnp.concatenate(outs, axis=1)
    x2 = x + jnp.dot(o, wo_ref[...], preferred_element_type=F32)
    x_out_ref[...] = x2
    h_out_ref[...] = _rmsnorm_rows(x2, gf_ref[...]).astype(h_out_ref.dtype)


def _xattn(x, g_x, w_q, kv, w_o, g_ffn, *, seq, n_mem, bm=256):
    t, d = x.shape
    per_batch = seq // bm
    return pl.pallas_call(
        _xattn_kernel,
        grid=(t // bm,),
        in_specs=[pl.BlockSpec((bm, d), lambda i: (i, 0)),
                  pl.BlockSpec((1, d), lambda i: (0, 0)),
                  pl.BlockSpec(w_q.shape, lambda i: (0, 0)),
                  pl.BlockSpec((n_mem, kv.shape[1]), lambda i: (i // per_batch, 0)),
                  pl.BlockSpec(w_o.shape, lambda i: (0, 0)),
                  pl.BlockSpec((1, d), lambda i: (0, 0))],
        out_specs=[pl.BlockSpec((bm, d), lambda i: (i, 0)),
                   pl.BlockSpec((bm, d), lambda i: (i, 0))],
        out_shape=[jax.ShapeDtypeStruct((t, d), F32),
                   jax.ShapeDtypeStruct((t, d), BF16)],
        compiler_params=_cparams(("arbitrary",)),
        name="mem_cross_attention",
    )(x, g_x.reshape(1, d), w_q, kv, w_o, g_ffn.reshape(1, d))


def _ffn_up_kernel(h_ref, wg_ref, wu_ref, o_ref):
    h = h_ref[...]
    g = jnp.dot(h, wg_ref[...], preferred_element_type=F32)
    u = jnp.dot(h, wu_ref[...], preferred_element_type=F32)
    o_ref[...] = (g * _sigmoid(g) * u).astype(o_ref.dtype)


def _ffn_up(h, w_g, w_u, *, bm=1024, bn=512):
    t, d = h.shape
    f = w_g.shape[1]
    return pl.pallas_call(
        _ffn_up_kernel,
        grid=(t // bm, f // bn),
        in_specs=[pl.BlockSpec((bm, d), lambda m, n: (m, 0)),
                  pl.BlockSpec((d, bn), lambda m, n: (0, n)),
                  pl.BlockSpec((d, bn), lambda m, n: (0, n))],
        out_specs=pl.BlockSpec((bm, bn), lambda m, n: (m, n)),
        out_shape=jax.ShapeDtypeStruct((t, f), BF16),
        compiler_params=_cparams(("arbitrary", "arbitrary")),
        name="ffn_up",
    )(h, w_g, w_u)


def _ffn_down_kernel(a_ref, w_ref, x_ref, g_ref, o_ref, *, final_norm):
    k = pl.program_id(1)
    part = jnp.dot(a_ref[...], w_ref[...], preferred_element_type=F32)

    @pl.when(k == 0)
    def _first():
        o_ref[...] = x_ref[...] + part

    @pl.when(k > 0)
    def _rest():
        o_ref[...] += part

    if final_norm:
        @pl.when(k == pl.num_programs(1) - 1)
        def _final_norm():
            o_ref[...] = _rmsnorm_rows(o_ref[...], g_ref[...])


def _ffn_down(a, w, x, g, *, final_norm, bm=256, bk=1024):
    t, f = a.shape
    d = w.shape[1]
    return pl.pallas_call(
        functools.partial(_ffn_down_kernel, final_norm=final_norm),
        grid=(t // bm, f // bk),
        in_specs=[pl.BlockSpec((bm, bk), lambda m, k: (m, k)),
                  pl.BlockSpec((bk, d), lambda m, k: (k, 0)),
                  pl.BlockSpec((bm, d), lambda m, k: (m, 0)),
                  pl.BlockSpec((1, d), lambda m, k: (0, 0))],
        out_specs=pl.BlockSpec((bm, d), lambda m, k: (m, 0)),
        out_shape=jax.ShapeDtypeStruct((t, d), F32),
        compiler_params=_cparams(("arbitrary", "arbitrary")),
        name="ffn_down",
    )(a, w, x, g.reshape(1, d))


@jax.jit
def kernel(x, mem, norm_mix_g, w_in, sgu_ln_g, sgu_ln_b, w_sgu, b_sgu, w_branch_a, w_branch_b, w_out, norm_xattn_g, norm_mem_g, w_xq, w_xkv, w_xo, norm_ffn_g, w_ff_gate, w_ff_up, w_ff_down, norm_final_g):
    batch, seq, d = x.shape
    n_mem = mem.shape[1]
    depth = w_in.shape[0]
    sgu_width = sgu_ln_g.shape[1]
    d_ff = w_ff_gate.shape[2]
    ff_pad = (-d_ff) % FF_PAD
    gate_col = 3 * MOBA_WIDTH + 2 * sgu_width

    xt = x.reshape(batch * seq, d)
    memt = mem.reshape(batch * n_mem, d)
    cos, sin = _rope_table(seq)

    for l in range(depth):
        h = _rmsnorm(xt, norm_mix_g[l], BF16)
        z = _inproj(h, w_in[l].astype(BF16), cos, sin, seq=seq, sgu_width=sgu_width)
        o_a = _moba(z, batch=batch, seq=seq)
        o_b = _sgu(z, sgu_ln_g[l], sgu_ln_b[l], w_sgu[l], jnp.transpose(b_sgu[l]), width=sgu_width)
        merged = _merge(o_a, o_b, w_branch_a[l].astype(BF16), w_branch_b[l].astype(BF16), z,
                        gate_col=gate_col)
        xt = _outproj(merged, w_out[l].astype(BF16), xt)

        kv = _memkv(memt, norm_mem_g[l], w_xkv[l].astype(BF16), n_mem=n_mem)
        xt, h = _xattn(xt, norm_xattn_g[l], w_xq[l].astype(BF16), kv, w_xo[l].astype(BF16),
                       norm_ffn_g[l], seq=seq, n_mem=n_mem)

        w_g = jnp.pad(w_ff_gate[l].astype(BF16), ((0, 0), (0, ff_pad)))
        w_u = jnp.pad(w_ff_up[l].astype(BF16), ((0, 0), (0, ff_pad)))
        w_d = jnp.pad(w_ff_down[l].astype(BF16), ((0, ff_pad), (0, 0)))
        a = _ffn_up(h, w_g, w_u)
        xt = _ffn_down(a, w_d, xt, norm_final_g, final_norm=(l == depth - 1))
    return xt.reshape(batch, seq, d)
```

```python
import functools
import math

import jax
import jax.numpy as jnp
from jax import lax
from jax.experimental import pallas as pl
from jax.experimental.pallas import tpu as pltpu

F32 = jnp.float32
BF16 = jnp.bfloat16

HEAD_DIM = 128
MOBA_HEADS = 16
MOBA_WIDTH = MOBA_HEADS * HEAD_DIM
MOBA_BLOCK = 256
MOBA_TOPK = 3
SGU_CHUNK = 128
SGU_GROUPS = 16
XATTN_HEADS = 4
XATTN_WIDTH = XATTN_HEADS * HEAD_DIM
RMS_EPS = 1e-6
LN_EPS = 1e-5
ROPE_THETA = 10000.0
NEG_INF = -1e30
REMOVED = -3e38
Q_SCALE = HEAD_DIM ** -0.5 * math.log2(math.e)

LANES = 128
FF_PAD = 512
DOWN_COLS = 1024
NORM_ROWS = 64
VMEM_LIMIT = 56 * 1024 * 1024


def _cparams(sem):
    return pltpu.CompilerParams(dimension_semantics=sem, vmem_limit_bytes=VMEM_LIMIT)


def _rmsnorm_rows(x, g):
    ms = jnp.mean(x * x, axis=-1, keepdims=True)
    return x * lax.rsqrt(ms + RMS_EPS) * g


def _rmsnorm_kernel(x_ref, g_ref, o_ref):
    o_ref[...] = _rmsnorm_rows(x_ref[...], g_ref[...]).astype(o_ref.dtype)


def _rmsnorm(x, g, out_dtype, bm=256):
    t, d = x.shape
    return pl.pallas_call(
        _rmsnorm_kernel,
        grid=(t // bm,),
        in_specs=[pl.BlockSpec((bm, d), lambda i: (i, 0)),
                  pl.BlockSpec((1, d), lambda i: (0, 0))],
        out_specs=pl.BlockSpec((bm, d), lambda i: (i, 0)),
        out_shape=jax.ShapeDtypeStruct((t, d), out_dtype),
        compiler_params=_cparams(("arbitrary",)),
        name="rmsnorm",
    )(x, g.reshape(1, d))


def _rope_kernel(freq_ref, cos_ref, sin_ref, *, rows):
    i = pl.program_id(0)
    pos = (i * rows + lax.broadcasted_iota(jnp.int32, (rows, LANES), 0)).astype(F32)
    lane = lax.broadcasted_iota(jnp.int32, (rows, LANES), 1)
    ang = pos * freq_ref[...]
    cos_ref[...] = jnp.cos(ang)
    s = jnp.sin(ang)
    sin_ref[...] = jnp.where(lane < HEAD_DIM // 2, -s, s)


def _rope_table(seq, rows=1024):
    half = HEAD_DIM // 2
    inv_freq = jnp.power(ROPE_THETA, -(jnp.arange(half, dtype=F32) * 2.0 / HEAD_DIM))
    freq = jnp.concatenate([inv_freq, inv_freq]).reshape(1, HEAD_DIM)
    return pl.pallas_call(
        functools.partial(_rope_kernel, rows=rows),
        grid=(seq // rows,),
        in_specs=[pl.BlockSpec((1, HEAD_DIM), lambda i: (0, 0))],
        out_specs=[pl.BlockSpec((rows, HEAD_DIM), lambda i: (i, 0))] * 2,
        out_shape=[jax.ShapeDtypeStruct((seq, HEAD_DIM), F32)] * 2,
        compiler_params=_cparams(("arbitrary",)),
        name="rope_table",
    )(freq)


def _gelu_tanh(x):
    c = math.sqrt(2.0 / math.pi)
    return x * (0.5 * (1.0 + jnp.tanh(c * (x + 0.044715 * (x * x * x)))))


def _sigmoid(x):
    return 1.0 / (1.0 + jnp.exp(-x))


def _inproj_kernel(h_ref, w_ref, cos_ref, sin_ref, z_ref, acc_ref, *, bn, n_q, n_k, n_v, n_act):
    n = pl.program_id(1)
    acc_ref[...] = jnp.dot(h_ref[...], w_ref[...], preferred_element_type=F32)

    @pl.when(n < n_k)
    def _rotary():
        scale = jnp.where(n < n_q, Q_SCALE, 1.0).astype(F32)
        cos = cos_ref[...] * scale
        sin = sin_ref[...] * scale
        for hh in range(bn // HEAD_DIM):
            blk = acc_ref[:, hh * HEAD_DIM:(hh + 1) * HEAD_DIM]
            rot = pltpu.roll(blk, HEAD_DIM // 2, axis=1)
            z_ref[:, hh * HEAD_DIM:(hh + 1) * HEAD_DIM] = (blk * cos + rot * sin).astype(z_ref.dtype)

    @pl.when((n >= n_k) & (n < n_v))
    def _plain():
        z_ref[...] = acc_ref[...].astype(z_ref.dtype)

    @pl.when((n >= n_v) & (n < n_act))
    def _gelu():
        z_ref[...] = _gelu_tanh(acc_ref[...]).astype(z_ref.dtype)

    @pl.when(n >= n_act)
    def _gate():
        z_ref[...] = _sigmoid(acc_ref[...]).astype(z_ref.dtype)


def _inproj(h, w, cos, sin, *, seq, sgu_width, bm=1024, bn=1024):
    t, d = h.shape
    cols = w.shape[1]
    seq_blocks = seq // bm
    n_q = MOBA_WIDTH // bn
    n_k = 2 * MOBA_WIDTH // bn
    n_v = 3 * MOBA_WIDTH // bn
    n_act = (3 * MOBA_WIDTH + 2 * sgu_width) // bn
    kern = functools.partial(_inproj_kernel, bn=bn, n_q=n_q, n_k=n_k, n_v=n_v, n_act=n_act)
    return pl.pallas_call(
        kern,
        grid=(t // bm, cols // bn),
        in_specs=[pl.BlockSpec((bm, d), lambda m, n: (m, 0)),
                  pl.BlockSpec((d, bn), lambda m, n: (0, n)),
                  pl.BlockSpec((bm, HEAD_DIM), lambda m, n: (m % seq_blocks, 0)),
                  pl.BlockSpec((bm, HEAD_DIM), lambda m, n: (m % seq_blocks, 0))],
        out_specs=pl.BlockSpec((bm, bn), lambda m, n: (m, n)),
        out_shape=jax.ShapeDtypeStruct((t, cols), BF16),
        scratch_shapes=[pltpu.VMEM((bm, bn), F32)],
        compiler_params=_cparams(("arbitrary", "arbitrary")),
        name="in_proj",
    )(h, w, cos, sin)


def _moba_kernel(q_ref, k_ref, v_ref, o_ref, kext_ref, vext_ref, kmean_ref, p_ref, *, tile, n_blocks, n_sub):
    qi = pl.program_id(2)
    blocks_per_tile = tile // MOBA_BLOCK
    contract_last = (((1,), (1,)), ((), ()))
    sub = tile // n_sub

    @pl.when(qi == 0)
    def _prepare_keys():
        kmean_ref[...] = jnp.zeros_like(kmean_ref)
        lane = lax.broadcasted_iota(jnp.int32, (MOBA_BLOCK, LANES), 1)
        ones_col = (lane == 0).astype(BF16)
        for j in range(n_blocks):
            rows = pl.ds(j * MOBA_BLOCK, MOBA_BLOCK)
            kb = k_ref[rows, :]
            kext_ref[rows, :HEAD_DIM] = kb
            kext_ref[rows, HEAD_DIM:] = (lane == j).astype(BF16)
            vext_ref[rows, :HEAD_DIM] = v_ref[rows, :]
            vext_ref[rows, HEAD_DIM:] = ones_col
            kmean_ref[j:j + 1, :] = jnp.mean(kb.astype(F32), axis=0, keepdims=True)

    q = q_ref[...]
    gate = lax.dot_general(kmean_ref[...].astype(BF16), q, contract_last,
                           preferred_element_type=F32)[:n_blocks, :]
    blk = lax.broadcasted_iota(jnp.int32, (n_blocks, tile), 0)
    col = lax.broadcasted_iota(jnp.int32, (n_blocks, tile), 1)
    own_block = qi * blocks_per_tile + col // MOBA_BLOCK
    gate = jnp.where(blk < own_block, gate, NEG_INF)
    allowed = blk == own_block
    for _ in range(MOBA_TOPK):
        best = jnp.max(gate, axis=0, keepdims=True)
        first = jnp.min(jnp.where(gate == best, blk, n_blocks), axis=0, keepdims=True)
        hit = blk == first
        allowed = allowed | (hit & (best > 0.5 * NEG_INF))
        gate = jnp.where(hit, REMOVED, gate)
    bias_t = jnp.where(allowed, 0.0, NEG_INF)
    bias_t = jnp.concatenate([bias_t, jnp.zeros((LANES - n_blocks, tile), F32)], axis=0)
    bias = jnp.transpose(bias_t).astype(BF16)
    q_ext = jnp.concatenate([q, bias], axis=1)

    def keys(j):
        return kext_ref[pl.ds(pl.multiple_of(j * tile, tile), tile), :]

    def values(j):
        return vext_ref[pl.ds(pl.multiple_of(j * tile, tile), tile), :]

    def scores(i, k_tile):
        return lax.dot_general(q_ext[i * sub:(i + 1) * sub], k_tile, contract_last,
                               preferred_element_type=F32)

    def weighted_values(i, v_tile):
        return jnp.dot(p_ref[i * sub:(i + 1) * sub, :], v_tile, preferred_element_type=F32)

    carry0 = []
    for i in range(n_sub):
        seen = (i + 1) * sub
        k_seen = kext_ref[pl.ds(pl.multiple_of(qi * tile, tile), seen), :]
        r = i * sub + lax.broadcasted_iota(jnp.int32, (sub, seen), 0)
        c = lax.broadcasted_iota(jnp.int32, (sub, seen), 1)
        s = jnp.where(c <= r, scores(i, k_seen), NEG_INF)
        m0 = jnp.max(s, axis=1, keepdims=True)
        p_ref[i * sub:(i + 1) * sub, :seen] = jnp.exp2(s - m0).astype(BF16)
        if seen < tile:
            p_ref[i * sub:(i + 1) * sub, seen:] = jnp.zeros((sub, tile - seen), BF16)
        carry0 += [m0, jnp.ones((sub, 1), F32), jnp.zeros((sub, 2 * HEAD_DIM), F32)]

    def body(j, carry):
        k_tile = keys(j)
        v_tile = values(jnp.where(j == 0, qi, j - 1))
        out = []
        for i in range(n_sub):
            m, alpha, acc = carry[3 * i:3 * i + 3]
            acc = alpha * acc + weighted_values(i, v_tile)
            s = scores(i, k_tile)
            m_new = jnp.maximum(m, jnp.max(s, axis=1, keepdims=True))
            p_ref[i * sub:(i + 1) * sub, :] = jnp.exp2(s - m_new).astype(BF16)
            out += [m_new, jnp.exp2(m - m_new), acc]
        return tuple(out)

    carry = lax.fori_loop(0, qi, body, tuple(carry0))
    v_tile = values(jnp.where(qi == 0, qi, qi - 1))
    for i in range(n_sub):
        _, alpha, acc = carry[3 * i:3 * i + 3]
        acc = alpha * acc + weighted_values(i, v_tile)
        o_ref[i * sub:(i + 1) * sub, :] = (acc[:, :HEAD_DIM] / acc[:, HEAD_DIM:HEAD_DIM + 1]).astype(o_ref.dtype)


def _moba(z, *, batch, seq, tile=1024, n_sub=4):
    t = z.shape[0]
    n_blocks = seq // MOBA_BLOCK
    tiles = seq // tile
    kern = functools.partial(_moba_kernel, tile=tile, n_blocks=n_blocks, n_sub=n_sub)
    return pl.pallas_call(
        kern,
        grid=(batch, MOBA_HEADS, tiles),
        in_specs=[pl.BlockSpec((tile, HEAD_DIM), lambda b, h, i: (b * tiles + i, h)),
                  pl.BlockSpec((seq, HEAD_DIM), lambda b, h, i: (b, MOBA_HEADS + h)),
                  pl.BlockSpec((seq, HEAD_DIM), lambda b, h, i: (b, 2 * MOBA_HEADS + h))],
        out_specs=pl.BlockSpec((tile, HEAD_DIM), lambda b, h, i: (b * tiles + i, h)),
        out_shape=jax.ShapeDtypeStruct((t, MOBA_WIDTH), BF16),
        scratch_shapes=[pltpu.VMEM((seq, 2 * HEAD_DIM), BF16),
                        pltpu.VMEM((seq, 2 * HEAD_DIM), BF16),
                        pltpu.VMEM((LANES, HEAD_DIM), F32),
                        pltpu.VMEM((tile, tile), BF16)],
        compiler_params=_cparams(("arbitrary", "arbitrary", "arbitrary")),
        name="moba_attention",
    )(z, z, z)


def _sgu_kernel(u_ref, v_ref, g_ref, b_ref, w_ref, bs_ref, o_ref, *, rows):
    v = v_ref[...].astype(F32)
    mu = jnp.mean(v, axis=-1, keepdims=True)
    var = jnp.mean(jnp.square(v - mu), axis=-1, keepdims=True)
    vln = ((v - mu) * lax.rsqrt(var + LN_EPS) * g_ref[...] + b_ref[...]).astype(BF16)
    t_idx = lax.broadcasted_iota(jnp.int32, (SGU_CHUNK, SGU_CHUNK), 0)
    s_idx = lax.broadcasted_iota(jnp.int32, (SGU_CHUNK, SGU_CHUNK), 1)
    causal = s_idx <= t_idx
    for g in range(SGU_GROUPS):
        w = jnp.where(causal, w_ref[g], 0.0).astype(BF16)
        bias = bs_ref[:, g:g + 1]
        cols = slice(g * LANES, (g + 1) * LANES)
        for ch in range(rows // SGU_CHUNK):
            rws = slice(ch * SGU_CHUNK, (ch + 1) * SGU_CHUNK)
            mixed = jnp.dot(w, vln[rws, cols], preferred_element_type=F32) + bias
            o_ref[rws, cols] = (u_ref[rws, cols].astype(F32) * mixed).astype(o_ref.dtype)


def _sgu(z, ln_g, ln_b, w_s, b_s_t, *, width, rows=512):
    t = z.shape[0]
    u_blk = 3 * MOBA_WIDTH // width
    kern = functools.partial(_sgu_kernel, rows=rows)
    return pl.pallas_call(
        kern,
        grid=(t // rows,),
        in_specs=[pl.BlockSpec((rows, width), lambda i: (i, u_blk)),
                  pl.BlockSpec((rows, width), lambda i: (i, u_blk + 1)),
                  pl.BlockSpec((1, width), lambda i: (0, 0)),
                  pl.BlockSpec((1, width), lambda i: (0, 0)),
                  pl.BlockSpec((SGU_GROUPS, SGU_CHUNK, SGU_CHUNK), lambda i: (0, 0, 0)),
                  pl.BlockSpec((SGU_CHUNK, SGU_GROUPS), lambda i: (0, 0))],
        out_specs=pl.BlockSpec((rows, width), lambda i: (i, 0)),
        out_shape=jax.ShapeDtypeStruct((t, width), BF16),
        compiler_params=_cparams(("arbitrary",)),
        name="spatial_gating",
    )(z, z, ln_g.reshape(1, width), ln_b.reshape(1, width), w_s, b_s_t)


def _merge_kernel(oa_ref, ob_ref, pa_ref, pb_ref, ga_ref, gb_ref, o_ref):
    a = jnp.dot(oa_ref[...], pa_ref[...], preferred_element_type=F32)
    b = jnp.dot(ob_ref[...], pb_ref[...], preferred_element_type=F32)
    o_ref[...] = (ga_ref[...].astype(F32) * a + gb_ref[...].astype(F32) * b).astype(o_ref.dtype)


def _merge(o_a, o_b, p_a, p_b, z, *, gate_col, bm=1024, bn=1024):
    t = o_a.shape[0]
    d = p_a.shape[1]
    ga_blk = gate_col // bn
    gb_blk = (gate_col + d) // bn
    return pl.pallas_call(
        _merge_kernel,
        grid=(t // bm, d // bn),
        in_specs=[pl.BlockSpec((bm, o_a.shape[1]), lambda m, n: (m, 0)),
                  pl.BlockSpec((bm, o_b.shape[1]), lambda m, n: (m, 0)),
                  pl.BlockSpec((p_a.shape[0], bn), lambda m, n: (0, n)),
                  pl.BlockSpec((p_b.shape[0], bn), lambda m, n: (0, n)),
                  pl.BlockSpec((bm, bn), lambda m, n: (m, ga_blk + n)),
                  pl.BlockSpec((bm, bn), lambda m, n: (m, gb_blk + n))],
        out_specs=pl.BlockSpec((bm, bn), lambda m, n: (m, n)),
        out_shape=jax.ShapeDtypeStruct((t, d), BF16),
        compiler_params=_cparams(("arbitrary", "arbitrary")),
        name="branch_merge",
    )(o_a, o_b, p_a, p_b, z, z)


def _outproj_kernel(a_ref, w_ref, x_ref, o_ref):
    o_ref[...] = x_ref[...] + jnp.dot(a_ref[...], w_ref[...], preferred_element_type=F32)


def _outproj(a, w, x, *, bm=1024, bn=512):
    t, k = a.shape
    d = w.shape[1]
    return pl.pallas_call(
        _outproj_kernel,
        grid=(t // bm, d // bn),
        in_specs=[pl.BlockSpec((bm, k), lambda m, n: (m, 0)),
                  pl.BlockSpec((k, bn), lambda m, n: (0, n)),
                  pl.BlockSpec((bm, bn), lambda m, n: (m, n))],
        out_specs=pl.BlockSpec((bm, bn), lambda m, n: (m, n)),
        out_shape=jax.ShapeDtypeStruct((t, d), F32),
        compiler_params=_cparams(("arbitrary", "arbitrary")),
        name="out_proj",
    )(a, w, x)


def _memkv_kernel(mem_ref, g_ref, w_ref, o_ref):
    mem_n = _rmsnorm_rows(mem_ref[...], g_ref[...]).astype(BF16)
    o_ref[...] = jnp.dot(mem_n, w_ref[...], preferred_element_type=F32).astype(o_ref.dtype)


def _memkv(mem, g, w, *, n_mem):
    t, d = mem.shape
    cols = w.shape[1]
    return pl.pallas_call(
        _memkv_kernel,
        grid=(t // n_mem,),
        in_specs=[pl.BlockSpec((n_mem, d), lambda i: (i, 0)),
                  pl.BlockSpec((1, d), lambda i: (0, 0)),
                  pl.BlockSpec((d, cols), lambda i: (0, 0))],
        out_specs=pl.BlockSpec((n_mem, cols), lambda i: (i, 0)),
        out_shape=jax.ShapeDtypeStruct((t, cols), BF16),
        compiler_params=_cparams(("arbitrary",)),
        name="mem_kv",
    )(mem, g.reshape(1, d), w)


def _xattn_kernel(x_ref, gx_ref, wq_ref, kv_ref, wo_ref, gf_ref, x_out_ref, h_out_ref):
    x = x_ref[...]
    h = _rmsnorm_rows(x, gx_ref[...]).astype(BF16)
    q = jnp.dot(h, wq_ref[...], preferred_element_type=F32) * (HEAD_DIM ** -0.5)
    q = q.astype(BF16)
    outs = []
    for hh in range(XATTN_HEADS):
        cols = slice(hh * HEAD_DIM, (hh + 1) * HEAD_DIM)
        k = kv_ref[:, hh * HEAD_DIM:(hh + 1) * HEAD_DIM]
        v = kv_ref[:, XATTN_WIDTH + hh * HEAD_DIM:XATTN_WIDTH + (hh + 1) * HEAD_DIM]
        s = lax.dot_general(q[:, cols], k, (((1,), (1,)), ((), ())), preferred_element_type=F32)
        s = s - jnp.max(s, axis=-1, keepdims=True)
        p = jnp.exp(s)
        p = p / jnp.sum(p, axis=-1, keepdims=True)
        outs.append(jnp.dot(p.astype(BF16), v, preferred_element_type=F32).astype(BF16))
    o = jnp.concatenate(outs, axis=1)
    x2 = x + jnp.dot(o, wo_ref[...], preferred_element_type=F32)
    x_out_ref[...] = x2
    h_out_ref[...] = _rmsnorm_rows(x2, gf_ref[...]).astype(h_out_ref.dtype)


def _xattn(x, g_x, w_q, kv, w_o, g_ffn, *, seq, n_mem, bm=256):
    t, d = x.shape
    per_batch = seq // bm
    return pl.pallas_call(
        _xattn_kernel,
        grid=(t // bm,),
        in_specs=[pl.BlockSpec((bm, d), lambda i: (i, 0)),
                  pl.BlockSpec((1, d), lambda i: (0, 0)),
                  pl.BlockSpec(w_q.shape, lambda i: (0, 0)),
                  pl.BlockSpec((n_mem, kv.shape[1]), lambda i: (i // per_batch, 0)),
                  pl.BlockSpec(w_o.shape, lambda i: (0, 0)),
                  pl.BlockSpec((1, d), lambda i: (0, 0))],
        out_specs=[pl.BlockSpec((bm, d), lambda i: (i, 0)),
                   pl.BlockSpec((bm, d), lambda i: (i, 0))],
        out_shape=[jax.ShapeDtypeStruct((t, d), F32),
                   jax.ShapeDtypeStruct((t, d), BF16)],
        compiler_params=_cparams(("arbitrary",)),
        name="mem_cross_attention",
    )(x, g_x.reshape(1, d), w_q, kv, w_o, g_ffn.reshape(1, d))


def _ffn_up_kernel(h_ref, wg_ref, wu_ref, o_ref):
    h = h_ref[...]
    g = jnp.dot(h, wg_ref[...], preferred_element_type=F32)
    u = jnp.dot(h, wu_ref[...], preferred_element_type=F32)
    o_ref[...] = (g * _sigmoid(g) * u).astype(o_ref.dtype)


def _ffn_up(h, w_g, w_u, *, bm=1024, bn=512):
    t, d = h.shape
    f = w_g.shape[1]
    return pl.pallas_call(
        _ffn_up_kernel,
        grid=(t // bm, f // bn),
        in_specs=[pl.BlockSpec((bm, d), lambda m, n: (m, 0)),
                  pl.BlockSpec((d, bn), lambda m, n: (0, n)),
                  pl.BlockSpec((d, bn), lambda m, n: (0, n))],
        out_specs=pl.BlockSpec((bm, bn), lambda m, n: (m, n)),
        out_shape=jax.ShapeDtypeStruct((t, f), BF16),
        compiler_params=_cparams(("arbitrary", "arbitrary")),
        name="ffn_up",
    )(h, w_g, w_u)


def _ffn_down_kernel(a_ref, w_ref, x_ref, g_ref, o_ref, *, final_norm):
    k = pl.program_id(1)
    bm, d = o_ref.shape

    @pl.when(k == 0)
    def _residual():
        o_ref[...] = x_ref[...]

    a = a_ref[...]
    for c in range(d // DOWN_COLS):
        cols = slice(c * DOWN_COLS, (c + 1) * DOWN_COLS)
        o_ref[:, cols] += jnp.dot(a, w_ref[:, cols], preferred_element_type=F32)

    if final_norm:
        @pl.when(k == pl.num_programs(1) - 1)
        def _final_norm():
            for r in range(bm // NORM_ROWS):
                rows = slice(r * NORM_ROWS, (r + 1) * NORM_ROWS)
                o_ref[rows, :] = _rmsnorm_rows(o_ref[rows, :], g_ref[...])


def _ffn_down(a, w, x, g, *, final_norm, bm=512, bk=1024):
    t, f = a.shape
    d = w.shape[1]
    return pl.pallas_call(
        functools.partial(_ffn_down_kernel, final_norm=final_norm),
        grid=(t // bm, f // bk),
        in_specs=[pl.BlockSpec((bm, bk), lambda m, k: (m, k)),
                  pl.BlockSpec((bk, d), lambda m, k: (k, 0)),
                  pl.BlockSpec((bm, d), lambda m, k: (m, 0), pipeline_mode=pl.Buffered(1)),
                  pl.BlockSpec((1, d), lambda m, k: (0, 0))],
        out_specs=pl.BlockSpec((bm, d), lambda m, k: (m, 0)),
        out_shape=jax.ShapeDtypeStruct((t, d), F32),
        compiler_params=_cparams(("arbitrary", "arbitrary")),
        name="ffn_down",
    )(a, w, x, g.reshape(1, d))


@jax.jit
def kernel(x, mem, norm_mix_g, w_in, sgu_ln_g, sgu_ln_b, w_sgu, b_sgu, w_branch_a, w_branch_b, w_out, norm_xattn_g, norm_mem_g, w_xq, w_xkv, w_xo, norm_ffn_g, w_ff_gate, w_ff_up, w_ff_down, norm_final_g):
    batch, seq, d = x.shape
    n_mem = mem.shape[1]
    depth = w_in.shape[0]
    sgu_width = sgu_ln_g.shape[1]
    d_ff = w_ff_gate.shape[2]
    ff_pad = (-d_ff) % FF_PAD
    gate_col = 3 * MOBA_WIDTH + 2 * sgu_width

    xt = x.reshape(batch * seq, d)
    memt = mem.reshape(batch * n_mem, d)
    cos, sin = _rope_table(seq)

    for l in range(depth):
        h = _rmsnorm(xt, norm_mix_g[l], BF16)
        z = _inproj(h, w_in[l].astype(BF16), cos, sin, seq=seq, sgu_width=sgu_width)
        o_a = _moba(z, batch=batch, seq=seq)
        o_b = _sgu(z, sgu_ln_g[l], sgu_ln_b[l], w_sgu[l], jnp.transpose(b_sgu[l]), width=sgu_width)
        merged = _merge(o_a, o_b, w_branch_a[l].astype(BF16), w_branch_b[l].astype(BF16), z,
                        gate_col=gate_col)
        xt = _outproj(merged, w_out[l].astype(BF16), xt)

        kv = _memkv(memt, norm_mem_g[l], w_xkv[l].astype(BF16), n_mem=n_mem)
        xt, h = _xattn(xt, norm_xattn_g[l], w_xq[l].astype(BF16), kv, w_xo[l].astype(BF16),
                       norm_ffn_g[l], seq=seq, n_mem=n_mem)

        w_g = jnp.pad(w_ff_gate[l].astype(BF16), ((0, 0), (0, ff_pad)))
        w_u = jnp.pad(w_ff_up[l].astype(BF16), ((0, 0), (0, ff_pad)))
        w_d = jnp.pad(w_ff_down[l].astype(BF16), ((0, ff_pad), (0, 0)))
        a = _ffn_up(h, w_g, w_u)
        xt = _ffn_down(a, w_d, xt, norm_final_g, final_norm=(l == depth - 1))
    return xt.reshape(batch, seq, d)
```

```python
import functools
import math

import jax
import jax.numpy as jnp
from jax import lax
from jax.experimental import pallas as pl
from jax.experimental.pallas import tpu as pltpu

F32 = jnp.float32
BF16 = jnp.bfloat16

HEAD_DIM = 128
MOBA_HEADS = 16
MOBA_WIDTH = MOBA_HEADS * HEAD_DIM
MOBA_BLOCK = 256
MOBA_TOPK = 3
SGU_CHUNK = 128
SGU_GROUPS = 16
XATTN_HEADS = 4
XATTN_WIDTH = XATTN_HEADS * HEAD_DIM
RMS_EPS = 1e-6
LN_EPS = 1e-5
ROPE_THETA = 10000.0
NEG_INF = -1e30
REMOVED = -3e38
Q_SCALE = HEAD_DIM ** -0.5 * math.log2(math.e)

LANES = 128
FF_PAD = 512
PROJ_COLS = 256
DOWN_COLS = 1024
NORM_ROWS = 64
VMEM_LIMIT = 56 * 1024 * 1024


def _cparams(sem):
    return pltpu.CompilerParams(dimension_semantics=sem, vmem_limit_bytes=VMEM_LIMIT)


def _rmsnorm_rows(x, g):
    ms = jnp.mean(x * x, axis=-1, keepdims=True)
    return x * lax.rsqrt(ms + RMS_EPS) * g


def _rmsnorm_kernel(x_ref, g_ref, o_ref):
    o_ref[...] = _rmsnorm_rows(x_ref[...], g_ref[...]).astype(o_ref.dtype)


def _rmsnorm(x, g, out_dtype, bm=256):
    t, d = x.shape
    return pl.pallas_call(
        _rmsnorm_kernel,
        grid=(t // bm,),
        in_specs=[pl.BlockSpec((bm, d), lambda i: (i, 0)),
                  pl.BlockSpec((1, d), lambda i: (0, 0))],
        out_specs=pl.BlockSpec((bm, d), lambda i: (i, 0)),
        out_shape=jax.ShapeDtypeStruct((t, d), out_dtype),
        compiler_params=_cparams(("arbitrary",)),
        name="rmsnorm",
    )(x, g.reshape(1, d))


def _rope_kernel(freq_ref, cos_ref, sin_ref, *, rows):
    i = pl.program_id(0)
    pos = (i * rows + lax.broadcasted_iota(jnp.int32, (rows, LANES), 0)).astype(F32)
    lane = lax.broadcasted_iota(jnp.int32, (rows, LANES), 1)
    ang = pos * freq_ref[...]
    cos_ref[...] = jnp.cos(ang)
    s = jnp.sin(ang)
    sin_ref[...] = jnp.where(lane < HEAD_DIM // 2, -s, s)


def _rope_table(seq, rows=1024):
    half = HEAD_DIM // 2
    inv_freq = jnp.power(ROPE_THETA, -(jnp.arange(half, dtype=F32) * 2.0 / HEAD_DIM))
    freq = jnp.concatenate([inv_freq, inv_freq]).reshape(1, HEAD_DIM)
    return pl.pallas_call(
        functools.partial(_rope_kernel, rows=rows),
        grid=(seq // rows,),
        in_specs=[pl.BlockSpec((1, HEAD_DIM), lambda i: (0, 0))],
        out_specs=[pl.BlockSpec((rows, HEAD_DIM), lambda i: (i, 0))] * 2,
        out_shape=[jax.ShapeDtypeStruct((seq, HEAD_DIM), F32)] * 2,
        compiler_params=_cparams(("arbitrary",)),
        name="rope_table",
    )(freq)


def _gelu_tanh(x):
    c = math.sqrt(2.0 / math.pi)
    return x * (0.5 * (1.0 + jnp.tanh(c * (x + 0.044715 * (x * x * x)))))


def _sigmoid(x):
    return 1.0 / (1.0 + jnp.exp(-x))


def _inproj_kernel(h_ref, w_ref, cos_ref, sin_ref, z_ref, *, bn, n_q, n_k, n_v, n_act):
    n = pl.program_id(1)

    def project(epilogue):
        h = h_ref[...]
        for c in range(bn // PROJ_COLS):
            cols = slice(c * PROJ_COLS, (c + 1) * PROJ_COLS)
            blk = jnp.dot(h, w_ref[:, cols], preferred_element_type=F32)
            z_ref[:, cols] = epilogue(blk).astype(z_ref.dtype)

    @pl.when(n < n_k)
    def _rotary():
        scale = jnp.where(n < n_q, Q_SCALE, 1.0).astype(F32)
        cos = cos_ref[...] * scale
        sin = sin_ref[...] * scale

        def rotate(blk):
            heads = []
            for hh in range(PROJ_COLS // HEAD_DIM):
                x = blk[:, hh * HEAD_DIM:(hh + 1) * HEAD_DIM]
                heads.append(x * cos + pltpu.roll(x, HEAD_DIM // 2, axis=1) * sin)
            return jnp.concatenate(heads, axis=1)

        project(rotate)

    @pl.when((n >= n_k) & (n < n_v))
    def _plain():
        project(lambda blk: blk)

    @pl.when((n >= n_v) & (n < n_act))
    def _gelu():
        project(_gelu_tanh)

    @pl.when(n >= n_act)
    def _gate():
        project(_sigmoid)


def _inproj(h, w, cos, sin, *, seq, sgu_width, bm=1024, bn=1024):
    t, d = h.shape
    cols = w.shape[1]
    seq_blocks = seq // bm
    n_q = MOBA_WIDTH // bn
    n_k = 2 * MOBA_WIDTH // bn
    n_v = 3 * MOBA_WIDTH // bn
    n_act = (3 * MOBA_WIDTH + 2 * sgu_width) // bn
    kern = functools.partial(_inproj_kernel, bn=bn, n_q=n_q, n_k=n_k, n_v=n_v, n_act=n_act)
    return pl.pallas_call(
        kern,
        grid=(t // bm, cols // bn),
        in_specs=[pl.BlockSpec((bm, d), lambda m, n: (m, 0)),
                  pl.BlockSpec((d, bn), lambda m, n: (0, n)),
                  pl.BlockSpec((bm, HEAD_DIM), lambda m, n: (m % seq_blocks, 0)),
                  pl.BlockSpec((bm, HEAD_DIM), lambda m, n: (m % seq_blocks, 0))],
        out_specs=pl.BlockSpec((bm, bn), lambda m, n: (m, n)),
        out_shape=jax.ShapeDtypeStruct((t, cols), BF16),
        compiler_params=_cparams(("arbitrary", "arbitrary")),
        name="in_proj",
    )(h, w, cos, sin)


def _moba_kernel(q_ref, k_ref, v_ref, o_ref, kext_ref, vext_ref, kmean_ref, p_ref, *, tile, n_blocks, n_sub, heads):
    qi = pl.program_id(2)
    blocks_per_tile = tile // MOBA_BLOCK
    contract_last = (((1,), (1,)), ((), ()))
    sub = tile // n_sub

    chains = [(hd, i) for hd in range(heads) for i in range(n_sub)]

    def head_cols(hd):
        return slice(hd * HEAD_DIM, (hd + 1) * HEAD_DIM)

    @pl.when(qi == 0)
    def _prepare_keys():
        kmean_ref[...] = jnp.zeros_like(kmean_ref)
        lane = lax.broadcasted_iota(jnp.int32, (MOBA_BLOCK, LANES), 1)
        ones_col = (lane == 0).astype(BF16)
        for hd in range(heads):
            for j in range(n_blocks):
                rows = pl.ds(j * MOBA_BLOCK, MOBA_BLOCK)
                kb = k_ref[rows, head_cols(hd)]
                kext_ref[hd, rows, :HEAD_DIM] = kb
                kext_ref[hd, rows, HEAD_DIM:] = (lane == j).astype(BF16)
                vext_ref[hd, rows, :HEAD_DIM] = v_ref[rows, head_cols(hd)]
                vext_ref[hd, rows, HEAD_DIM:] = ones_col
                kmean_ref[hd, j:j + 1, :] = jnp.mean(kb.astype(F32), axis=0, keepdims=True)

    blk = lax.broadcasted_iota(jnp.int32, (n_blocks, tile), 0)
    col = lax.broadcasted_iota(jnp.int32, (n_blocks, tile), 1)
    own_block = qi * blocks_per_tile + col // MOBA_BLOCK
    q_ext = []
    for hd in range(heads):
        q = q_ref[:, head_cols(hd)]
        gate = lax.dot_general(kmean_ref[hd].astype(BF16), q, contract_last,
                               preferred_element_type=F32)[:n_blocks, :]
        gate = jnp.where(blk < own_block, gate, NEG_INF)
        allowed = blk == own_block
        for _ in range(MOBA_TOPK):
            best = jnp.max(gate, axis=0, keepdims=True)
            first = jnp.min(jnp.where(gate == best, blk, n_blocks), axis=0, keepdims=True)
            hit = blk == first
            allowed = allowed | (hit & (best > 0.5 * NEG_INF))
            gate = jnp.where(hit, REMOVED, gate)
        bias_t = jnp.where(allowed, 0.0, NEG_INF)
        bias_t = jnp.concatenate([bias_t, jnp.zeros((LANES - n_blocks, tile), F32)], axis=0)
        bias = jnp.transpose(bias_t).astype(BF16)
        q_ext.append(jnp.concatenate([q, bias], axis=1))

    def keys(hd, j, rows=tile):
        return kext_ref[hd, pl.ds(pl.multiple_of(j * tile, tile), rows), :]

    def values(hd, j):
        return vext_ref[hd, pl.ds(pl.multiple_of(j * tile, tile), tile), :]

    def scores(hd, i, k_tile):
        return lax.dot_general(q_ext[hd][i * sub:(i + 1) * sub], k_tile, contract_last,
                               preferred_element_type=F32)

    def weighted_values(hd, i, v_tile):
        return jnp.dot(p_ref[hd, i * sub:(i + 1) * sub, :], v_tile, preferred_element_type=F32)

    carry0 = []
    for hd, i in chains:
        seen = (i + 1) * sub
        r = i * sub + lax.broadcasted_iota(jnp.int32, (sub, seen), 0)
        c = lax.broadcasted_iota(jnp.int32, (sub, seen), 1)
        s = jnp.where(c <= r, scores(hd, i, keys(hd, qi, seen)), NEG_INF)
        m0 = jnp.max(s, axis=1, keepdims=True)
        p_ref[hd, i * sub:(i + 1) * sub, :seen] = jnp.exp2(s - m0).astype(BF16)
        if seen < tile:
            p_ref[hd, i * sub:(i + 1) * sub, seen:] = jnp.zeros((sub, tile - seen), BF16)
        carry0 += [m0, jnp.ones((sub, 1), F32), jnp.zeros((sub, 2 * HEAD_DIM), F32)]

    def body(j, carry):
        j_prev = jnp.where(j == 0, qi, j - 1)
        k_tiles = [keys(hd, j) for hd in range(heads)]
        v_tiles = [values(hd, j_prev) for hd in range(heads)]
        out = []
        for n, (hd, i) in enumerate(chains):
            m, alpha, acc = carry[3 * n:3 * n + 3]
            acc = alpha * acc + weighted_values(hd, i, v_tiles[hd])
            s = scores(hd, i, k_tiles[hd])
            m_new = jnp.maximum(m, jnp.max(s, axis=1, keepdims=True))
            p_ref[hd, i * sub:(i + 1) * sub, :] = jnp.exp2(s - m_new).astype(BF16)
            out += [m_new, jnp.exp2(m - m_new), acc]
        return tuple(out)

    carry = lax.fori_loop(0, qi, body, tuple(carry0))
    j_last = jnp.where(qi == 0, qi, qi - 1)
    for n, (hd, i) in enumerate(chains):
        _, alpha, acc = carry[3 * n:3 * n + 3]
        acc = alpha * acc + weighted_values(hd, i, values(hd, j_last))
        o_ref[i * sub:(i + 1) * sub, head_cols(hd)] = (
            acc[:, :HEAD_DIM] / acc[:, HEAD_DIM:HEAD_DIM + 1]).astype(o_ref.dtype)


def _moba(z, *, batch, seq, tile=1024, n_sub=4, heads=2):
    t = z.shape[0]
    n_blocks = seq // MOBA_BLOCK
    tiles = seq // tile
    groups = MOBA_HEADS // heads
    width = heads * HEAD_DIM
    kern = functools.partial(_moba_kernel, tile=tile, n_blocks=n_blocks, n_sub=n_sub, heads=heads)
    return pl.pallas_call(
        kern,
        grid=(batch, groups, tiles),
        in_specs=[pl.BlockSpec((tile, width), lambda b, g, i: (b * tiles + i, g)),
                  pl.BlockSpec((seq, width), lambda b, g, i: (b, groups + g)),
                  pl.BlockSpec((seq, width), lambda b, g, i: (b, 2 * groups + g))],
        out_specs=pl.BlockSpec((tile, width), lambda b, g, i: (b * tiles + i, g)),
        out_shape=jax.ShapeDtypeStruct((t, MOBA_WIDTH), BF16),
        scratch_shapes=[pltpu.VMEM((heads, seq, 2 * HEAD_DIM), BF16),
                        pltpu.VMEM((heads, seq, 2 * HEAD_DIM), BF16),
                        pltpu.VMEM((heads, LANES, HEAD_DIM), F32),
                        pltpu.VMEM((heads, tile, tile), BF16)],
        compiler_params=_cparams(("arbitrary", "arbitrary", "arbitrary")),
        name="moba_attention",
    )(z, z, z)


def _sgu_kernel(u_ref, v_ref, g_ref, b_ref, w_ref, bs_ref, o_ref, *, rows):
    v = v_ref[...].astype(F32)
    mu = jnp.mean(v, axis=-1, keepdims=True)
    var = jnp.mean(jnp.square(v - mu), axis=-1, keepdims=True)
    vln = ((v - mu) * lax.rsqrt(var + LN_EPS) * g_ref[...] + b_ref[...]).astype(BF16)
    t_idx = lax.broadcasted_iota(jnp.int32, (SGU_CHUNK, SGU_CHUNK), 0)
    s_idx = lax.broadcasted_iota(jnp.int32, (SGU_CHUNK, SGU_CHUNK), 1)
    causal = s_idx <= t_idx
    for g in range(SGU_GROUPS):
        w = jnp.where(causal, w_ref[g], 0.0).astype(BF16)
        bias = bs_ref[:, g:g + 1]
        cols = slice(g * LANES, (g + 1) * LANES)
        for ch in range(rows // SGU_CHUNK):
            rws = slice(ch * SGU_CHUNK, (ch + 1) * SGU_CHUNK)
            mixed = jnp.dot(w, vln[rws, cols], preferred_element_type=F32) + bias
            o_ref[rws, cols] = (u_ref[rws, cols].astype(F32) * mixed).astype(o_ref.dtype)


def _sgu(z, ln_g, ln_b, w_s, b_s_t, *, width, rows=512):
    t = z.shape[0]
    u_blk = 3 * MOBA_WIDTH // width
    kern = functools.partial(_sgu_kernel, rows=rows)
    return pl.pallas_call(
        kern,
        grid=(t // rows,),
        in_specs=[pl.BlockSpec((rows, width), lambda i: (i, u_blk)),
                  pl.BlockSpec((rows, width), lambda i: (i, u_blk + 1)),
                  pl.BlockSpec((1, width), lambda i: (0, 0)),
                  pl.BlockSpec((1, width), lambda i: (0, 0)),
                  pl.BlockSpec((SGU_GROUPS, SGU_CHUNK, SGU_CHUNK), lambda i: (0, 0, 0)),
                  pl.BlockSpec((SGU_CHUNK, SGU_GROUPS), lambda i: (0, 0))],
        out_specs=pl.BlockSpec((rows, width), lambda i: (i, 0)),
        out_shape=jax.ShapeDtypeStruct((t, width), BF16),
        compiler_params=_cparams(("arbitrary",)),
        name="spatial_gating",
    )(z, z, ln_g.reshape(1, width), ln_b.reshape(1, width), w_s, b_s_t)


def _merge_kernel(oa_ref, ob_ref, pa_ref, pb_ref, ga_ref, gb_ref, o_ref):
    a = jnp.dot(oa_ref[...], pa_ref[...], preferred_element_type=F32)
    b = jnp.dot(ob_ref[...], pb_ref[...], preferred_element_type=F32)
    o_ref[...] = (ga_ref[...].astype(F32) * a + gb_ref[...].astype(F32) * b).astype(o_ref.dtype)


def _merge(o_a, o_b, p_a, p_b, z, *, gate_col, bm=1024, bn=1024):
    t = o_a.shape[0]
    d = p_a.shape[1]
    ga_blk = gate_col // bn
    gb_blk = (gate_col + d) // bn
    return pl.pallas_call(
        _merge_kernel,
        grid=(t // bm, d // bn),
        in_specs=[pl.BlockSpec((bm, o_a.shape[1]), lambda m, n: (m, 0)),
                  pl.BlockSpec((bm, o_b.shape[1]), lambda m, n: (m, 0)),
                  pl.BlockSpec((p_a.shape[0], bn), lambda m, n: (0, n)),
                  pl.BlockSpec((p_b.shape[0], bn), lambda m, n: (0, n)),
                  pl.BlockSpec((bm, bn), lambda m, n: (m, ga_blk + n)),
                  pl.BlockSpec((bm, bn), lambda m, n: (m, gb_blk + n))],
        out_specs=pl.BlockSpec((bm, bn), lambda m, n: (m, n)),
        out_shape=jax.ShapeDtypeStruct((t, d), BF16),
        compiler_params=_cparams(("arbitrary", "arbitrary")),
        name="branch_merge",
    )(o_a, o_b, p_a, p_b, z, z)


def _outproj_kernel(a_ref, w_ref, x_ref, o_ref):
    o_ref[...] = x_ref[...] + jnp.dot(a_ref[...], w_ref[...], preferred_element_type=F32)


def _outproj(a, w, x, *, bm=1024, bn=512):
    t, k = a.shape
    d = w.shape[1]
    return pl.pallas_call(
        _outproj_kernel,
        grid=(t // bm, d // bn),
        in_specs=[pl.BlockSpec((bm, k), lambda m, n: (m, 0)),
                  pl.BlockSpec((k, bn), lambda m, n: (0, n)),
                  pl.BlockSpec((bm, bn), lambda m, n: (m, n))],
        out_specs=pl.BlockSpec((bm, bn), lambda m, n: (m, n)),
        out_shape=jax.ShapeDtypeStruct((t, d), F32),
        compiler_params=_cparams(("arbitrary", "arbitrary")),
        name="out_proj",
    )(a, w, x)


def _memkv_kernel(mem_ref, g_ref, w_ref, o_ref):
    mem_n = _rmsnorm_rows(mem_ref[...], g_ref[...]).astype(BF16)
    o_ref[...] = jnp.dot(mem_n, w_ref[...], preferred_element_type=F32).astype(o_ref.dtype)


def _memkv(mem, g, w, *, n_mem):
    t, d = mem.shape
    cols = w.shape[1]
    return pl.pallas_call(
        _memkv_kernel,
        grid=(t // n_mem,),
        in_specs=[pl.BlockSpec((n_mem, d), lambda i: (i, 0)),
                  pl.BlockSpec((1, d), lambda i: (0, 0)),
                  pl.BlockSpec((d, cols), lambda i: (0, 0))],
        out_specs=pl.BlockSpec((n_mem, cols), lambda i: (i, 0)),
        out_shape=jax.ShapeDtypeStruct((t, cols), BF16),
        compiler_params=_cparams(("arbitrary",)),
        name="mem_kv",
    )(mem, g.reshape(1, d), w)


def _xattn_kernel(x_ref, gx_ref, wq_ref, kv_ref, wo_ref, gf_ref, x_out_ref, h_out_ref):
    x = x_ref[...]
    h = _rmsnorm_rows(x, gx_ref[...]).astype(BF16)
    q = jnp.dot(h, wq_ref[...], preferred_element_type=F32) * (HEAD_DIM ** -0.5)
    q = q.astype(BF16)
    outs = []
    for hh in range(XATTN_HEADS):
        cols = slice(hh * HEAD_DIM, (hh + 1) * HEAD_DIM)
        k = kv_ref[:, hh * HEAD_DIM:(hh + 1) * HEAD_DIM]
        v = kv_ref[:, XATTN_WIDTH + hh * HEAD_DIM:XATTN_WIDTH + (hh + 1) * HEAD_DIM]
        s = lax.dot_general(q[:, cols], k, (((1,), (1,)), ((), ())), preferred_element_type=F32)
        s = s - jnp.max(s, axis=-1, keepdims=True)
        p = jnp.exp(s)
        p = p / jnp.sum(p, axis=-1, keepdims=True)
        outs.append(jnp.dot(p.astype(BF16), v, preferred_element_type=F32).astype(BF16))
    o = jnp.concatenate(outs, axis=1)
    x2 = x + jnp.dot(o, wo_ref[...], preferred_element_type=F32)
    x_out_ref[...] = x2
    h_out_ref[...] = _rmsnorm_rows(x2, gf_ref[...]).astype(h_out_ref.dtype)


def _xattn(x, g_x, w_q, kv, w_o, g_ffn, *, seq, n_mem, bm=256):
    t, d = x.shape
    per_batch = seq // bm
    return pl.pallas_call(
        _xattn_kernel,
        grid=(t // bm,),
        in_specs=[pl.BlockSpec((bm, d), lambda i: (i, 0)),
                  pl.BlockSpec((1, d), lambda i: (0, 0)),
                  pl.BlockSpec(w_q.shape, lambda i: (0, 0)),
                  pl.BlockSpec((n_mem, kv.shape[1]), lambda i: (i // per_batch, 0)),
                  pl.BlockSpec(w_o.shape, lambda i: (0, 0)),
                  pl.BlockSpec((1, d), lambda i: (0, 0))],
        out_specs=[pl.BlockSpec((bm, d), lambda i: (i, 0)),
                   pl.BlockSpec((bm, d), lambda i: (i, 0))],
        out_shape=[jax.ShapeDtypeStruct((t, d), F32),
                   jax.ShapeDtypeStruct((t, d), BF16)],
        compiler_params=_cparams(("arbitrary",)),
        name="mem_cross_attention",
    )(x, g_x.reshape(1, d), w_q, kv, w_o, g_ffn.reshape(1, d))


def _ffn_up_kernel(h_ref, wg_ref, wu_ref, o_ref):
    h = h_ref[...]
    g = jnp.dot(h, wg_ref[...], preferred_element_type=F32)
    u = jnp.dot(h, wu_ref[...], preferred_element_type=F32)
    o_ref[...] = (g * _sigmoid(g) * u).astype(o_ref.dtype)


def _ffn_up(h, w_g, w_u, *, bm=1024, bn=512):
    t, d = h.shape
    f = w_g.shape[1]
    return pl.pallas_call(
        _ffn_up_kernel,
        grid=(t // bm, f // bn),
        in_specs=[pl.BlockSpec((bm, d), lambda m, n: (m, 0)),
                  pl.BlockSpec((d, bn), lambda m, n: (0, n)),
                  pl.BlockSpec((d, bn), lambda m, n: (0, n))],
        out_specs=pl.BlockSpec((bm, bn), lambda m, n: (m, n)),
        out_shape=jax.ShapeDtypeStruct((t, f), BF16),
        compiler_params=_cparams(("arbitrary", "arbitrary")),
        name="ffn_up",
    )(h, w_g, w_u)


def _ffn_down_kernel(a_ref, w_ref, x_ref, g_ref, o_ref, *, final_norm):
    k = pl.program_id(1)
    bm, d = o_ref.shape

    @pl.when(k == 0)
    def _residual():
        o_ref[...] = x_ref[...]

    a = a_ref[...]
    for c in range(d // DOWN_COLS):
        cols = slice(c * DOWN_COLS, (c + 1) * DOWN_COLS)
        o_ref[:, cols] += jnp.dot(a, w_ref[:, cols], preferred_element_type=F32)

    if final_norm:
        @pl.when(k == pl.num_programs(1) - 1)
        def _final_norm():
            for r in range(bm // NORM_ROWS):
                rows = slice(r * NORM_ROWS, (r + 1) * NORM_ROWS)
                o_ref[rows, :] = _rmsnorm_rows(o_ref[rows, :], g_ref[...])


def _ffn_down(a, w, x, g, *, final_norm, bm=512, bk=1024):
    t, f = a.shape
    d = w.shape[1]
    return pl.pallas_call(
        functools.partial(_ffn_down_kernel, final_norm=final_norm),
        grid=(t // bm, f // bk),
        in_specs=[pl.BlockSpec((bm, bk), lambda m, k: (m, k)),
                  pl.BlockSpec((bk, d), lambda m, k: (k, 0)),
                  pl.BlockSpec((bm, d), lambda m, k: (m, 0)),
                  pl.BlockSpec((1, d), lambda m, k: (0, 0))],
        out_specs=pl.BlockSpec((bm, d), lambda m, k: (m, 0)),
        out_shape=jax.ShapeDtypeStruct((t, d), F32),
        compiler_params=_cparams(("arbitrary", "arbitrary")),
        name="ffn_down",
    )(a, w, x, g.reshape(1, d))


@jax.jit
def kernel(x, mem, norm_mix_g, w_in, sgu_ln_g, sgu_ln_b, w_sgu, b_sgu, w_branch_a, w_branch_b, w_out, norm_xattn_g, norm_mem_g, w_xq, w_xkv, w_xo, norm_ffn_g, w_ff_gate, w_ff_up, w_ff_down, norm_final_g):
    batch, seq, d = x.shape
    n_mem = mem.shape[1]
    depth = w_in.shape[0]
    sgu_width = sgu_ln_g.shape[1]
    d_ff = w_ff_gate.shape[2]
    ff_pad = (-d_ff) % FF_PAD
    gate_col = 3 * MOBA_WIDTH + 2 * sgu_width

    xt = x.reshape(batch * seq, d)
    memt = mem.reshape(batch * n_mem, d)
    cos, sin = _rope_table(seq)

    for l in range(depth):
        h = _rmsnorm(xt, norm_mix_g[l], BF16)
        z = _inproj(h, w_in[l].astype(BF16), cos, sin, seq=seq, sgu_width=sgu_width)
        o_a = _moba(z, batch=batch, seq=seq)
        o_b = _sgu(z, sgu_ln_g[l], sgu_ln_b[l], w_sgu[l], jnp.transpose(b_sgu[l]), width=sgu_width)
        merged = _merge(o_a, o_b, w_branch_a[l].astype(BF16), w_branch_b[l].astype(BF16), z,
                        gate_col=gate_col)
        xt = _outproj(merged, w_out[l].astype(BF16), xt)

        kv = _memkv(memt, norm_mem_g[l], w_xkv[l].astype(BF16), n_mem=n_mem)
        xt, h = _xattn(xt, norm_xattn_g[l], w_xq[l].astype(BF16), kv, w_xo[l].astype(BF16),
                       norm_ffn_g[l], seq=seq, n_mem=n_mem)

        w_g = jnp.pad(w_ff_gate[l].astype(BF16), ((0, 0), (0, ff_pad)))
        w_u = jnp.pad(w_ff_up[l].astype(BF16), ((0, 0), (0, ff_pad)))
        w_d = jnp.pad(w_ff_down[l].astype(BF16), ((0, ff_pad), (0, 0)))
        a = _ffn_up(h, w_g, w_u)
        xt = _ffn_down(a, w_d, xt, norm_final_g, final_norm=(l == depth - 1))
    return xt.reshape(batch, seq, d)
```

```python
import functools
import math

import jax
import jax.numpy as jnp
from jax import lax
from jax.experimental import pallas as pl
from jax.experimental.pallas import tpu as pltpu

F32 = jnp.float32
BF16 = jnp.bfloat16

HEAD_DIM = 128
MOBA_HEADS = 16
MOBA_WIDTH = MOBA_HEADS * HEAD_DIM
MOBA_BLOCK = 256
MOBA_TOPK = 3
SGU_CHUNK = 128
SGU_GROUPS = 16
XATTN_HEADS = 4
XATTN_WIDTH = XATTN_HEADS * HEAD_DIM
RMS_EPS = 1e-6
LN_EPS = 1e-5
ROPE_THETA = 10000.0
NEG_INF = -1e30
REMOVED = -3e38
Q_SCALE = HEAD_DIM ** -0.5 * math.log2(math.e)

LANES = 128
FF_PAD = 512
PROJ_COLS = 256
DOWN_COLS = 1024
NORM_ROWS = 64
VMEM_LIMIT = 56 * 1024 * 1024


def _cparams(sem):
    return pltpu.CompilerParams(dimension_semantics=sem, vmem_limit_bytes=VMEM_LIMIT)


def _rmsnorm_rows(x, g):
    ms = jnp.mean(x * x, axis=-1, keepdims=True)
    return x * lax.rsqrt(ms + RMS_EPS) * g


def _rmsnorm_kernel(x_ref, g_ref, o_ref):
    o_ref[...] = _rmsnorm_rows(x_ref[...], g_ref[...]).astype(o_ref.dtype)


def _rmsnorm(x, g, out_dtype, bm=256):
    t, d = x.shape
    return pl.pallas_call(
        _rmsnorm_kernel,
        grid=(t // bm,),
        in_specs=[pl.BlockSpec((bm, d), lambda i: (i, 0)),
                  pl.BlockSpec((1, d), lambda i: (0, 0))],
        out_specs=pl.BlockSpec((bm, d), lambda i: (i, 0)),
        out_shape=jax.ShapeDtypeStruct((t, d), out_dtype),
        compiler_params=_cparams(("arbitrary",)),
        name="rmsnorm",
    )(x, g.reshape(1, d))


def _rope_kernel(freq_ref, cos_ref, sin_ref, *, rows):
    i = pl.program_id(0)
    pos = (i * rows + lax.broadcasted_iota(jnp.int32, (rows, LANES), 0)).astype(F32)
    lane = lax.broadcasted_iota(jnp.int32, (rows, LANES), 1)
    ang = pos * freq_ref[...]
    cos_ref[...] = jnp.cos(ang)
    s = jnp.sin(ang)
    sin_ref[...] = jnp.where(lane < HEAD_DIM // 2, -s, s)


def _rope_table(seq, rows=1024):
    half = HEAD_DIM // 2
    inv_freq = jnp.power(ROPE_THETA, -(jnp.arange(half, dtype=F32) * 2.0 / HEAD_DIM))
    freq = jnp.concatenate([inv_freq, inv_freq]).reshape(1, HEAD_DIM)
    return pl.pallas_call(
        functools.partial(_rope_kernel, rows=rows),
        grid=(seq // rows,),
        in_specs=[pl.BlockSpec((1, HEAD_DIM), lambda i: (0, 0))],
        out_specs=[pl.BlockSpec((rows, HEAD_DIM), lambda i: (i, 0))] * 2,
        out_shape=[jax.ShapeDtypeStruct((seq, HEAD_DIM), F32)] * 2,
        compiler_params=_cparams(("arbitrary",)),
        name="rope_table",
    )(freq)


def _gelu_tanh(x):
    c = math.sqrt(2.0 / math.pi)
    return x * (0.5 * (1.0 + jnp.tanh(c * (x + 0.044715 * (x * x * x)))))


def _sigmoid(x):
    return 1.0 / (1.0 + jnp.exp(-x))


def _inproj_kernel(h_ref, w_ref, cos_ref, sin_ref, z_ref, *, bn, n_q, n_k, n_v, n_act):
    n = pl.program_id(1)

    def project(epilogue):
        h = h_ref[...]
        for c in range(bn // PROJ_COLS):
            cols = slice(c * PROJ_COLS, (c + 1) * PROJ_COLS)
            blk = jnp.dot(h, w_ref[:, cols], preferred_element_type=F32)
            z_ref[:, cols] = epilogue(blk).astype(z_ref.dtype)

    @pl.when(n < n_k)
    def _rotary():
        scale = jnp.where(n < n_q, Q_SCALE, 1.0).astype(F32)
        cos = cos_ref[...] * scale
        sin = sin_ref[...] * scale

        def rotate(blk):
            heads = []
            for hh in range(PROJ_COLS // HEAD_DIM):
                x = blk[:, hh * HEAD_DIM:(hh + 1) * HEAD_DIM]
                heads.append(x * cos + pltpu.roll(x, HEAD_DIM // 2, axis=1) * sin)
            return jnp.concatenate(heads, axis=1)

        project(rotate)

    @pl.when((n >= n_k) & (n < n_v))
    def _plain():
        project(lambda blk: blk)

    @pl.when((n >= n_v) & (n < n_act))
    def _gelu():
        project(_gelu_tanh)

    @pl.when(n >= n_act)
    def _gate():
        project(_sigmoid)


def _inproj(h, w, cos, sin, *, seq, sgu_width, bm=1024, bn=1024):
    t, d = h.shape
    cols = w.shape[1]
    seq_blocks = seq // bm
    n_q = MOBA_WIDTH // bn
    n_k = 2 * MOBA_WIDTH // bn
    n_v = 3 * MOBA_WIDTH // bn
    n_act = (3 * MOBA_WIDTH + 2 * sgu_width) // bn
    kern = functools.partial(_inproj_kernel, bn=bn, n_q=n_q, n_k=n_k, n_v=n_v, n_act=n_act)
    return pl.pallas_call(
        kern,
        grid=(t // bm, cols // bn),
        in_specs=[pl.BlockSpec((bm, d), lambda m, n: (m, 0)),
                  pl.BlockSpec((d, bn), lambda m, n: (0, n)),
                  pl.BlockSpec((bm, HEAD_DIM), lambda m, n: (m % seq_blocks, 0)),
                  pl.BlockSpec((bm, HEAD_DIM), lambda m, n: (m % seq_blocks, 0))],
        out_specs=pl.BlockSpec((bm, bn), lambda m, n: (m, n)),
        out_shape=jax.ShapeDtypeStruct((t, cols), BF16),
        compiler_params=_cparams(("arbitrary", "arbitrary")),
        name="in_proj",
    )(h, w, cos, sin)


def _moba_kernel(q_ref, k_ref, v_ref, o_ref, kext_ref, vext_ref, kmean_ref, p_ref, m_ref, alpha_ref, acc_ref,
                 *, tile, n_blocks, n_sub, heads):
    qi = pl.program_id(2)
    blocks_per_tile = tile // MOBA_BLOCK
    contract_last = (((1,), (1,)), ((), ()))
    sub = tile // n_sub

    chains = [(hd, i) for hd in range(heads) for i in range(n_sub)]

    def head_cols(hd):
        return slice(hd * HEAD_DIM, (hd + 1) * HEAD_DIM)

    @pl.when(qi == 0)
    def _prepare_keys():
        kmean_ref[...] = jnp.zeros_like(kmean_ref)
        lane = lax.broadcasted_iota(jnp.int32, (MOBA_BLOCK, LANES), 1)
        ones_col = (lane == 0).astype(BF16)
        for hd in range(heads):
            for j in range(n_blocks):
                rows = pl.ds(j * MOBA_BLOCK, MOBA_BLOCK)
                kb = k_ref[rows, head_cols(hd)]
                kext_ref[hd, rows, :HEAD_DIM] = kb
                kext_ref[hd, rows, HEAD_DIM:] = (lane == j).astype(BF16)
                vext_ref[hd, rows, :HEAD_DIM] = v_ref[rows, head_cols(hd)]
                vext_ref[hd, rows, HEAD_DIM:] = ones_col
                kmean_ref[hd, j:j + 1, :] = jnp.mean(kb.astype(F32), axis=0, keepdims=True)

    blk = lax.broadcasted_iota(jnp.int32, (n_blocks, tile), 0)
    col = lax.broadcasted_iota(jnp.int32, (n_blocks, tile), 1)
    own_block = qi * blocks_per_tile + col // MOBA_BLOCK
    q_ext = []
    for hd in range(heads):
        q = q_ref[:, head_cols(hd)]
        gate = lax.dot_general(kmean_ref[hd].astype(BF16), q, contract_last,
                               preferred_element_type=F32)[:n_blocks, :]
        gate = jnp.where(blk < own_block, gate, NEG_INF)
        allowed = blk == own_block
        for _ in range(MOBA_TOPK):
            best = jnp.max(gate, axis=0, keepdims=True)
            first = jnp.min(jnp.where(gate == best, blk, n_blocks), axis=0, keepdims=True)
            hit = blk == first
            allowed = allowed | (hit & (best > 0.5 * NEG_INF))
            gate = jnp.where(hit, REMOVED, gate)
        bias_t = jnp.where(allowed, 0.0, NEG_INF)
        bias_t = jnp.concatenate([bias_t, jnp.zeros((LANES - n_blocks, tile), F32)], axis=0)
        bias = jnp.transpose(bias_t).astype(BF16)
        q_ext.append(jnp.concatenate([q, bias], axis=1))

    def keys(hd, j, rows=tile):
        return kext_ref[hd, pl.ds(pl.multiple_of(j * tile, tile), rows), :]

    def values(hd, j):
        return vext_ref[hd, pl.ds(pl.multiple_of(j * tile, tile), tile), :]

    def scores(hd, i, k_tile):
        return lax.dot_general(q_ext[hd][i * sub:(i + 1) * sub], k_tile, contract_last,
                               preferred_element_type=F32)

    def weighted_values(hd, i, v_tile):
        return jnp.dot(p_ref[hd, i * sub:(i + 1) * sub, :], v_tile, preferred_element_type=F32)

    for hd, i in chains:
        rows = slice(i * sub, (i + 1) * sub)
        seen = (i + 1) * sub
        r = i * sub + lax.broadcasted_iota(jnp.int32, (sub, seen), 0)
        c = lax.broadcasted_iota(jnp.int32, (sub, seen), 1)
        s = jnp.where(c <= r, scores(hd, i, keys(hd, qi, seen)), NEG_INF)
        m0 = jnp.max(s, axis=1, keepdims=True)
        p_ref[hd, rows, :seen] = jnp.exp2(s - m0).astype(BF16)
        if seen < tile:
            p_ref[hd, rows, seen:] = jnp.zeros((sub, tile - seen), BF16)
        m_ref[hd, rows, :] = m0
        alpha_ref[hd, rows, :] = jnp.ones((sub, 1), F32)
        acc_ref[hd, rows, :] = jnp.zeros((sub, 2 * HEAD_DIM), F32)

    def accumulate(hd, i, v_tile):
        rows = slice(i * sub, (i + 1) * sub)
        return alpha_ref[hd, rows, :] * acc_ref[hd, rows, :] + weighted_values(hd, i, v_tile)

    def body(j, carry):
        j_prev = jnp.where(j == 0, qi, j - 1)
        k_tiles = [keys(hd, j) for hd in range(heads)]
        v_tiles = [values(hd, j_prev) for hd in range(heads)]
        for hd, i in chains:
            rows = slice(i * sub, (i + 1) * sub)
            acc_ref[hd, rows, :] = accumulate(hd, i, v_tiles[hd])
            s = scores(hd, i, k_tiles[hd])
            m = m_ref[hd, rows, :]
            m_new = jnp.maximum(m, jnp.max(s, axis=1, keepdims=True))
            p_ref[hd, rows, :] = jnp.exp2(s - m_new).astype(BF16)
            alpha_ref[hd, rows, :] = jnp.exp2(m - m_new)
            m_ref[hd, rows, :] = m_new
        return carry

    lax.fori_loop(0, qi, body, 0)
    j_last = jnp.where(qi == 0, qi, qi - 1)
    for hd, i in chains:
        acc = accumulate(hd, i, values(hd, j_last))
        o_ref[i * sub:(i + 1) * sub, head_cols(hd)] = (
            acc[:, :HEAD_DIM] / acc[:, HEAD_DIM:HEAD_DIM + 1]).astype(o_ref.dtype)


def _moba(z, *, batch, seq, tile=1024, n_sub=4, heads=2):
    t = z.shape[0]
    n_blocks = seq // MOBA_BLOCK
    tiles = seq // tile
    groups = MOBA_HEADS // heads
    width = heads * HEAD_DIM
    kern = functools.partial(_moba_kernel, tile=tile, n_blocks=n_blocks, n_sub=n_sub, heads=heads)
    return pl.pallas_call(
        kern,
        grid=(batch, groups, tiles),
        in_specs=[pl.BlockSpec((tile, width), lambda b, g, i: (b * tiles + i, g)),
                  pl.BlockSpec((seq, width), lambda b, g, i: (b, groups + g)),
                  pl.BlockSpec((seq, width), lambda b, g, i: (b, 2 * groups + g))],
        out_specs=pl.BlockSpec((tile, width), lambda b, g, i: (b * tiles + i, g)),
        out_shape=jax.ShapeDtypeStruct((t, MOBA_WIDTH), BF16),
        scratch_shapes=[pltpu.VMEM((heads, seq, 2 * HEAD_DIM), BF16),
                        pltpu.VMEM((heads, seq, 2 * HEAD_DIM), BF16),
                        pltpu.VMEM((heads, LANES, HEAD_DIM), F32),
                        pltpu.VMEM((heads, tile, tile), BF16),
                        pltpu.VMEM((heads, tile, 1), F32),
                        pltpu.VMEM((heads, tile, 1), F32),
                        pltpu.VMEM((heads, tile, 2 * HEAD_DIM), F32)],
        compiler_params=_cparams(("arbitrary", "arbitrary", "arbitrary")),
        name="moba_attention",
    )(z, z, z)


def _sgu_kernel(u_ref, v_ref, g_ref, b_ref, w_ref, bs_ref, o_ref, *, rows):
    v = v_ref[...].astype(F32)
    mu = jnp.mean(v, axis=-1, keepdims=True)
    var = jnp.mean(jnp.square(v - mu), axis=-1, keepdims=True)
    vln = ((v - mu) * lax.rsqrt(var + LN_EPS) * g_ref[...] + b_ref[...]).astype(BF16)
    t_idx = lax.broadcasted_iota(jnp.int32, (SGU_CHUNK, SGU_CHUNK), 0)
    s_idx = lax.broadcasted_iota(jnp.int32, (SGU_CHUNK, SGU_CHUNK), 1)
    causal = s_idx <= t_idx
    for g in range(SGU_GROUPS):
        w = jnp.where(causal, w_ref[g], 0.0).astype(BF16)
        bias = bs_ref[:, g:g + 1]
        cols = slice(g * LANES, (g + 1) * LANES)
        for ch in range(rows // SGU_CHUNK):
            rws = slice(ch * SGU_CHUNK, (ch + 1) * SGU_CHUNK)
            mixed = jnp.dot(w, vln[rws, cols], preferred_element_type=F32) + bias
            o_ref[rws, cols] = (u_ref[rws, cols].astype(F32) * mixed).astype(o_ref.dtype)


def _sgu(z, ln_g, ln_b, w_s, b_s_t, *, width, rows=512):
    t = z.shape[0]
    u_blk = 3 * MOBA_WIDTH // width
    kern = functools.partial(_sgu_kernel, rows=rows)
    return pl.pallas_call(
        kern,
        grid=(t // rows,),
        in_specs=[pl.BlockSpec((rows, width), lambda i: (i, u_blk)),
                  pl.BlockSpec((rows, width), lambda i: (i, u_blk + 1)),
                  pl.BlockSpec((1, width), lambda i: (0, 0)),
                  pl.BlockSpec((1, width), lambda i: (0, 0)),
                  pl.BlockSpec((SGU_GROUPS, SGU_CHUNK, SGU_CHUNK), lambda i: (0, 0, 0)),
                  pl.BlockSpec((SGU_CHUNK, SGU_GROUPS), lambda i: (0, 0))],
        out_specs=pl.BlockSpec((rows, width), lambda i: (i, 0)),
        out_shape=jax.ShapeDtypeStruct((t, width), BF16),
        compiler_params=_cparams(("arbitrary",)),
        name="spatial_gating",
    )(z, z, ln_g.reshape(1, width), ln_b.reshape(1, width), w_s, b_s_t)


def _merge_kernel(oa_ref, ob_ref, pa_ref, pb_ref, ga_ref, gb_ref, o_ref):
    a = jnp.dot(oa_ref[...], pa_ref[...], preferred_element_type=F32)
    b = jnp.dot(ob_ref[...], pb_ref[...], preferred_element_type=F32)
    o_ref[...] = (ga_ref[...].astype(F32) * a + gb_ref[...].astype(F32) * b).astype(o_ref.dtype)


def _merge(o_a, o_b, p_a, p_b, z, *, gate_col, bm=1024, bn=1024):
    t = o_a.shape[0]
    d = p_a.shape[1]
    ga_blk = gate_col // bn
    gb_blk = (gate_col + d) // bn
    return pl.pallas_call(
        _merge_kernel,
        grid=(t // bm, d // bn),
        in_specs=[pl.BlockSpec((bm, o_a.shape[1]), lambda m, n: (m, 0)),
                  pl.BlockSpec((bm, o_b.shape[1]), lambda m, n: (m, 0)),
                  pl.BlockSpec((p_a.shape[0], bn), lambda m, n: (0, n)),
                  pl.BlockSpec((p_b.shape[0], bn), lambda m, n: (0, n)),
                  pl.BlockSpec((bm, bn), lambda m, n: (m, ga_blk + n)),
                  pl.BlockSpec((bm, bn), lambda m, n: (m, gb_blk + n))],
        out_specs=pl.BlockSpec((bm, bn), lambda m, n: (m, n)),
        out_shape=jax.ShapeDtypeStruct((t, d), BF16),
        compiler_params=_cparams(("arbitrary", "arbitrary")),
        name="branch_merge",
    )(o_a, o_b, p_a, p_b, z, z)


def _outproj_kernel(a_ref, w_ref, x_ref, o_ref):
    o_ref[...] = x_ref[...] + jnp.dot(a_ref[...], w_ref[...], preferred_element_type=F32)


def _outproj(a, w, x, *, bm=1024, bn=512):
    t, k = a.shape
    d = w.shape[1]
    return pl.pallas_call(
        _outproj_kernel,
        grid=(t // bm, d // bn),
        in_specs=[pl.BlockSpec((bm, k), lambda m, n: (m, 0)),
                  pl.BlockSpec((k, bn), lambda m, n: (0, n)),
                  pl.BlockSpec((bm, bn), lambda m, n: (m, n))],
        out_specs=pl.BlockSpec((bm, bn), lambda m, n: (m, n)),
        out_shape=jax.ShapeDtypeStruct((t, d), F32),
        compiler_params=_cparams(("arbitrary", "arbitrary")),
        name="out_proj",
    )(a, w, x)


def _memkv_kernel(mem_ref, g_ref, w_ref, o_ref):
    mem_n = _rmsnorm_rows(mem_ref[...], g_ref[...]).astype(BF16)
    o_ref[...] = jnp.dot(mem_n, w_ref[...], preferred_element_type=F32).astype(o_ref.dtype)


def _memkv(mem, g, w, *, n_mem):
    t, d = mem.shape
    cols = w.shape[1]
    return pl.pallas_call(
        _memkv_kernel,
        grid=(t // n_mem,),
        in_specs=[pl.BlockSpec((n_mem, d), lambda i: (i, 0)),
                  pl.BlockSpec((1, d), lambda i: (0, 0)),
                  pl.BlockSpec((d, cols), lambda i: (0, 0))],
        out_specs=pl.BlockSpec((n_mem, cols), lambda i: (i, 0)),
        out_shape=jax.ShapeDtypeStruct((t, cols), BF16),
        compiler_params=_cparams(("arbitrary",)),
        name="mem_kv",
    )(mem, g.reshape(1, d), w)


def _xattn_kernel(x_ref, gx_ref, wq_ref, kv_ref, wo_ref, gf_ref, x_out_ref, h_out_ref):
    x = x_ref[...]
    h = _rmsnorm_rows(x, gx_ref[...]).astype(BF16)
    q = jnp.dot(h, wq_ref[...], preferred_element_type=F32) * (HEAD_DIM ** -0.5)
    q = q.astype(BF16)
    outs = []
    for hh in range(XATTN_HEADS):
        cols = slice(hh * HEAD_DIM, (hh + 1) * HEAD_DIM)
        k = kv_ref[:, hh * HEAD_DIM:(hh + 1) * HEAD_DIM]
        v = kv_ref[:, XATTN_WIDTH + hh * HEAD_DIM:XATTN_WIDTH + (hh + 1) * HEAD_DIM]
        s = lax.dot_general(q[:, cols], k, (((1,), (1,)), ((), ())), preferred_element_type=F32)
        s = s - jnp.max(s, axis=-1, keepdims=True)
        p = jnp.exp(s)
        p = p / jnp.sum(p, axis=-1, keepdims=True)
        outs.append(jnp.dot(p.astype(BF16), v, preferred_element_type=F32).astype(BF16))
    o = jnp.concatenate(outs, axis=1)
    x2 = x + jnp.dot(o, wo_ref[...], preferred_element_type=F32)
    x_out_ref[...] = x2
    h_out_ref[...] = _rmsnorm_rows(x2, gf_ref[...]).astype(h_out_ref.dtype)


def _xattn(x, g_x, w_q, kv, w_o, g_ffn, *, seq, n_mem, bm=256):
    t, d = x.shape
    per_batch = seq // bm
    return pl.pallas_call(
        _xattn_kernel,
        grid=(t // bm,),
        in_specs=[pl.BlockSpec((bm, d), lambda i: (i, 0)),
                  pl.BlockSpec((1, d), lambda i: (0, 0)),
                  pl.BlockSpec(w_q.shape, lambda i: (0, 0)),
                  pl.BlockSpec((n_mem, kv.shape[1]), lambda i: (i // per_batch, 0)),
                  pl.BlockSpec(w_o.shape, lambda i: (0, 0)),
                  pl.BlockSpec((1, d), lambda i: (0, 0))],
        out_specs=[pl.BlockSpec((bm, d), lambda i: (i, 0)),
                   pl.BlockSpec((bm, d), lambda i: (i, 0))],
        out_shape=[jax.ShapeDtypeStruct((t, d), F32),
                   jax.ShapeDtypeStruct((t, d), BF16)],
        compiler_params=_cparams(("arbitrary",)),
        name="mem_cross_attention",
    )(x, g_x.reshape(1, d), w_q, kv, w_o, g_ffn.reshape(1, d))


def _ffn_up_kernel(h_ref, wg_ref, wu_ref, o_ref, wg_bf, wu_bf, *, d_ff, bn):
    n = pl.program_id(0)

    @pl.when(pl.program_id(1) == 0)
    def _cast_weights():
        valid = lax.broadcasted_iota(jnp.int32, wg_bf.shape, 1) < d_ff - n * bn
        wg_bf[...] = jnp.where(valid, wg_ref[...], 0.0).astype(BF16)
        wu_bf[...] = jnp.where(valid, wu_ref[...], 0.0).astype(BF16)

    h = h_ref[...]
    g = jnp.dot(h, wg_bf[...], preferred_element_type=F32)
    u = jnp.dot(h, wu_bf[...], preferred_element_type=F32)
    o_ref[...] = (g * _sigmoid(g) * u).astype(o_ref.dtype)


def _ffn_up(h, w_g, w_u, *, f_pad, bm=512, bn=512):
    t, d = h.shape
    d_ff = w_g.shape[1]
    return pl.pallas_call(
        functools.partial(_ffn_up_kernel, d_ff=d_ff, bn=bn),
        grid=(f_pad // bn, t // bm),
        in_specs=[pl.BlockSpec((bm, d), lambda n, m: (m, 0)),
                  pl.BlockSpec((d, bn), lambda n, m: (0, n)),
                  pl.BlockSpec((d, bn), lambda n, m: (0, n))],
        out_specs=pl.BlockSpec((bm, bn), lambda n, m: (m, n)),
        out_shape=jax.ShapeDtypeStruct((t, f_pad), BF16),
        scratch_shapes=[pltpu.VMEM((d, bn), BF16), pltpu.VMEM((d, bn), BF16)],
        compiler_params=_cparams(("arbitrary", "arbitrary")),
        name="ffn_up",
    )(h, w_g, w_u)


def _ffn_down_kernel(a_ref, w_ref, x_ref, g_ref, o_ref, *, final_norm):
    k = pl.program_id(1)
    bm, d = o_ref.shape

    @pl.when(k == 0)
    def _residual():
        o_ref[...] = x_ref[...]

    a = a_ref[...]
    for c in range(d // DOWN_COLS):
        cols = slice(c * DOWN_COLS, (c + 1) * DOWN_COLS)
        o_ref[:, cols] += jnp.dot(a, w_ref[:, cols], preferred_element_type=F32)

    if final_norm:
        @pl.when(k == pl.num_programs(1) - 1)
        def _final_norm():
            for r in range(bm // NORM_ROWS):
                rows = slice(r * NORM_ROWS, (r + 1) * NORM_ROWS)
                o_ref[rows, :] = _rmsnorm_rows(o_ref[rows, :], g_ref[...])


def _ffn_down(a, w, x, g, *, final_norm, bm=512, bk=1024):
    t, f = a.shape
    d = w.shape[1]
    return pl.pallas_call(
        functools.partial(_ffn_down_kernel, final_norm=final_norm),
        grid=(t // bm, f // bk),
        in_specs=[pl.BlockSpec((bm, bk), lambda m, k: (m, k)),
                  pl.BlockSpec((bk, d), lambda m, k: (k, 0)),
                  pl.BlockSpec((bm, d), lambda m, k: (m, 0)),
                  pl.BlockSpec((1, d), lambda m, k: (0, 0))],
        out_specs=pl.BlockSpec((bm, d), lambda m, k: (m, 0)),
        out_shape=jax.ShapeDtypeStruct((t, d), F32),
        compiler_params=_cparams(("arbitrary", "arbitrary")),
        name="ffn_down",
    )(a, w, x, g.reshape(1, d))


@jax.jit
def kernel(x, mem, norm_mix_g, w_in, sgu_ln_g, sgu_ln_b, w_sgu, b_sgu, w_branch_a, w_branch_b, w_out, norm_xattn_g, norm_mem_g, w_xq, w_xkv, w_xo, norm_ffn_g, w_ff_gate, w_ff_up, w_ff_down, norm_final_g):
    batch, seq, d = x.shape
    n_mem = mem.shape[1]
    depth = w_in.shape[0]
    sgu_width = sgu_ln_g.shape[1]
    d_ff = w_ff_gate.shape[2]
    ff_pad = (-d_ff) % FF_PAD
    gate_col = 3 * MOBA_WIDTH + 2 * sgu_width

    xt = x.reshape(batch * seq, d)
    memt = mem.reshape(batch * n_mem, d)
    cos, sin = _rope_table(seq)

    for l in range(depth):
        h = _rmsnorm(xt, norm_mix_g[l], BF16)
        z = _inproj(h, w_in[l].astype(BF16), cos, sin, seq=seq, sgu_width=sgu_width)
        o_a = _moba(z, batch=batch, seq=seq)
        o_b = _sgu(z, sgu_ln_g[l], sgu_ln_b[l], w_sgu[l], jnp.transpose(b_sgu[l]), width=sgu_width)
        merged = _merge(o_a, o_b, w_branch_a[l].astype(BF16), w_branch_b[l].astype(BF16), z,
                        gate_col=gate_col)
        xt = _outproj(merged, w_out[l].astype(BF16), xt)

        kv = _memkv(memt, norm_mem_g[l], w_xkv[l].astype(BF16), n_mem=n_mem)
        xt, h = _xattn(xt, norm_xattn_g[l], w_xq[l].astype(BF16), kv, w_xo[l].astype(BF16),
                       norm_ffn_g[l], seq=seq, n_mem=n_mem)

        w_d = jnp.pad(w_ff_down[l].astype(BF16), ((0, ff_pad), (0, 0)))
        a = _ffn_up(h, w_ff_gate[l], w_ff_up[l], f_pad=d_ff + ff_pad)
        xt = _ffn_down(a, w_d, xt, norm_final_g, final_norm=(l == depth - 1))
    return xt.reshape(batch, seq, d)
```

```python
import functools
import math

import jax
import jax.numpy as jnp
from jax import lax
from jax.experimental import pallas as pl
from jax.experimental.pallas import tpu as pltpu

F32 = jnp.float32
BF16 = jnp.bfloat16

HEAD_DIM = 128
MOBA_HEADS = 16
MOBA_WIDTH = MOBA_HEADS * HEAD_DIM
MOBA_BLOCK = 256
MOBA_TOPK = 3
SGU_CHUNK = 128
SGU_GROUPS = 16
XATTN_HEADS = 4
XATTN_WIDTH = XATTN_HEADS * HEAD_DIM
RMS_EPS = 1e-6
LN_EPS = 1e-5
ROPE_THETA = 10000.0
NEG_INF = -1e30
REMOVED = -3e38
Q_SCALE = HEAD_DIM ** -0.5 * math.log2(math.e)

LANES = 128
BF16_ROWS = 16
FF_PAD = 512
PROJ_COLS = 256
DOWN_COLS = 1024
NORM_ROWS = 64
VMEM_LIMIT = 60 * 1024 * 1024


def _cparams(sem):
    return pltpu.CompilerParams(dimension_semantics=sem, vmem_limit_bytes=VMEM_LIMIT)


def _rmsnorm_rows(x, g):
    ms = jnp.mean(x * x, axis=-1, keepdims=True)
    return x * lax.rsqrt(ms + RMS_EPS) * g


def _rmsnorm_kernel(x_ref, g_ref, o_ref):
    o_ref[...] = _rmsnorm_rows(x_ref[...], g_ref[...]).astype(o_ref.dtype)


def _rmsnorm(x, g, out_dtype, bm=256):
    t, d = x.shape
    return pl.pallas_call(
        _rmsnorm_kernel,
        grid=(t // bm,),
        in_specs=[pl.BlockSpec((bm, d), lambda i: (i, 0)),
                  pl.BlockSpec((1, d), lambda i: (0, 0))],
        out_specs=pl.BlockSpec((bm, d), lambda i: (i, 0)),
        out_shape=jax.ShapeDtypeStruct((t, d), out_dtype),
        compiler_params=_cparams(("arbitrary",)),
        name="rmsnorm",
    )(x, g.reshape(1, d))


def _rope_kernel(freq_ref, cos_ref, sin_ref, *, rows):
    i = pl.program_id(0)
    pos = (i * rows + lax.broadcasted_iota(jnp.int32, (rows, LANES), 0)).astype(F32)
    lane = lax.broadcasted_iota(jnp.int32, (rows, LANES), 1)
    ang = pos * freq_ref[...]
    cos_ref[...] = jnp.cos(ang)
    s = jnp.sin(ang)
    sin_ref[...] = jnp.where(lane < HEAD_DIM // 2, -s, s)


def _rope_table(seq, rows=1024):
    half = HEAD_DIM // 2
    inv_freq = jnp.power(ROPE_THETA, -(jnp.arange(half, dtype=F32) * 2.0 / HEAD_DIM))
    freq = jnp.concatenate([inv_freq, inv_freq]).reshape(1, HEAD_DIM)
    return pl.pallas_call(
        functools.partial(_rope_kernel, rows=rows),
        grid=(seq // rows,),
        in_specs=[pl.BlockSpec((1, HEAD_DIM), lambda i: (0, 0))],
        out_specs=[pl.BlockSpec((rows, HEAD_DIM), lambda i: (i, 0))] * 2,
        out_shape=[jax.ShapeDtypeStruct((seq, HEAD_DIM), F32)] * 2,
        compiler_params=_cparams(("arbitrary",)),
        name="rope_table",
    )(freq)


def _gelu_tanh(x):
    c = math.sqrt(2.0 / math.pi)
    return x * (0.5 * (1.0 + jnp.tanh(c * (x + 0.044715 * (x * x * x)))))


def _sigmoid(x):
    return 1.0 / (1.0 + jnp.exp(-x))


def _inproj_kernel(h_ref, w_ref, cos_ref, sin_ref, z_ref, *, bn, n_q, n_k, n_v, n_act):
    n = pl.program_id(1)

    def project(epilogue):
        h = h_ref[...]
        for c in range(bn // PROJ_COLS):
            cols = slice(c * PROJ_COLS, (c + 1) * PROJ_COLS)
            blk = jnp.dot(h, w_ref[:, cols], preferred_element_type=F32)
            z_ref[:, cols] = epilogue(blk).astype(z_ref.dtype)

    @pl.when(n < n_k)
    def _rotary():
        scale = jnp.where(n < n_q, Q_SCALE, 1.0).astype(F32)
        cos = cos_ref[...] * scale
        sin = sin_ref[...] * scale

        def rotate(blk):
            heads = []
            for hh in range(PROJ_COLS // HEAD_DIM):
                x = blk[:, hh * HEAD_DIM:(hh + 1) * HEAD_DIM]
                heads.append(x * cos + pltpu.roll(x, HEAD_DIM // 2, axis=1) * sin)
            return jnp.concatenate(heads, axis=1)

        project(rotate)

    @pl.when((n >= n_k) & (n < n_v))
    def _plain():
        project(lambda blk: blk)

    @pl.when((n >= n_v) & (n < n_act))
    def _gelu():
        project(_gelu_tanh)

    @pl.when(n >= n_act)
    def _gate():
        project(_sigmoid)


def _inproj(h, w, cos, sin, *, seq, sgu_width, bm=1024, bn=1024):
    t, d = h.shape
    cols = w.shape[1]
    seq_blocks = seq // bm
    n_q = MOBA_WIDTH // bn
    n_k = 2 * MOBA_WIDTH // bn
    n_v = 3 * MOBA_WIDTH // bn
    n_act = (3 * MOBA_WIDTH + 2 * sgu_width) // bn
    kern = functools.partial(_inproj_kernel, bn=bn, n_q=n_q, n_k=n_k, n_v=n_v, n_act=n_act)
    return pl.pallas_call(
        kern,
        grid=(t // bm, cols // bn),
        in_specs=[pl.BlockSpec((bm, d), lambda m, n: (m, 0)),
                  pl.BlockSpec((d, bn), lambda m, n: (0, n)),
                  pl.BlockSpec((bm, HEAD_DIM), lambda m, n: (m % seq_blocks, 0)),
                  pl.BlockSpec((bm, HEAD_DIM), lambda m, n: (m % seq_blocks, 0))],
        out_specs=pl.BlockSpec((bm, bn), lambda m, n: (m, n)),
        out_shape=jax.ShapeDtypeStruct((t, cols), BF16),
        compiler_params=_cparams(("arbitrary", "arbitrary")),
        name="in_proj",
    )(h, w, cos, sin)


def _moba_kernel(q_ref, k_ref, v_ref, *refs, tile, n_blocks, n_sub, heads, stage_args, groups, tiles):
    n_stage = len(stage_args)
    stage_src, o_ref, stage_dst = refs[:n_stage], refs[n_stage], refs[n_stage + 1:2 * n_stage + 1]
    kext_ref, vext_ref, kmean_ref, p_ref, m_ref, alpha_ref, acc_ref = refs[2 * n_stage + 1:]
    _moba_body(q_ref, k_ref, v_ref, o_ref, kext_ref, vext_ref, kmean_ref, p_ref, m_ref, alpha_ref, acc_ref,
               tile=tile, n_blocks=n_blocks, n_sub=n_sub, heads=heads)
    step = (pl.program_id(0) * groups + pl.program_id(1)) * tiles + pl.program_id(2)
    for src, dst, args in zip(stage_src, stage_dst, stage_args):
        _stage_block(step, src, dst, **args)


def _moba_body(q_ref, k_ref, v_ref, o_ref, kext_ref, vext_ref, kmean_ref, p_ref, m_ref, alpha_ref, acc_ref,
               *, tile, n_blocks, n_sub, heads):
    qi = pl.program_id(2)
    blocks_per_tile = tile // MOBA_BLOCK
    contract_last = (((1,), (1,)), ((), ()))
    sub = tile // n_sub

    chains = [(hd, i) for hd in range(heads) for i in range(n_sub)]

    def head_cols(hd):
        return slice(hd * HEAD_DIM, (hd + 1) * HEAD_DIM)

    @pl.when(qi == 0)
    def _prepare_keys():
        kmean_ref[...] = jnp.zeros_like(kmean_ref)
        lane = lax.broadcasted_iota(jnp.int32, (MOBA_BLOCK, LANES), 1)
        ones_col = (lane == 0).astype(BF16)
        for hd in range(heads):
            for j in range(n_blocks):
                rows = pl.ds(j * MOBA_BLOCK, MOBA_BLOCK)
                kb = k_ref[rows, head_cols(hd)]
                kext_ref[hd, rows, :HEAD_DIM] = kb
                kext_ref[hd, rows, HEAD_DIM:] = (lane == j).astype(BF16)
                vext_ref[hd, rows, :HEAD_DIM] = v_ref[rows, head_cols(hd)]
                vext_ref[hd, rows, HEAD_DIM:] = ones_col
                kmean_ref[hd, j:j + 1, :] = jnp.mean(kb.astype(F32), axis=0, keepdims=True)

    blk = lax.broadcasted_iota(jnp.int32, (n_blocks, tile), 0)
    col = lax.broadcasted_iota(jnp.int32, (n_blocks, tile), 1)
    own_block = qi * blocks_per_tile + col // MOBA_BLOCK
    q_ext = []
    for hd in range(heads):
        q = q_ref[:, head_cols(hd)]
        gate = lax.dot_general(kmean_ref[hd].astype(BF16), q, contract_last,
                               preferred_element_type=F32)[:n_blocks, :]
        gate = jnp.where(blk < own_block, gate, NEG_INF)
        allowed = blk == own_block
        for _ in range(MOBA_TOPK):
            best = jnp.max(gate, axis=0, keepdims=True)
            first = jnp.min(jnp.where(gate == best, blk, n_blocks), axis=0, keepdims=True)
            hit = blk == first
            allowed = allowed | (hit & (best > 0.5 * NEG_INF))
            gate = jnp.where(hit, REMOVED, gate)
        bias_t = jnp.where(allowed, 0.0, NEG_INF)
        bias_t = jnp.concatenate([bias_t, jnp.zeros((LANES - n_blocks, tile), F32)], axis=0)
        bias = jnp.transpose(bias_t).astype(BF16)
        q_ext.append(jnp.concatenate([q, bias], axis=1))

    def keys(hd, j, rows=tile):
        return kext_ref[hd, pl.ds(pl.multiple_of(j * tile, tile), rows), :]

    def values(hd, j):
        return vext_ref[hd, pl.ds(pl.multiple_of(j * tile, tile), tile), :]

    def scores(hd, i, k_tile):
        return lax.dot_general(q_ext[hd][i * sub:(i + 1) * sub], k_tile, contract_last,
                               preferred_element_type=F32)

    def weighted_values(hd, i, v_tile):
        return jnp.dot(p_ref[hd, i * sub:(i + 1) * sub, :], v_tile, preferred_element_type=F32)

    for hd, i in chains:
        rows = slice(i * sub, (i + 1) * sub)
        seen = (i + 1) * sub
        r = i * sub + lax.broadcasted_iota(jnp.int32, (sub, seen), 0)
        c = lax.broadcasted_iota(jnp.int32, (sub, seen), 1)
        s = jnp.where(c <= r, scores(hd, i, keys(hd, qi, seen)), NEG_INF)
        m0 = jnp.max(s, axis=1, keepdims=True)
        p_ref[hd, rows, :seen] = jnp.exp2(s - m0).astype(BF16)
        if seen < tile:
            p_ref[hd, rows, seen:] = jnp.zeros((sub, tile - seen), BF16)
        m_ref[hd, rows, :] = m0
        alpha_ref[hd, rows, :] = jnp.ones((sub, 1), F32)
        acc_ref[hd, rows, :] = jnp.zeros((sub, 2 * HEAD_DIM), F32)

    def accumulate(hd, i, v_tile):
        rows = slice(i * sub, (i + 1) * sub)
        return alpha_ref[hd, rows, :] * acc_ref[hd, rows, :] + weighted_values(hd, i, v_tile)

    def body(j, carry):
        j_prev = jnp.where(j == 0, qi, j - 1)
        k_tiles = [keys(hd, j) for hd in range(heads)]
        v_tiles = [values(hd, j_prev) for hd in range(heads)]
        for hd, i in chains:
            rows = slice(i * sub, (i + 1) * sub)
            acc_ref[hd, rows, :] = accumulate(hd, i, v_tiles[hd])
            s = scores(hd, i, k_tiles[hd])
            m = m_ref[hd, rows, :]
            m_new = jnp.maximum(m, jnp.max(s, axis=1, keepdims=True))
            p_ref[hd, rows, :] = jnp.exp2(s - m_new).astype(BF16)
            alpha_ref[hd, rows, :] = jnp.exp2(m - m_new)
            m_ref[hd, rows, :] = m_new
        return carry

    lax.fori_loop(0, qi, body, 0)
    j_last = jnp.where(qi == 0, qi, qi - 1)
    for hd, i in chains:
        acc = accumulate(hd, i, values(hd, j_last))
        o_ref[i * sub:(i + 1) * sub, head_cols(hd)] = (
            acc[:, :HEAD_DIM] / acc[:, HEAD_DIM:HEAD_DIM + 1]).astype(o_ref.dtype)


def _stage_plan(shape, out_shape, steps):
    (r, c), (rows_out, cols_out) = shape, out_shape
    for col_blocks in (1, 2, 4):
        row_blocks = steps // col_blocks
        rb = rows_out // row_blocks
        if steps % col_blocks or rows_out % row_blocks or rb % BF16_ROWS:
            continue
        if col_blocks > 1 and (c != cols_out or (c // col_blocks) % LANES):
            continue
        return rb, col_blocks
    raise ValueError(f"no staging plan for {shape} -> {out_shape} in {steps} steps")


def _stage_block(step, src_ref, dst_ref, *, rows, col_blocks):
    rb, cb_in = src_ref.shape
    x = src_ref[...]
    row = (step // col_blocks) * rb + lax.broadcasted_iota(jnp.int32, x.shape, 0)
    dst_ref[:, :cb_in] = jnp.where(row < rows, x, 0.0).astype(dst_ref.dtype)
    if dst_ref.shape[1] > cb_in:
        dst_ref[:, cb_in:] = jnp.zeros((rb, dst_ref.shape[1] - cb_in), dst_ref.dtype)


def _moba(z, *, batch, seq, stage=(), tile=1024, n_sub=4, heads=2):
    t = z.shape[0]
    n_blocks = seq // MOBA_BLOCK
    tiles = seq // tile
    groups = MOBA_HEADS // heads
    width = heads * HEAD_DIM
    steps = batch * groups * tiles

    def step_of(b, g, i):
        return (b * groups + g) * tiles + i

    stage_in, stage_out, stage_shapes, stage_args = [], [], [], []
    for w, out_shape in stage:
        rb, col_blocks = _stage_plan(w.shape, out_shape, steps)
        cb_out = out_shape[1] // col_blocks
        cb_in = w.shape[1] if col_blocks == 1 else cb_out
        last = (w.shape[0] - 1) // rb
        stage_in.append(pl.BlockSpec(
            (rb, cb_in), lambda b, g, i, cbs=col_blocks, last=last:
            (jnp.minimum(step_of(b, g, i) // cbs, last), step_of(b, g, i) % cbs)))
        stage_out.append(pl.BlockSpec(
            (rb, cb_out), lambda b, g, i, cbs=col_blocks: (step_of(b, g, i) // cbs, step_of(b, g, i) % cbs)))
        stage_shapes.append(jax.ShapeDtypeStruct(out_shape, BF16))
        stage_args.append(dict(rows=w.shape[0], col_blocks=col_blocks))

    kern = functools.partial(_moba_kernel, tile=tile, n_blocks=n_blocks, n_sub=n_sub, heads=heads,
                             stage_args=tuple(stage_args), groups=groups, tiles=tiles)
    outs = pl.pallas_call(
        kern,
        grid=(batch, groups, tiles),
        in_specs=[pl.BlockSpec((tile, width), lambda b, g, i: (b * tiles + i, g)),
                  pl.BlockSpec((seq, width), lambda b, g, i: (b, groups + g)),
                  pl.BlockSpec((seq, width), lambda b, g, i: (b, 2 * groups + g))] + stage_in,
        out_specs=[pl.BlockSpec((tile, width), lambda b, g, i: (b * tiles + i, g))] + stage_out,
        out_shape=[jax.ShapeDtypeStruct((t, MOBA_WIDTH), BF16)] + stage_shapes,
        scratch_shapes=[pltpu.VMEM((heads, seq, 2 * HEAD_DIM), BF16),
                        pltpu.VMEM((heads, seq, 2 * HEAD_DIM), BF16),
                        pltpu.VMEM((heads, LANES, HEAD_DIM), F32),
                        pltpu.VMEM((heads, tile, tile), BF16),
                        pltpu.VMEM((heads, tile, 1), F32),
                        pltpu.VMEM((heads, tile, 1), F32),
                        pltpu.VMEM((heads, tile, 2 * HEAD_DIM), F32)],
        compiler_params=_cparams(("arbitrary", "arbitrary", "arbitrary")),
        name="moba_attention",
    )(z, z, z, *[w for w, _ in stage])
    return outs[0] if not stage else tuple(outs)


def _sgu_kernel(u_ref, v_ref, g_ref, b_ref, w_ref, bs_ref, o_ref, *, rows):
    v = v_ref[...].astype(F32)
    mu = jnp.mean(v, axis=-1, keepdims=True)
    var = jnp.mean(jnp.square(v - mu), axis=-1, keepdims=True)
    vln = ((v - mu) * lax.rsqrt(var + LN_EPS) * g_ref[...] + b_ref[...]).astype(BF16)
    t_idx = lax.broadcasted_iota(jnp.int32, (SGU_CHUNK, SGU_CHUNK), 0)
    s_idx = lax.broadcasted_iota(jnp.int32, (SGU_CHUNK, SGU_CHUNK), 1)
    causal = s_idx <= t_idx
    for g in range(SGU_GROUPS):
        w = jnp.where(causal, w_ref[g], 0.0).astype(BF16)
        bias = bs_ref[:, g:g + 1]
        cols = slice(g * LANES, (g + 1) * LANES)
        for ch in range(rows // SGU_CHUNK):
            rws = slice(ch * SGU_CHUNK, (ch + 1) * SGU_CHUNK)
            mixed = jnp.dot(w, vln[rws, cols], preferred_element_type=F32) + bias
            o_ref[rws, cols] = (u_ref[rws, cols].astype(F32) * mixed).astype(o_ref.dtype)


def _sgu(z, ln_g, ln_b, w_s, b_s_t, *, width, rows=512):
    t = z.shape[0]
    u_blk = 3 * MOBA_WIDTH // width
    kern = functools.partial(_sgu_kernel, rows=rows)
    return pl.pallas_call(
        kern,
        grid=(t // rows,),
        in_specs=[pl.BlockSpec((rows, width), lambda i: (i, u_blk)),
                  pl.BlockSpec((rows, width), lambda i: (i, u_blk + 1)),
                  pl.BlockSpec((1, width), lambda i: (0, 0)),
                  pl.BlockSpec((1, width), lambda i: (0, 0)),
                  pl.BlockSpec((SGU_GROUPS, SGU_CHUNK, SGU_CHUNK), lambda i: (0, 0, 0)),
                  pl.BlockSpec((SGU_CHUNK, SGU_GROUPS), lambda i: (0, 0))],
        out_specs=pl.BlockSpec((rows, width), lambda i: (i, 0)),
        out_shape=jax.ShapeDtypeStruct((t, width), BF16),
        compiler_params=_cparams(("arbitrary",)),
        name="spatial_gating",
    )(z, z, ln_g.reshape(1, width), ln_b.reshape(1, width), w_s, b_s_t)


def _merge_kernel(oa_ref, ob_ref, pa_ref, pb_ref, ga_ref, gb_ref, o_ref):
    a = jnp.dot(oa_ref[...], pa_ref[...], preferred_element_type=F32)
    b = jnp.dot(ob_ref[...], pb_ref[...], preferred_element_type=F32)
    o_ref[...] = (ga_ref[...].astype(F32) * a + gb_ref[...].astype(F32) * b).astype(o_ref.dtype)


def _merge(o_a, o_b, p_a, p_b, z, *, gate_col, bm=1024, bn=1024):
    t = o_a.shape[0]
    d = p_a.shape[1]
    ga_blk = gate_col // bn
    gb_blk = (gate_col + d) // bn
    return pl.pallas_call(
        _merge_kernel,
        grid=(t // bm, d // bn),
        in_specs=[pl.BlockSpec((bm, o_a.shape[1]), lambda m, n: (m, 0)),
                  pl.BlockSpec((bm, o_b.shape[1]), lambda m, n: (m, 0)),
                  pl.BlockSpec((p_a.shape[0], bn), lambda m, n: (0, n)),
                  pl.BlockSpec((p_b.shape[0], bn), lambda m, n: (0, n)),
                  pl.BlockSpec((bm, bn), lambda m, n: (m, ga_blk + n)),
                  pl.BlockSpec((bm, bn), lambda m, n: (m, gb_blk + n))],
        out_specs=pl.BlockSpec((bm, bn), lambda m, n: (m, n)),
        out_shape=jax.ShapeDtypeStruct((t, d), BF16),
        compiler_params=_cparams(("arbitrary", "arbitrary")),
        name="branch_merge",
    )(o_a, o_b, p_a, p_b, z, z)


def _outproj_kernel(a_ref, w_ref, x_ref, o_ref):
    o_ref[...] = x_ref[...] + jnp.dot(a_ref[...], w_ref[...], preferred_element_type=F32)


def _outproj(a, w, x, *, bm=1024, bn=512):
    t, k = a.shape
    d = w.shape[1]
    return pl.pallas_call(
        _outproj_kernel,
        grid=(t // bm, d // bn),
        in_specs=[pl.BlockSpec((bm, k), lambda m, n: (m, 0)),
                  pl.BlockSpec((k, bn), lambda m, n: (0, n)),
                  pl.BlockSpec((bm, bn), lambda m, n: (m, n))],
        out_specs=pl.BlockSpec((bm, bn), lambda m, n: (m, n)),
        out_shape=jax.ShapeDtypeStruct((t, d), F32),
        compiler_params=_cparams(("arbitrary", "arbitrary")),
        name="out_proj",
    )(a, w, x)


def _memkv_kernel(mem_ref, g_ref, w_ref, o_ref):
    mem_n = _rmsnorm_rows(mem_ref[...], g_ref[...]).astype(BF16)
    o_ref[...] = jnp.dot(mem_n, w_ref[...], preferred_element_type=F32).astype(o_ref.dtype)


def _memkv(mem, g, w, *, n_mem):
    t, d = mem.shape
    cols = w.shape[1]
    return pl.pallas_call(
        _memkv_kernel,
        grid=(t // n_mem,),
        in_specs=[pl.BlockSpec((n_mem, d), lambda i: (i, 0)),
                  pl.BlockSpec((1, d), lambda i: (0, 0)),
                  pl.BlockSpec((d, cols), lambda i: (0, 0))],
        out_specs=pl.BlockSpec((n_mem, cols), lambda i: (i, 0)),
        out_shape=jax.ShapeDtypeStruct((t, cols), BF16),
        compiler_params=_cparams(("arbitrary",)),
        name="mem_kv",
    )(mem, g.reshape(1, d), w)


def _xattn_kernel(x_ref, gx_ref, wq_ref, kv_ref, wo_ref, gf_ref, x_out_ref, h_out_ref):
    x = x_ref[...]
    h = _rmsnorm_rows(x, gx_ref[...]).astype(BF16)
    q = jnp.dot(h, wq_ref[...], preferred_element_type=F32) * (HEAD_DIM ** -0.5)
    q = q.astype(BF16)
    outs = []
    for hh in range(XATTN_HEADS):
        cols = slice(hh * HEAD_DIM, (hh + 1) * HEAD_DIM)
        k = kv_ref[:, hh * HEAD_DIM:(hh + 1) * HEAD_DIM]
        v = kv_ref[:, XATTN_WIDTH + hh * HEAD_DIM:XATTN_WIDTH + (hh + 1) * HEAD_DIM]
        s = lax.dot_general(q[:, cols], k, (((1,), (1,)), ((), ())), preferred_element_type=F32)
        s = s - jnp.max(s, axis=-1, keepdims=True)
        p = jnp.exp(s)
        p = p / jnp.sum(p, axis=-1, keepdims=True)
        outs.append(jnp.dot(p.astype(BF16), v, preferred_element_type=F32).astype(BF16))
    o = jnp.concatenate(outs, axis=1)
    x2 = x + jnp.dot(o, wo_ref[...], preferred_element_type=F32)
    x_out_ref[...] = x2
    h_out_ref[...] = _rmsnorm_rows(x2, gf_ref[...]).astype(h_out_ref.dtype)


def _xattn(x, g_x, w_q, kv, w_o, g_ffn, *, seq, n_mem, bm=256):
    t, d = x.shape
    per_batch = seq // bm
    return pl.pallas_call(
        _xattn_kernel,
        grid=(t // bm,),
        in_specs=[pl.BlockSpec((bm, d), lambda i: (i, 0)),
                  pl.BlockSpec((1, d), lambda i: (0, 0)),
                  pl.BlockSpec(w_q.shape, lambda i: (0, 0)),
                  pl.BlockSpec((n_mem, kv.shape[1]), lambda i: (i // per_batch, 0)),
                  pl.BlockSpec(w_o.shape, lambda i: (0, 0)),
                  pl.BlockSpec((1, d), lambda i: (0, 0))],
        out_specs=[pl.BlockSpec((bm, d), lambda i: (i, 0)),
                   pl.BlockSpec((bm, d), lambda i: (i, 0))],
        out_shape=[jax.ShapeDtypeStruct((t, d), F32),
                   jax.ShapeDtypeStruct((t, d), BF16)],
        compiler_params=_cparams(("arbitrary",)),
        name="mem_cross_attention",
    )(x, g_x.reshape(1, d), w_q, kv, w_o, g_ffn.reshape(1, d))


def _ffn_up_kernel(h_ref, wg_ref, wu_ref, o_ref):
    h = h_ref[...]
    g = jnp.dot(h, wg_ref[...], preferred_element_type=F32)
    u = jnp.dot(h, wu_ref[...], preferred_element_type=F32)
    o_ref[...] = (g * _sigmoid(g) * u).astype(o_ref.dtype)


def _ffn_up(h, w_g, w_u, *, bm=1024, bn=512):
    t, d = h.shape
    f = w_g.shape[1]
    return pl.pallas_call(
        _ffn_up_kernel,
        grid=(t // bm, f // bn),
        in_specs=[pl.BlockSpec((bm, d), lambda m, n: (m, 0)),
                  pl.BlockSpec((d, bn), lambda m, n: (0, n)),
                  pl.BlockSpec((d, bn), lambda m, n: (0, n))],
        out_specs=pl.BlockSpec((bm, bn), lambda m, n: (m, n)),
        out_shape=jax.ShapeDtypeStruct((t, f), BF16),
        compiler_params=_cparams(("arbitrary", "arbitrary")),
        name="ffn_up",
    )(h, w_g, w_u)


def _ffn_down_kernel(a_ref, w_ref, x_ref, g_ref, o_ref, *, final_norm):
    k = pl.program_id(1)
    bm, d = o_ref.shape

    @pl.when(k == 0)
    def _residual():
        o_ref[...] = x_ref[...]

    a = a_ref[...]
    for c in range(d // DOWN_COLS):
        cols = slice(c * DOWN_COLS, (c + 1) * DOWN_COLS)
        o_ref[:, cols] += jnp.dot(a, w_ref[:, cols], preferred_element_type=F32)

    if final_norm:
        @pl.when(k == pl.num_programs(1) - 1)
        def _final_norm():
            for r in range(bm // NORM_ROWS):
                rows = slice(r * NORM_ROWS, (r + 1) * NORM_ROWS)
                o_ref[rows, :] = _rmsnorm_rows(o_ref[rows, :], g_ref[...])


def _ffn_down(a, w, x, g, *, final_norm, bm=512, bk=1024):
    t, f = a.shape
    d = w.shape[1]
    return pl.pallas_call(
        functools.partial(_ffn_down_kernel, final_norm=final_norm),
        grid=(t // bm, f // bk),
        in_specs=[pl.BlockSpec((bm, bk), lambda m, k: (m, k)),
                  pl.BlockSpec((bk, d), lambda m, k: (k, 0)),
                  pl.BlockSpec((bm, d), lambda m, k: (m, 0)),
                  pl.BlockSpec((1, d), lambda m, k: (0, 0))],
        out_specs=pl.BlockSpec((bm, d), lambda m, k: (m, 0)),
        out_shape=jax.ShapeDtypeStruct((t, d), F32),
        compiler_params=_cparams(("arbitrary", "arbitrary")),
        name="ffn_down",
    )(a, w, x, g.reshape(1, d))


@jax.jit
def kernel(x, mem, norm_mix_g, w_in, sgu_ln_g, sgu_ln_b, w_sgu, b_sgu, w_branch_a, w_branch_b, w_out, norm_xattn_g, norm_mem_g, w_xq, w_xkv, w_xo, norm_ffn_g, w_ff_gate, w_ff_up, w_ff_down, norm_final_g):
    batch, seq, d = x.shape
    n_mem = mem.shape[1]
    depth = w_in.shape[0]
    sgu_width = sgu_ln_g.shape[1]
    d_ff = w_ff_gate.shape[2]
    ff_pad = (-d_ff) % FF_PAD
    gate_col = 3 * MOBA_WIDTH + 2 * sgu_width

    xt = x.reshape(batch * seq, d)
    memt = mem.reshape(batch * n_mem, d)
    cos, sin = _rope_table(seq)

    for l in range(depth):
        h = _rmsnorm(xt, norm_mix_g[l], BF16)
        z = _inproj(h, w_in[l].astype(BF16), cos, sin, seq=seq, sgu_width=sgu_width)
        f_pad = d_ff + ff_pad
        o_a, w_g, w_u, w_d = _moba(z, batch=batch, seq=seq, stage=(
            (w_ff_gate[l], (d, f_pad)), (w_ff_up[l], (d, f_pad)), (w_ff_down[l], (f_pad, d))))
        o_b = _sgu(z, sgu_ln_g[l], sgu_ln_b[l], w_sgu[l], jnp.transpose(b_sgu[l]), width=sgu_width)
        merged = _merge(o_a, o_b, w_branch_a[l].astype(BF16), w_branch_b[l].astype(BF16), z,
                        gate_col=gate_col)
        xt = _outproj(merged, w_out[l].astype(BF16), xt)

        kv = _memkv(memt, norm_mem_g[l], w_xkv[l].astype(BF16), n_mem=n_mem)
        xt, h = _xattn(xt, norm_xattn_g[l], w_xq[l].astype(BF16), kv, w_xo[l].astype(BF16),
                       norm_ffn_g[l], seq=seq, n_mem=n_mem)

        a = _ffn_up(h, w_g, w_u)
        xt = _ffn_down(a, w_d, xt, norm_final_g, final_norm=(l == depth - 1))
    return xt.reshape(batch, seq, d)
```

```python
import functools
import math

import jax
import jax.numpy as jnp
from jax import lax
from jax.experimental import pallas as pl
from jax.experimental.pallas import tpu as pltpu

F32 = jnp.float32
BF16 = jnp.bfloat16

HEAD_DIM = 128
MOBA_HEADS = 16
MOBA_WIDTH = MOBA_HEADS * HEAD_DIM
MOBA_BLOCK = 256
MOBA_TOPK = 3
SGU_CHUNK = 128
SGU_GROUPS = 16
XATTN_HEADS = 4
XATTN_WIDTH = XATTN_HEADS * HEAD_DIM
RMS_EPS = 1e-6
LN_EPS = 1e-5
ROPE_THETA = 10000.0
NEG_INF = -1e30
REMOVED = -3e38
Q_SCALE = HEAD_DIM ** -0.5 * math.log2(math.e)

LANES = 128
BF16_ROWS = 16
FF_PAD = 512
PROJ_COLS = 256
DOWN_COLS = 1024
NORM_ROWS = 64
VMEM_LIMIT = 60 * 1024 * 1024


def _cparams(sem):
    return pltpu.CompilerParams(dimension_semantics=sem, vmem_limit_bytes=VMEM_LIMIT)


def _rmsnorm_rows(x, g):
    ms = jnp.mean(x * x, axis=-1, keepdims=True)
    return x * lax.rsqrt(ms + RMS_EPS) * g


def _rmsnorm_kernel(x_ref, g_ref, o_ref):
    o_ref[...] = _rmsnorm_rows(x_ref[...], g_ref[...]).astype(o_ref.dtype)


def _rmsnorm(x, g, out_dtype, bm=256):
    t, d = x.shape
    return pl.pallas_call(
        _rmsnorm_kernel,
        grid=(t // bm,),
        in_specs=[pl.BlockSpec((bm, d), lambda i: (i, 0)),
                  pl.BlockSpec((1, d), lambda i: (0, 0))],
        out_specs=pl.BlockSpec((bm, d), lambda i: (i, 0)),
        out_shape=jax.ShapeDtypeStruct((t, d), out_dtype),
        compiler_params=_cparams(("arbitrary",)),
        name="rmsnorm",
    )(x, g.reshape(1, d))


def _rope_kernel(freq_ref, cos_ref, sin_ref, *, rows):
    i = pl.program_id(0)
    pos = (i * rows + lax.broadcasted_iota(jnp.int32, (rows, LANES), 0)).astype(F32)
    lane = lax.broadcasted_iota(jnp.int32, (rows, LANES), 1)
    ang = pos * freq_ref[...]
    cos_ref[...] = jnp.cos(ang)
    s = jnp.sin(ang)
    sin_ref[...] = jnp.where(lane < HEAD_DIM // 2, -s, s)


def _rope_table(seq, rows=1024):
    half = HEAD_DIM // 2
    inv_freq = jnp.power(ROPE_THETA, -(jnp.arange(half, dtype=F32) * 2.0 / HEAD_DIM))
    freq = jnp.concatenate([inv_freq, inv_freq]).reshape(1, HEAD_DIM)
    return pl.pallas_call(
        functools.partial(_rope_kernel, rows=rows),
        grid=(seq // rows,),
        in_specs=[pl.BlockSpec((1, HEAD_DIM), lambda i: (0, 0))],
        out_specs=[pl.BlockSpec((rows, HEAD_DIM), lambda i: (i, 0))] * 2,
        out_shape=[jax.ShapeDtypeStruct((seq, HEAD_DIM), F32)] * 2,
        compiler_params=_cparams(("arbitrary",)),
        name="rope_table",
    )(freq)


def _gelu_tanh(x):
    c = math.sqrt(2.0 / math.pi)
    half = 0.5 * x
    return half + half * jnp.tanh(x * (c + (c * 0.044715) * (x * x)))


def _sigmoid(x):
    return 1.0 / (1.0 + jnp.exp(-x))


def _inproj_kernel(h_ref, w_ref, cos_ref, sin_ref, z_ref, *, bn, n_q, n_k, n_v, n_act):
    n = pl.program_id(1)

    def project(epilogue):
        h = h_ref[...]
        for c in range(bn // PROJ_COLS):
            cols = slice(c * PROJ_COLS, (c + 1) * PROJ_COLS)
            blk = jnp.dot(h, w_ref[:, cols], preferred_element_type=F32)
            z_ref[:, cols] = epilogue(blk).astype(z_ref.dtype)

    @pl.when(n < n_k)
    def _rotary():
        scale = jnp.where(n < n_q, Q_SCALE, 1.0).astype(F32)
        cos = cos_ref[...] * scale
        sin = sin_ref[...] * scale

        def rotate(blk):
            heads = []
            for hh in range(PROJ_COLS // HEAD_DIM):
                x = blk[:, hh * HEAD_DIM:(hh + 1) * HEAD_DIM]
                heads.append(x * cos + pltpu.roll(x, HEAD_DIM // 2, axis=1) * sin)
            return jnp.concatenate(heads, axis=1)

        project(rotate)

    @pl.when((n >= n_k) & (n < n_v))
    def _plain():
        project(lambda blk: blk)

    @pl.when((n >= n_v) & (n < n_act))
    def _gelu():
        project(_gelu_tanh)

    @pl.when(n >= n_act)
    def _gate():
        project(lambda blk: 0.5 + 0.5 * jnp.tanh(0.5 * blk))


def _inproj(h, w, cos, sin, *, seq, sgu_width, bm=1024, bn=1024):
    t, d = h.shape
    cols = w.shape[1]
    seq_blocks = seq // bm
    n_q = MOBA_WIDTH // bn
    n_k = 2 * MOBA_WIDTH // bn
    n_v = 3 * MOBA_WIDTH // bn
    n_act = (3 * MOBA_WIDTH + 2 * sgu_width) // bn
    kern = functools.partial(_inproj_kernel, bn=bn, n_q=n_q, n_k=n_k, n_v=n_v, n_act=n_act)
    return pl.pallas_call(
        kern,
        grid=(t // bm, cols // bn),
        in_specs=[pl.BlockSpec((bm, d), lambda m, n: (m, 0)),
                  pl.BlockSpec((d, bn), lambda m, n: (0, n)),
                  pl.BlockSpec((bm, HEAD_DIM), lambda m, n: (m % seq_blocks, 0)),
                  pl.BlockSpec((bm, HEAD_DIM), lambda m, n: (m % seq_blocks, 0))],
        out_specs=pl.BlockSpec((bm, bn), lambda m, n: (m, n)),
        out_shape=jax.ShapeDtypeStruct((t, cols), BF16),
        compiler_params=_cparams(("arbitrary", "arbitrary")),
        name="in_proj",
    )(h, w, cos, sin)


def _moba_kernel(q_ref, k_ref, v_ref, *refs, tile, n_blocks, n_sub, heads, stage_args, groups, tiles):
    n_stage = len(stage_args)
    stage_src, o_ref, stage_dst = refs[:n_stage], refs[n_stage], refs[n_stage + 1:2 * n_stage + 1]
    kext_ref, vext_ref, kmean_ref, p_ref, m_ref, alpha_ref, acc_ref = refs[2 * n_stage + 1:]
    step = (pl.program_id(0) * groups + pl.program_id(1)) * tiles + pl.program_id(2)
    pieces = [p for src, dst, args in zip(stage_src, stage_dst, stage_args)
              for p in _stage_pieces(step, src, dst, **args)]

    def stage_weights(slot, n_slots):
        for p in pieces[slot::n_slots]:
            p()

    _moba_body(q_ref, k_ref, v_ref, o_ref, kext_ref, vext_ref, kmean_ref, p_ref, m_ref, alpha_ref, acc_ref,
               tile=tile, n_blocks=n_blocks, n_sub=n_sub, heads=heads, side_work=stage_weights)


def _moba_body(q_ref, k_ref, v_ref, o_ref, kext_ref, vext_ref, kmean_ref, p_ref, m_ref, alpha_ref, acc_ref,
               *, tile, n_blocks, n_sub, heads, side_work):
    qi = pl.program_id(2)
    blocks_per_tile = tile // MOBA_BLOCK
    contract_last = (((1,), (1,)), ((), ()))
    sub = tile // n_sub

    chains = [(hd, i) for hd in range(heads) for i in range(n_sub)]

    def head_cols(hd):
        return slice(hd * HEAD_DIM, (hd + 1) * HEAD_DIM)

    @pl.when(qi == 0)
    def _prepare_keys():
        kmean_ref[...] = jnp.zeros_like(kmean_ref)
        lane = lax.broadcasted_iota(jnp.int32, (MOBA_BLOCK, LANES), 1)
        ones_col = (lane == 0).astype(BF16)
        for hd in range(heads):
            for j in range(n_blocks):
                rows = pl.ds(j * MOBA_BLOCK, MOBA_BLOCK)
                kb = k_ref[rows, head_cols(hd)]
                kext_ref[hd, rows, :HEAD_DIM] = kb
                kext_ref[hd, rows, HEAD_DIM:] = (lane == j).astype(BF16)
                vext_ref[hd, rows, :HEAD_DIM] = v_ref[rows, head_cols(hd)]
                vext_ref[hd, rows, HEAD_DIM:] = ones_col
                kmean_ref[hd, j:j + 1, :] = jnp.mean(kb.astype(F32), axis=0, keepdims=True)

    blk = lax.broadcasted_iota(jnp.int32, (n_blocks, tile), 0)
    col = lax.broadcasted_iota(jnp.int32, (n_blocks, tile), 1)
    own_block = qi * blocks_per_tile + col // MOBA_BLOCK
    q_ext = []
    for hd in range(heads):
        q = q_ref[:, head_cols(hd)]
        gate = lax.dot_general(kmean_ref[hd].astype(BF16), q, contract_last,
                               preferred_element_type=F32)[:n_blocks, :]
        gate = jnp.where(blk < own_block, gate, NEG_INF)
        allowed = blk == own_block
        for _ in range(MOBA_TOPK):
            best = jnp.max(gate, axis=0, keepdims=True)
            first = jnp.min(jnp.where(gate == best, blk, n_blocks), axis=0, keepdims=True)
            hit = blk == first
            allowed = allowed | (hit & (best > 0.5 * NEG_INF))
            gate = jnp.where(hit, REMOVED, gate)
        bias_t = jnp.where(allowed, 0.0, NEG_INF)
        bias_t = jnp.concatenate([bias_t, jnp.zeros((LANES - n_blocks, tile), F32)], axis=0)
        bias = jnp.transpose(bias_t).astype(BF16)
        q_ext.append(jnp.concatenate([q, bias], axis=1))

    def keys(hd, j, rows=tile):
        return kext_ref[hd, pl.ds(pl.multiple_of(j * tile, tile), rows), :]

    def values(hd, j):
        return vext_ref[hd, pl.ds(pl.multiple_of(j * tile, tile), tile), :]

    def scores(hd, i, k_tile):
        return lax.dot_general(q_ext[hd][i * sub:(i + 1) * sub], k_tile, contract_last,
                               preferred_element_type=F32)

    def weighted_values(hd, i, v_tile):
        return jnp.dot(p_ref[hd, i * sub:(i + 1) * sub, :], v_tile, preferred_element_type=F32)

    for hd, i in chains:
        rows = slice(i * sub, (i + 1) * sub)
        seen = (i + 1) * sub
        r = i * sub + lax.broadcasted_iota(jnp.int32, (sub, seen), 0)
        c = lax.broadcasted_iota(jnp.int32, (sub, seen), 1)
        s = jnp.where(c <= r, scores(hd, i, keys(hd, qi, seen)), NEG_INF)
        m0 = jnp.max(s, axis=1, keepdims=True)
        p_ref[hd, rows, :seen] = jnp.exp2(s - m0).astype(BF16)
        if seen < tile:
            p_ref[hd, rows, seen:] = jnp.zeros((sub, tile - seen), BF16)
        m_ref[hd, rows, :] = m0
        alpha_ref[hd, rows, :] = jnp.ones((sub, 1), F32)
        acc_ref[hd, rows, :] = jnp.zeros((sub, 2 * HEAD_DIM), F32)

    def accumulate(hd, i, v_tile):
        rows = slice(i * sub, (i + 1) * sub)
        return alpha_ref[hd, rows, :] * acc_ref[hd, rows, :] + weighted_values(hd, i, v_tile)

    def body(j, carry):
        j_prev = jnp.where(j == 0, qi, j - 1)
        k_tiles = [keys(hd, j) for hd in range(heads)]
        v_tiles = [values(hd, j_prev) for hd in range(heads)]
        for hd, i in chains:
            rows = slice(i * sub, (i + 1) * sub)
            acc_ref[hd, rows, :] = accumulate(hd, i, v_tiles[hd])
            s = scores(hd, i, k_tiles[hd])
            m = m_ref[hd, rows, :]
            m_new = jnp.maximum(m, jnp.max(s, axis=1, keepdims=True))
            p_ref[hd, rows, :] = jnp.exp2(s - m_new).astype(BF16)
            alpha_ref[hd, rows, :] = jnp.exp2(m - m_new)
            m_ref[hd, rows, :] = m_new
        return carry

    lax.fori_loop(0, qi, body, 0)
    j_last = jnp.where(qi == 0, qi, qi - 1)
    for n, (hd, i) in enumerate(chains):
        acc = accumulate(hd, i, values(hd, j_last))
        o_ref[i * sub:(i + 1) * sub, head_cols(hd)] = (
            acc[:, :HEAD_DIM] / acc[:, HEAD_DIM:HEAD_DIM + 1]).astype(o_ref.dtype)
        side_work(n, len(chains))


def _stage_plan(shape, out_shape, steps):
    (r, c), (rows_out, cols_out) = shape, out_shape
    for col_blocks in (1, 2, 4):
        row_blocks = steps // col_blocks
        rb = rows_out // row_blocks
        if steps % col_blocks or rows_out % row_blocks or rb % BF16_ROWS:
            continue
        if col_blocks > 1 and (c != cols_out or (c // col_blocks) % LANES):
            continue
        return rb, col_blocks
    raise ValueError(f"no staging plan for {shape} -> {out_shape} in {steps} steps")


def _stage_pieces(step, src_ref, dst_ref, *, rows, col_blocks):
    rb, cb_in = src_ref.shape
    cb_out = dst_ref.shape[1]
    row0 = (step // col_blocks) * rb

    def piece(r0, nr, c0, nc):
        def run():
            x = src_ref[r0:r0 + nr, c0:c0 + nc]
            row = row0 + r0 + lax.broadcasted_iota(jnp.int32, x.shape, 0)
            dst_ref[r0:r0 + nr, c0:c0 + nc] = jnp.where(row < rows, x, 0.0).astype(dst_ref.dtype)
            if c0 + nc == cb_in and cb_out > cb_in:
                dst_ref[r0:r0 + nr, cb_in:] = jnp.zeros((nr, cb_out - cb_in), dst_ref.dtype)
        return run

    if (rb // 2) % BF16_ROWS == 0:
        return [piece(0, rb // 2, 0, cb_in), piece(rb // 2, rb // 2, 0, cb_in)]
    assert (cb_in // 2) % LANES == 0
    return [piece(0, rb, 0, cb_in // 2), piece(0, rb, cb_in // 2, cb_in // 2)]


def _moba(z, *, batch, seq, stage=(), tile=1024, n_sub=4, heads=2):
    t = z.shape[0]
    n_blocks = seq // MOBA_BLOCK
    tiles = seq // tile
    groups = MOBA_HEADS // heads
    width = heads * HEAD_DIM
    steps = batch * groups * tiles

    def step_of(b, g, i):
        return (b * groups + g) * tiles + i

    stage_in, stage_out, stage_shapes, stage_args = [], [], [], []
    for w, out_shape in stage:
        rb, col_blocks = _stage_plan(w.shape, out_shape, steps)
        cb_out = out_shape[1] // col_blocks
        cb_in = w.shape[1] if col_blocks == 1 else cb_out
        last = (w.shape[0] - 1) // rb
        stage_in.append(pl.BlockSpec(
            (rb, cb_in), lambda b, g, i, cbs=col_blocks, last=last:
            (jnp.minimum(step_of(b, g, i) // cbs, last), step_of(b, g, i) % cbs)))
        stage_out.append(pl.BlockSpec(
            (rb, cb_out), lambda b, g, i, cbs=col_blocks: (step_of(b, g, i) // cbs, step_of(b, g, i) % cbs)))
        stage_shapes.append(jax.ShapeDtypeStruct(out_shape, BF16))
        stage_args.append(dict(rows=w.shape[0], col_blocks=col_blocks))

    kern = functools.partial(_moba_kernel, tile=tile, n_blocks=n_blocks, n_sub=n_sub, heads=heads,
                             stage_args=tuple(stage_args), groups=groups, tiles=tiles)
    outs = pl.pallas_call(
        kern,
        grid=(batch, groups, tiles),
        in_specs=[pl.BlockSpec((tile, width), lambda b, g, i: (b * tiles + i, g)),
                  pl.BlockSpec((seq, width), lambda b, g, i: (b, groups + g)),
                  pl.BlockSpec((seq, width), lambda b, g, i: (b, 2 * groups + g))] + stage_in,
        out_specs=[pl.BlockSpec((tile, width), lambda b, g, i: (b * tiles + i, g))] + stage_out,
        out_shape=[jax.ShapeDtypeStruct((t, MOBA_WIDTH), BF16)] + stage_shapes,
        scratch_shapes=[pltpu.VMEM((heads, seq, 2 * HEAD_DIM), BF16),
                        pltpu.VMEM((heads, seq, 2 * HEAD_DIM), BF16),
                        pltpu.VMEM((heads, LANES, HEAD_DIM), F32),
                        pltpu.VMEM((heads, tile, tile), BF16),
                        pltpu.VMEM((heads, tile, 1), F32),
                        pltpu.VMEM((heads, tile, 1), F32),
                        pltpu.VMEM((heads, tile, 2 * HEAD_DIM), F32)],
        compiler_params=_cparams(("arbitrary", "arbitrary", "arbitrary")),
        name="moba_attention",
    )(z, z, z, *[w for w, _ in stage])
    return outs[0] if not stage else tuple(outs)


def _sgu_kernel(u_ref, v_ref, g_ref, b_ref, w_ref, bs_ref, o_ref, *, rows):
    v = v_ref[...].astype(F32)
    mu = jnp.mean(v, axis=-1, keepdims=True)
    var = jnp.mean(jnp.square(v - mu), axis=-1, keepdims=True)
    vln = ((v - mu) * lax.rsqrt(var + LN_EPS) * g_ref[...] + b_ref[...]).astype(BF16)
    t_idx = lax.broadcasted_iota(jnp.int32, (SGU_CHUNK, SGU_CHUNK), 0)
    s_idx = lax.broadcasted_iota(jnp.int32, (SGU_CHUNK, SGU_CHUNK), 1)
    causal = s_idx <= t_idx
    for g in range(SGU_GROUPS):
        w = jnp.where(causal, w_ref[g], 0.0).astype(BF16)
        bias = bs_ref[:, g:g + 1]
        cols = slice(g * LANES, (g + 1) * LANES)
        for ch in range(rows // SGU_CHUNK):
            rws = slice(ch * SGU_CHUNK, (ch + 1) * SGU_CHUNK)
            mixed = jnp.dot(w, vln[rws, cols], preferred_element_type=F32) + bias
            o_ref[rws, cols] = (u_ref[rws, cols].astype(F32) * mixed).astype(o_ref.dtype)


def _sgu(z, ln_g, ln_b, w_s, b_s_t, *, width, rows=512):
    t = z.shape[0]
    u_blk = 3 * MOBA_WIDTH // width
    kern = functools.partial(_sgu_kernel, rows=rows)
    return pl.pallas_call(
        kern,
        grid=(t // rows,),
        in_specs=[pl.BlockSpec((rows, width), lambda i: (i, u_blk)),
                  pl.BlockSpec((rows, width), lambda i: (i, u_blk + 1)),
                  pl.BlockSpec((1, width), lambda i: (0, 0)),
                  pl.BlockSpec((1, width), lambda i: (0, 0)),
                  pl.BlockSpec((SGU_GROUPS, SGU_CHUNK, SGU_CHUNK), lambda i: (0, 0, 0)),
                  pl.BlockSpec((SGU_CHUNK, SGU_GROUPS), lambda i: (0, 0))],
        out_specs=pl.BlockSpec((rows, width), lambda i: (i, 0)),
        out_shape=jax.ShapeDtypeStruct((t, width), BF16),
        compiler_params=_cparams(("arbitrary",)),
        name="spatial_gating",
    )(z, z, ln_g.reshape(1, width), ln_b.reshape(1, width), w_s, b_s_t)


def _merge_kernel(oa_ref, ob_ref, pa_ref, pb_ref, ga_ref, gb_ref, o_ref):
    a = jnp.dot(oa_ref[...], pa_ref[...], preferred_element_type=F32)
    b = jnp.dot(ob_ref[...], pb_ref[...], preferred_element_type=F32)
    o_ref[...] = (ga_ref[...].astype(F32) * a + gb_ref[...].astype(F32) * b).astype(o_ref.dtype)


def _merge(o_a, o_b, p_a, p_b, z, *, gate_col, bm=1024, bn=1024):
    t = o_a.shape[0]
    d = p_a.shape[1]
    ga_blk = gate_col // bn
    gb_blk = (gate_col + d) // bn
    return pl.pallas_call(
        _merge_kernel,
        grid=(t // bm, d // bn),
        in_specs=[pl.BlockSpec((bm, o_a.shape[1]), lambda m, n: (m, 0)),
                  pl.BlockSpec((bm, o_b.shape[1]), lambda m, n: (m, 0)),
                  pl.BlockSpec((p_a.shape[0], bn), lambda m, n: (0, n)),
                  pl.BlockSpec((p_b.shape[0], bn), lambda m, n: (0, n)),
                  pl.BlockSpec((bm, bn), lambda m, n: (m, ga_blk + n)),
                  pl.BlockSpec((bm, bn), lambda m, n: (m, gb_blk + n))],
        out_specs=pl.BlockSpec((bm, bn), lambda m, n: (m, n)),
        out_shape=jax.ShapeDtypeStruct((t, d), BF16),
        compiler_params=_cparams(("arbitrary", "arbitrary")),
        name="branch_merge",
    )(o_a, o_b, p_a, p_b, z, z)


def _outproj_kernel(a_ref, w_ref, x_ref, o_ref):
    o_ref[...] = x_ref[...] + jnp.dot(a_ref[...], w_ref[...], preferred_element_type=F32)


def _outproj(a, w, x, *, bm=1024, bn=512):
    t, k = a.shape
    d = w.shape[1]
    return pl.pallas_call(
        _outproj_kernel,
        grid=(t // bm, d // bn),
        in_specs=[pl.BlockSpec((bm, k), lambda m, n: (m, 0)),
                  pl.BlockSpec((k, bn), lambda m, n: (0, n)),
                  pl.BlockSpec((bm, bn), lambda m, n: (m, n))],
        out_specs=pl.BlockSpec((bm, bn), lambda m, n: (m, n)),
        out_shape=jax.ShapeDtypeStruct((t, d), F32),
        compiler_params=_cparams(("arbitrary", "arbitrary")),
        name="out_proj",
    )(a, w, x)


def _memkv_kernel(mem_ref, g_ref, w_ref, o_ref):
    mem_n = _rmsnorm_rows(mem_ref[...], g_ref[...]).astype(BF16)
    o_ref[...] = jnp.dot(mem_n, w_ref[...], preferred_element_type=F32).astype(o_ref.dtype)


def _memkv(mem, g, w, *, n_mem):
    t, d = mem.shape
    cols = w.shape[1]
    return pl.pallas_call(
        _memkv_kernel,
        grid=(t // n_mem,),
        in_specs=[pl.BlockSpec((n_mem, d), lambda i: (i, 0)),
                  pl.BlockSpec((1, d), lambda i: (0, 0)),
                  pl.BlockSpec((d, cols), lambda i: (0, 0))],
        out_specs=pl.BlockSpec((n_mem, cols), lambda i: (i, 0)),
        out_shape=jax.ShapeDtypeStruct((t, cols), BF16),
        compiler_params=_cparams(("arbitrary",)),
        name="mem_kv",
    )(mem, g.reshape(1, d), w)


def _xattn_kernel(x_ref, gx_ref, wq_ref, kv_ref, wo_ref, gf_ref, x_out_ref, h_out_ref, *, chain_rows):
    for c in range(x_ref.shape[0] // chain_rows):
        rows = slice(c * chain_rows, (c + 1) * chain_rows)
        x = x_ref[rows, :]
        h = _rmsnorm_rows(x, gx_ref[...]).astype(BF16)
        q = jnp.dot(h, wq_ref[...], preferred_element_type=F32) * (HEAD_DIM ** -0.5)
        q = q.astype(BF16)
        outs = []
        for hh in range(XATTN_HEADS):
            cols = slice(hh * HEAD_DIM, (hh + 1) * HEAD_DIM)
            k = kv_ref[:, hh * HEAD_DIM:(hh + 1) * HEAD_DIM]
            v = kv_ref[:, XATTN_WIDTH + hh * HEAD_DIM:XATTN_WIDTH + (hh + 1) * HEAD_DIM]
            s = lax.dot_general(q[:, cols], k, (((1,), (1,)), ((), ())), preferred_element_type=F32)
            s = s - jnp.max(s, axis=-1, keepdims=True)
            p = jnp.exp(s)
            p = p / jnp.sum(p, axis=-1, keepdims=True)
            outs.append(jnp.dot(p.astype(BF16), v, preferred_element_type=F32).astype(BF16))
        o = jnp.concatenate(outs, axis=1)
        x2 = x + jnp.dot(o, wo_ref[...], preferred_element_type=F32)
        x_out_ref[rows, :] = x2
        h_out_ref[rows, :] = _rmsnorm_rows(x2, gf_ref[...]).astype(h_out_ref.dtype)


def _xattn(x, g_x, w_q, kv, w_o, g_ffn, *, seq, n_mem, bm=512, chain_rows=256):
    t, d = x.shape
    per_batch = seq // bm
    resident = pl.Buffered(1)
    return pl.pallas_call(
        functools.partial(_xattn_kernel, chain_rows=chain_rows),
        grid=(t // bm,),
        in_specs=[pl.BlockSpec((bm, d), lambda i: (i, 0)),
                  pl.BlockSpec((1, d), lambda i: (0, 0)),
                  pl.BlockSpec(w_q.shape, lambda i: (0, 0), pipeline_mode=resident),
                  pl.BlockSpec((n_mem, kv.shape[1]), lambda i: (i // per_batch, 0)),
                  pl.BlockSpec(w_o.shape, lambda i: (0, 0), pipeline_mode=resident),
                  pl.BlockSpec((1, d), lambda i: (0, 0))],
        out_specs=[pl.BlockSpec((bm, d), lambda i: (i, 0)),
                   pl.BlockSpec((bm, d), lambda i: (i, 0))],
        out_shape=[jax.ShapeDtypeStruct((t, d), F32),
                   jax.ShapeDtypeStruct((t, d), BF16)],
        compiler_params=_cparams(("arbitrary",)),
        name="mem_cross_attention",
    )(x, g_x.reshape(1, d), w_q, kv, w_o, g_ffn.reshape(1, d))


def _ffn_up_kernel(h_ref, wg_ref, wu_ref, o_ref):
    h = h_ref[...]
    g = jnp.dot(h, wg_ref[...], preferred_element_type=F32)
    u = jnp.dot(h, wu_ref[...], preferred_element_type=F32)
    o_ref[...] = (g * _sigmoid(g) * u).astype(o_ref.dtype)


def _ffn_up(h, w_g, w_u, *, bm=1024, bn=512):
    t, d = h.shape
    f = w_g.shape[1]
    return pl.pallas_call(
        _ffn_up_kernel,
        grid=(t // bm, f // bn),
        in_specs=[pl.BlockSpec((bm, d), lambda m, n: (m, 0)),
                  pl.BlockSpec((d, bn), lambda m, n: (0, n)),
                  pl.BlockSpec((d, bn), lambda m, n: (0, n))],
        out_specs=pl.BlockSpec((bm, bn), lambda m, n: (m, n)),
        out_shape=jax.ShapeDtypeStruct((t, f), BF16),
        compiler_params=_cparams(("arbitrary", "arbitrary")),
        name="ffn_up",
    )(h, w_g, w_u)


def _ffn_down_kernel(a_ref, w_ref, x_ref, g_ref, o_ref, *, final_norm):
    k = pl.program_id(1)
    bm, d = o_ref.shape

    @pl.when(k == 0)
    def _residual():
        o_ref[...] = x_ref[...]

    a = a_ref[...]
    for c in range(d // DOWN_COLS):
        cols = slice(c * DOWN_COLS, (c + 1) * DOWN_COLS)
        o_ref[:, cols] += jnp.dot(a, w_ref[:, cols], preferred_element_type=F32)

    if final_norm:
        @pl.when(k == pl.num_programs(1) - 1)
        def _final_norm():
            for r in range(bm // NORM_ROWS):
                rows = slice(r * NORM_ROWS, (r + 1) * NORM_ROWS)
                o_ref[rows, :] = _rmsnorm_rows(o_ref[rows, :], g_ref[...])


def _ffn_down(a, w, x, g, *, final_norm, bm=512, bk=1024):
    t, f = a.shape
    d = w.shape[1]
    return pl.pallas_call(
        functools.partial(_ffn_down_kernel, final_norm=final_norm),
        grid=(t // bm, f // bk),
        in_specs=[pl.BlockSpec((bm, bk), lambda m, k: (m, k)),
                  pl.BlockSpec((bk, d), lambda m, k: (k, 0)),
                  pl.BlockSpec((bm, d), lambda m, k: (m, 0)),
                  pl.BlockSpec((1, d), lambda m, k: (0, 0))],
        out_specs=pl.BlockSpec((bm, d), lambda m, k: (m, 0)),
        out_shape=jax.ShapeDtypeStruct((t, d), F32),
        compiler_params=_cparams(("arbitrary", "arbitrary")),
        name="ffn_down",
    )(a, w, x, g.reshape(1, d))


@jax.jit
def kernel(x, mem, norm_mix_g, w_in, sgu_ln_g, sgu_ln_b, w_sgu, b_sgu, w_branch_a, w_branch_b, w_out, norm_xattn_g, norm_mem_g, w_xq, w_xkv, w_xo, norm_ffn_g, w_ff_gate, w_ff_up, w_ff_down, norm_final_g):
    batch, seq, d = x.shape
    n_mem = mem.shape[1]
    depth = w_in.shape[0]
    sgu_width = sgu_ln_g.shape[1]
    d_ff = w_ff_gate.shape[2]
    ff_pad = (-d_ff) % FF_PAD
    gate_col = 3 * MOBA_WIDTH + 2 * sgu_width

    xt = x.reshape(batch * seq, d)
    memt = mem.reshape(batch * n_mem, d)
    cos, sin = _rope_table(seq)

    for l in range(depth):
        h = _rmsnorm(xt, norm_mix_g[l], BF16)
        z = _inproj(h, w_in[l].astype(BF16), cos, sin, seq=seq, sgu_width=sgu_width)
        f_pad = d_ff + ff_pad
        o_a, w_g, w_u, w_d = _moba(z, batch=batch, seq=seq, stage=(
            (w_ff_gate[l], (d, f_pad)), (w_ff_up[l], (d, f_pad)), (w_ff_down[l], (f_pad, d))))
        o_b = _sgu(z, sgu_ln_g[l], sgu_ln_b[l], w_sgu[l], jnp.transpose(b_sgu[l]), width=sgu_width)
        merged = _merge(o_a, o_b, w_branch_a[l].astype(BF16), w_branch_b[l].astype(BF16), z,
                        gate_col=gate_col)
        xt = _outproj(merged, w_out[l].astype(BF16), xt)

        kv = _memkv(memt, norm_mem_g[l], w_xkv[l].astype(BF16), n_mem=n_mem)
        xt, h = _xattn(xt, norm_xattn_g[l], w_xq[l].astype(BF16), kv, w_xo[l].astype(BF16),
                       norm_ffn_g[l], seq=seq, n_mem=n_mem)

        a = _ffn_up(h, w_g, w_u)
        xt = _ffn_down(a, w_d, xt, norm_final_g, final_norm=(l == depth - 1))
    return xt.reshape(batch, seq, d)
```

```python
import functools
import math

import jax
import jax.numpy as jnp
from jax import lax
from jax.experimental import pallas as pl
from jax.experimental.pallas import tpu as pltpu

F32 = jnp.float32
BF16 = jnp.bfloat16

HEAD_DIM = 128
MOBA_HEADS = 16
MOBA_WIDTH = MOBA_HEADS * HEAD_DIM
MOBA_BLOCK = 256
MOBA_TOPK = 3
SGU_CHUNK = 128
SGU_GROUPS = 16
XATTN_HEADS = 4
XATTN_WIDTH = XATTN_HEADS * HEAD_DIM
RMS_EPS = 1e-6
LN_EPS = 1e-5
ROPE_THETA = 10000.0
NEG_INF = -1e30
REMOVED = -3e38
Q_SCALE = HEAD_DIM ** -0.5 * math.log2(math.e)

LANES = 128
BF16_ROWS = 16
FF_PAD = 512
INPROJ_COLS = 1024
PROJ_COLS = 256
DOWN_COLS = 1024
NORM_ROWS = 64
VMEM_LIMIT = 60 * 1024 * 1024


def _cparams(sem):
    return pltpu.CompilerParams(dimension_semantics=sem, vmem_limit_bytes=VMEM_LIMIT)


def _rmsnorm_rows(x, g):
    ms = jnp.mean(x * x, axis=-1, keepdims=True)
    return x * lax.rsqrt(ms + RMS_EPS) * g


def _rmsnorm_kernel(x_ref, g_ref, o_ref):
    o_ref[...] = _rmsnorm_rows(x_ref[...], g_ref[...]).astype(o_ref.dtype)


def _rmsnorm(x, g, out_dtype, bm=256):
    t, d = x.shape
    return pl.pallas_call(
        _rmsnorm_kernel,
        grid=(t // bm,),
        in_specs=[pl.BlockSpec((bm, d), lambda i: (i, 0)),
                  pl.BlockSpec((1, d), lambda i: (0, 0))],
        out_specs=pl.BlockSpec((bm, d), lambda i: (i, 0)),
        out_shape=jax.ShapeDtypeStruct((t, d), out_dtype),
        compiler_params=_cparams(("arbitrary",)),
        name="rmsnorm",
    )(x, g.reshape(1, d))


def _rope_kernel(freq_ref, cos_ref, sin_ref, *, rows):
    i = pl.program_id(0)
    pos = (i * rows + lax.broadcasted_iota(jnp.int32, (rows, LANES), 0)).astype(F32)
    lane = lax.broadcasted_iota(jnp.int32, (rows, LANES), 1)
    ang = pos * freq_ref[...]
    cos_ref[...] = jnp.cos(ang)
    s = jnp.sin(ang)
    sin_ref[...] = jnp.where(lane < HEAD_DIM // 2, -s, s)


def _rope_table(seq, rows=1024):
    half = HEAD_DIM // 2
    inv_freq = jnp.power(ROPE_THETA, -(jnp.arange(half, dtype=F32) * 2.0 / HEAD_DIM))
    freq = jnp.concatenate([inv_freq, inv_freq]).reshape(1, HEAD_DIM)
    return pl.pallas_call(
        functools.partial(_rope_kernel, rows=rows),
        grid=(seq // rows,),
        in_specs=[pl.BlockSpec((1, HEAD_DIM), lambda i: (0, 0))],
        out_specs=[pl.BlockSpec((rows, HEAD_DIM), lambda i: (i, 0))] * 2,
        out_shape=[jax.ShapeDtypeStruct((seq, HEAD_DIM), F32)] * 2,
        compiler_params=_cparams(("arbitrary",)),
        name="rope_table",
    )(freq)


def _gelu_tanh(x):
    c = math.sqrt(2.0 / math.pi)
    half = 0.5 * x
    return half + half * jnp.tanh(x * (c + (c * 0.044715) * (x * x)))


def _sigmoid(x):
    return 1.0 / (1.0 + jnp.exp(-x))


def _inproj_kernel(h_ref, w_ref, *refs, bn, segments):
    z_ref = refs[-1]
    n = pl.program_id(1)

    def project(epilogue):
        h = h_ref[...]
        for c in range(bn // PROJ_COLS):
            cols = slice(c * PROJ_COLS, (c + 1) * PROJ_COLS)
            blk = jnp.dot(h, w_ref[:, cols], preferred_element_type=F32)
            z_ref[:, cols] = epilogue(blk).astype(z_ref.dtype)

    def rotary(scale):
        def rotate(blk):
            cos, sin = refs[0][...], refs[1][...]
            heads = []
            for hh in range(PROJ_COLS // HEAD_DIM):
                x = blk[:, hh * HEAD_DIM:(hh + 1) * HEAD_DIM]
                heads.append(x * cos + pltpu.roll(x, HEAD_DIM // 2, axis=1) * sin)
            out = jnp.concatenate(heads, axis=1)
            return out if scale is None else out * scale
        return rotate

    epilogues = {
        "rope_q": rotary(Q_SCALE),
        "rope": rotary(None),
        "plain": lambda blk: blk,
        "gelu": _gelu_tanh,
        "sigmoid": lambda blk: 0.5 + 0.5 * jnp.tanh(0.5 * blk),
    }
    start = 0
    for kind, count in segments:
        pl.when((n >= start) & (n < start + count))(functools.partial(project, epilogues[kind]))
        start += count


def _inproj(h, w, rope, *, seq, segments, name, bm=1024, bn=1024):
    t, d = h.shape
    cols = w.shape[1]
    assert sum(count for _, count in segments) * bn == cols
    seq_blocks = seq // bm
    rope_spec = pl.BlockSpec((bm, HEAD_DIM), lambda m, n: (m % seq_blocks, 0))
    return pl.pallas_call(
        functools.partial(_inproj_kernel, bn=bn, segments=segments),
        grid=(t // bm, cols // bn),
        in_specs=[pl.BlockSpec((bm, d), lambda m, n: (m, 0)),
                  pl.BlockSpec((d, bn), lambda m, n: (0, n))] + [rope_spec] * len(rope),
        out_specs=pl.BlockSpec((bm, bn), lambda m, n: (m, n)),
        out_shape=jax.ShapeDtypeStruct((t, cols), BF16),
        compiler_params=_cparams(("arbitrary", "arbitrary")),
        name=name,
    )(h, w, *rope)


def _moba_kernel(q_ref, k_ref, v_ref, *refs, tile, n_blocks, n_sub, heads, stage_args, groups, tiles):
    n_stage = len(stage_args)
    stage_src, o_ref, stage_dst = refs[:n_stage], refs[n_stage], refs[n_stage + 1:2 * n_stage + 1]
    onehot_ref, ones_ref, kmean_ref, p_ref, m_ref, alpha_ref, acc_ref = refs[2 * n_stage + 1:]
    step = (pl.program_id(0) * groups + pl.program_id(1)) * tiles + pl.program_id(2)
    pieces = [p for src, dst, args in zip(stage_src, stage_dst, stage_args)
              for p in _stage_pieces(step, src, dst, **args)]

    def stage_weights(slot, n_slots):
        for p in pieces[slot::n_slots]:
            p()

    _moba_body(q_ref, k_ref, v_ref, o_ref, onehot_ref, ones_ref, kmean_ref, p_ref, m_ref, alpha_ref, acc_ref,
               tile=tile, n_blocks=n_blocks, n_sub=n_sub, heads=heads, side_work=stage_weights)


def _moba_body(q_ref, k_ref, v_ref, o_ref, onehot_ref, ones_ref, kmean_ref, p_ref, m_ref, alpha_ref, acc_ref,
               *, tile, n_blocks, n_sub, heads, side_work):
    qi = pl.program_id(2)
    blocks_per_tile = tile // MOBA_BLOCK
    contract_last = (((1,), (1,)), ((), ()))
    sub = tile // n_sub

    chains = [(hd, i) for hd in range(heads) for i in range(n_sub)]

    def head_cols(hd):
        return slice(hd * HEAD_DIM, (hd + 1) * HEAD_DIM)

    @pl.when(qi == 0)
    def _prepare_keys():
        kmean_ref[...] = jnp.zeros_like(kmean_ref)
        lane = lax.broadcasted_iota(jnp.int32, (MOBA_BLOCK, LANES), 1)
        ones_ref[...] = (lax.broadcasted_iota(jnp.int32, ones_ref.shape, 1) == 0).astype(BF16)
        for j in range(n_blocks):
            rows = pl.ds(j * MOBA_BLOCK, MOBA_BLOCK)
            onehot_ref[rows, :] = (lane == j).astype(BF16)
            for hd in range(heads):
                kb = k_ref[rows, head_cols(hd)]
                kmean_ref[hd, j:j + 1, :] = jnp.mean(kb.astype(F32), axis=0, keepdims=True)

    blk = lax.broadcasted_iota(jnp.int32, (n_blocks, tile), 0)
    col = lax.broadcasted_iota(jnp.int32, (n_blocks, tile), 1)
    own_block = qi * blocks_per_tile + col // MOBA_BLOCK
    q_ext = []
    for hd in range(heads):
        q = q_ref[:, head_cols(hd)]
        gate = lax.dot_general(kmean_ref[hd].astype(BF16), q, contract_last,
                               preferred_element_type=F32)[:n_blocks, :]
        gate = jnp.where(blk < own_block, gate, NEG_INF)
        allowed = blk == own_block
        for _ in range(MOBA_TOPK):
            best = jnp.max(gate, axis=0, keepdims=True)
            first = jnp.min(jnp.where(gate == best, blk, n_blocks), axis=0, keepdims=True)
            hit = blk == first
            allowed = allowed | (hit & (best > 0.5 * NEG_INF))
            gate = jnp.where(hit, REMOVED, gate)
        bias_t = jnp.where(allowed, 0.0, NEG_INF)
        bias_t = jnp.concatenate([bias_t, jnp.zeros((LANES - n_blocks, tile), F32)], axis=0)
        bias = jnp.transpose(bias_t).astype(BF16)
        q_ext.append(jnp.concatenate([q, bias], axis=1))

    def keys(hd, j, rows=tile):
        at = pl.ds(pl.multiple_of(j * tile, tile), rows)
        return jnp.concatenate([k_ref[at, head_cols(hd)], onehot_ref[at, :]], axis=1)

    def values(hd, j):
        at = pl.ds(pl.multiple_of(j * tile, tile), tile)
        return jnp.concatenate([v_ref[at, head_cols(hd)], ones_ref[...]], axis=1)

    def scores(hd, i, k_tile):
        return lax.dot_general(q_ext[hd][i * sub:(i + 1) * sub], k_tile, contract_last,
                               preferred_element_type=F32)

    def weighted_values(hd, i, v_tile):
        return jnp.dot(p_ref[hd, i * sub:(i + 1) * sub, :], v_tile, preferred_element_type=F32)

    for hd, i in chains:
        rows = slice(i * sub, (i + 1) * sub)
        seen = (i + 1) * sub
        r = i * sub + lax.broadcasted_iota(jnp.int32, (sub, seen), 0)
        c = lax.broadcasted_iota(jnp.int32, (sub, seen), 1)
        s = jnp.where(c <= r, scores(hd, i, keys(hd, qi, seen)), NEG_INF)
        m0 = jnp.max(s, axis=1, keepdims=True)
        p_ref[hd, rows, :seen] = jnp.exp2(s - m0).astype(BF16)
        if seen < tile:
            p_ref[hd, rows, seen:] = jnp.zeros((sub, tile - seen), BF16)
        m_ref[hd, rows, :] = m0
        alpha_ref[hd, rows, :] = jnp.ones((sub, 1), F32)
        acc_ref[hd, rows, :] = jnp.zeros((sub, 2 * HEAD_DIM), F32)

    def accumulate(hd, i, v_tile):
        rows = slice(i * sub, (i + 1) * sub)
        return alpha_ref[hd, rows, :] * acc_ref[hd, rows, :] + weighted_values(hd, i, v_tile)

    def body(j, carry):
        j_prev = jnp.where(j == 0, qi, j - 1)
        k_tiles = [keys(hd, j) for hd in range(heads)]
        v_tiles = [values(hd, j_prev) for hd in range(heads)]
        for hd, i in chains:
            rows = slice(i * sub, (i + 1) * sub)
            acc_ref[hd, rows, :] = accumulate(hd, i, v_tiles[hd])
            s = scores(hd, i, k_tiles[hd])
            m = m_ref[hd, rows, :]
            m_new = jnp.maximum(m, jnp.max(s, axis=1, keepdims=True))
            p_ref[hd, rows, :] = jnp.exp2(s - m_new).astype(BF16)
            alpha_ref[hd, rows, :] = jnp.exp2(m - m_new)
            m_ref[hd, rows, :] = m_new
        return carry

    lax.fori_loop(0, qi, body, 0)
    j_last = jnp.where(qi == 0, qi, qi - 1)
    for n, (hd, i) in enumerate(chains):
        acc = accumulate(hd, i, values(hd, j_last))
        o_ref[i * sub:(i + 1) * sub, head_cols(hd)] = (
            acc[:, :HEAD_DIM] / acc[:, HEAD_DIM:HEAD_DIM + 1]).astype(o_ref.dtype)
        side_work(n, len(chains))


def _stage_plan(cols_used, col_start, out_shape, steps):
    rows_out, cols_out = out_shape
    for col_blocks in (1, 2, 4):
        row_blocks = steps // col_blocks
        rb = rows_out // row_blocks
        if steps % col_blocks or rows_out % row_blocks or rb % BF16_ROWS:
            continue
        cb_in = cols_used if col_blocks == 1 else cols_out // col_blocks
        if col_blocks > 1 and (cols_used != cols_out or cb_in % LANES):
            continue
        if col_start % cb_in:
            continue
        return rb, col_blocks, cb_in
    raise ValueError(f"no staging plan for {cols_used} columns at {col_start} -> {out_shape} in {steps} steps")


def _stage_pieces(step, src_ref, dst_ref, *, rows, col_blocks):
    rb, cb_in = src_ref.shape
    cb_out = dst_ref.shape[1]
    row0 = (step // col_blocks) * rb

    def piece(r0, nr, c0, nc):
        def run():
            x = src_ref[r0:r0 + nr, c0:c0 + nc]
            row = row0 + r0 + lax.broadcasted_iota(jnp.int32, x.shape, 0)
            dst_ref[r0:r0 + nr, c0:c0 + nc] = jnp.where(row < rows, x, 0.0).astype(dst_ref.dtype)
            if c0 + nc == cb_in and cb_out > cb_in:
                dst_ref[r0:r0 + nr, cb_in:] = jnp.zeros((nr, cb_out - cb_in), dst_ref.dtype)
        return run

    if (rb // 2) % BF16_ROWS == 0:
        return [piece(0, rb // 2, 0, cb_in), piece(rb // 2, rb // 2, 0, cb_in)]
    assert (cb_in // 2) % LANES == 0
    return [piece(0, rb, 0, cb_in // 2), piece(0, rb, cb_in // 2, cb_in // 2)]


def _moba(z, *, batch, seq, stage=(), tile=1024, n_sub=4, heads=2):
    t = z.shape[0]
    n_blocks = seq // MOBA_BLOCK
    tiles = seq // tile
    groups = MOBA_HEADS // heads
    width = heads * HEAD_DIM
    steps = batch * groups * tiles

    def step_of(b, g, i):
        return (b * groups + g) * tiles + i

    stage_in, stage_out, stage_shapes, stage_args = [], [], [], []
    for w, out_shape, col_start in stage:
        cols_used = min(w.shape[1] - col_start, out_shape[1])
        rb, col_blocks, cb_in = _stage_plan(cols_used, col_start, out_shape, steps)
        cb_out = out_shape[1] // col_blocks
        last = (w.shape[0] - 1) // rb
        stage_in.append(pl.BlockSpec(
            (rb, cb_in), lambda b, g, i, cbs=col_blocks, last=last, c0=col_start // cb_in:
            (jnp.minimum(step_of(b, g, i) // cbs, last), c0 + step_of(b, g, i) % cbs)))
        stage_out.append(pl.BlockSpec(
            (rb, cb_out), lambda b, g, i, cbs=col_blocks: (step_of(b, g, i) // cbs, step_of(b, g, i) % cbs)))
        stage_shapes.append(jax.ShapeDtypeStruct(out_shape, BF16))
        stage_args.append(dict(rows=w.shape[0], col_blocks=col_blocks))

    kern = functools.partial(_moba_kernel, tile=tile, n_blocks=n_blocks, n_sub=n_sub, heads=heads,
                             stage_args=tuple(stage_args), groups=groups, tiles=tiles)
    outs = pl.pallas_call(
        kern,
        grid=(batch, groups, tiles),
        in_specs=[pl.BlockSpec((tile, width), lambda b, g, i: (b * tiles + i, g)),
                  pl.BlockSpec((seq, width), lambda b, g, i: (b, groups + g)),
                  pl.BlockSpec((seq, width), lambda b, g, i: (b, 2 * groups + g))] + stage_in,
        out_specs=[pl.BlockSpec((tile, width), lambda b, g, i: (b * tiles + i, g))] + stage_out,
        out_shape=[jax.ShapeDtypeStruct((t, MOBA_WIDTH), BF16)] + stage_shapes,
        scratch_shapes=[pltpu.VMEM((seq, LANES), BF16),
                        pltpu.VMEM((tile, LANES), BF16),
                        pltpu.VMEM((heads, LANES, HEAD_DIM), F32),
                        pltpu.VMEM((heads, tile, tile), BF16),
                        pltpu.VMEM((heads, tile, 1), F32),
                        pltpu.VMEM((heads, tile, 1), F32),
                        pltpu.VMEM((heads, tile, 2 * HEAD_DIM), F32)],
        compiler_params=_cparams(("arbitrary", "arbitrary", "arbitrary")),
        name="moba_attention",
    )(z, z, z, *[item[0] for item in stage])
    return outs[0] if not stage else tuple(outs)


def _sgu_kernel(u_ref, v_ref, g_ref, b_ref, w_ref, bs_ref, o_ref, *, rows):
    v = v_ref[...].astype(F32)
    mu = jnp.mean(v, axis=-1, keepdims=True)
    var = jnp.mean(jnp.square(v - mu), axis=-1, keepdims=True)
    vln = ((v - mu) * lax.rsqrt(var + LN_EPS) * g_ref[...] + b_ref[...]).astype(BF16)
    t_idx = lax.broadcasted_iota(jnp.int32, (SGU_CHUNK, SGU_CHUNK), 0)
    s_idx = lax.broadcasted_iota(jnp.int32, (SGU_CHUNK, SGU_CHUNK), 1)
    causal = s_idx <= t_idx
    for g in range(SGU_GROUPS):
        w = jnp.where(causal, w_ref[g], 0.0).astype(BF16)
        bias = bs_ref[:, g:g + 1]
        cols = slice(g * LANES, (g + 1) * LANES)
        for ch in range(rows // SGU_CHUNK):
            rws = slice(ch * SGU_CHUNK, (ch + 1) * SGU_CHUNK)
            mixed = jnp.dot(w, vln[rws, cols], preferred_element_type=F32) + bias
            o_ref[rws, cols] = (u_ref[rws, cols].astype(F32) * mixed).astype(o_ref.dtype)


def _sgu(z, ln_g, ln_b, w_s, b_s_t, *, width, u_col, rows=512):
    t = z.shape[0]
    u_blk = u_col // width
    kern = functools.partial(_sgu_kernel, rows=rows)
    return pl.pallas_call(
        kern,
        grid=(t // rows,),
        in_specs=[pl.BlockSpec((rows, width), lambda i: (i, u_blk)),
                  pl.BlockSpec((rows, width), lambda i: (i, u_blk + 1)),
                  pl.BlockSpec((1, width), lambda i: (0, 0)),
                  pl.BlockSpec((1, width), lambda i: (0, 0)),
                  pl.BlockSpec((SGU_GROUPS, SGU_CHUNK, SGU_CHUNK), lambda i: (0, 0, 0)),
                  pl.BlockSpec((SGU_CHUNK, SGU_GROUPS), lambda i: (0, 0))],
        out_specs=pl.BlockSpec((rows, width), lambda i: (i, 0)),
        out_shape=jax.ShapeDtypeStruct((t, width), BF16),
        compiler_params=_cparams(("arbitrary",)),
        name="spatial_gating",
    )(z, z, ln_g.reshape(1, width), ln_b.reshape(1, width), w_s, b_s_t)


def _merge_kernel(oa_ref, ob_ref, pa_ref, pb_ref, ga_ref, gb_ref, o_ref):
    a = jnp.dot(oa_ref[...], pa_ref[...], preferred_element_type=F32)
    b = jnp.dot(ob_ref[...], pb_ref[...], preferred_element_type=F32)
    o_ref[...] = (ga_ref[...].astype(F32) * a + gb_ref[...].astype(F32) * b).astype(o_ref.dtype)


def _merge(o_a, o_b, p_a, p_b, z, *, gate_col, bm=1024, bn=1024):
    t = o_a.shape[0]
    d = p_a.shape[1]
    ga_blk = gate_col // bn
    gb_blk = (gate_col + d) // bn
    return pl.pallas_call(
        _merge_kernel,
        grid=(t // bm, d // bn),
        in_specs=[pl.BlockSpec((bm, o_a.shape[1]), lambda m, n: (m, 0)),
                  pl.BlockSpec((bm, o_b.shape[1]), lambda m, n: (m, 0)),
                  pl.BlockSpec((p_a.shape[0], bn), lambda m, n: (0, n)),
                  pl.BlockSpec((p_b.shape[0], bn), lambda m, n: (0, n)),
                  pl.BlockSpec((bm, bn), lambda m, n: (m, ga_blk + n)),
                  pl.BlockSpec((bm, bn), lambda m, n: (m, gb_blk + n))],
        out_specs=pl.BlockSpec((bm, bn), lambda m, n: (m, n)),
        out_shape=jax.ShapeDtypeStruct((t, d), BF16),
        compiler_params=_cparams(("arbitrary", "arbitrary")),
        name="branch_merge",
    )(o_a, o_b, p_a, p_b, z, z)


def _outproj_kernel(a_ref, w_ref, x_ref, o_ref):
    o_ref[...] = x_ref[...] + jnp.dot(a_ref[...], w_ref[...], preferred_element_type=F32)


def _outproj(a, w, x, *, bm=1024, bn=512):
    t, k = a.shape
    d = w.shape[1]
    return pl.pallas_call(
        _outproj_kernel,
        grid=(t // bm, d // bn),
        in_specs=[pl.BlockSpec((bm, k), lambda m, n: (m, 0)),
                  pl.BlockSpec((k, bn), lambda m, n: (0, n)),
                  pl.BlockSpec((bm, bn), lambda m, n: (m, n))],
        out_specs=pl.BlockSpec((bm, bn), lambda m, n: (m, n)),
        out_shape=jax.ShapeDtypeStruct((t, d), F32),
        compiler_params=_cparams(("arbitrary", "arbitrary")),
        name="out_proj",
    )(a, w, x)


def _memkv_kernel(mem_ref, g_ref, w_ref, o_ref):
    mem_n = _rmsnorm_rows(mem_ref[...], g_ref[...]).astype(BF16)
    o_ref[...] = jnp.dot(mem_n, w_ref[...], preferred_element_type=F32).astype(o_ref.dtype)


def _memkv(mem, g, w, *, n_mem):
    t, d = mem.shape
    cols = w.shape[1]
    return pl.pallas_call(
        _memkv_kernel,
        grid=(t // n_mem,),
        in_specs=[pl.BlockSpec((n_mem, d), lambda i: (i, 0)),
                  pl.BlockSpec((1, d), lambda i: (0, 0)),
                  pl.BlockSpec((d, cols), lambda i: (0, 0))],
        out_specs=pl.BlockSpec((n_mem, cols), lambda i: (i, 0)),
        out_shape=jax.ShapeDtypeStruct((t, cols), BF16),
        compiler_params=_cparams(("arbitrary",)),
        name="mem_kv",
    )(mem, g.reshape(1, d), w)


def _xattn_kernel(x_ref, gx_ref, wq_ref, kv_ref, wo_ref, gf_ref, x_out_ref, h_out_ref, *, chain_rows):
    for c in range(x_ref.shape[0] // chain_rows):
        rows = slice(c * chain_rows, (c + 1) * chain_rows)
        x = x_ref[rows, :]
        h = _rmsnorm_rows(x, gx_ref[...]).astype(BF16)
        q = jnp.dot(h, wq_ref[...], preferred_element_type=F32) * (HEAD_DIM ** -0.5)
        q = q.astype(BF16)
        outs = []
        for hh in range(XATTN_HEADS):
            cols = slice(hh * HEAD_DIM, (hh + 1) * HEAD_DIM)
            k = kv_ref[:, hh * HEAD_DIM:(hh + 1) * HEAD_DIM]
            v = kv_ref[:, XATTN_WIDTH + hh * HEAD_DIM:XATTN_WIDTH + (hh + 1) * HEAD_DIM]
            s = lax.dot_general(q[:, cols], k, (((1,), (1,)), ((), ())), preferred_element_type=F32)
            s = s - jnp.max(s, axis=-1, keepdims=True)
            p = jnp.exp(s)
            p = p / jnp.sum(p, axis=-1, keepdims=True)
            outs.append(jnp.dot(p.astype(BF16), v, preferred_element_type=F32).astype(BF16))
        o = jnp.concatenate(outs, axis=1)
        x2 = x + jnp.dot(o, wo_ref[...], preferred_element_type=F32)
        x_out_ref[rows, :] = x2
        h_out_ref[rows, :] = _rmsnorm_rows(x2, gf_ref[...]).astype(h_out_ref.dtype)


def _xattn(x, g_x, w_q, kv, w_o, g_ffn, *, seq, n_mem, bm=512, chain_rows=256):
    t, d = x.shape
    per_batch = seq // bm
    resident = pl.Buffered(1)
    return pl.pallas_call(
        functools.partial(_xattn_kernel, chain_rows=chain_rows),
        grid=(t // bm,),
        in_specs=[pl.BlockSpec((bm, d), lambda i: (i, 0)),
                  pl.BlockSpec((1, d), lambda i: (0, 0)),
                  pl.BlockSpec(w_q.shape, lambda i: (0, 0), pipeline_mode=resident),
                  pl.BlockSpec((n_mem, kv.shape[1]), lambda i: (i // per_batch, 0)),
                  pl.BlockSpec(w_o.shape, lambda i: (0, 0), pipeline_mode=resident),
                  pl.BlockSpec((1, d), lambda i: (0, 0))],
        out_specs=[pl.BlockSpec((bm, d), lambda i: (i, 0)),
                   pl.BlockSpec((bm, d), lambda i: (i, 0))],
        out_shape=[jax.ShapeDtypeStruct((t, d), F32),
                   jax.ShapeDtypeStruct((t, d), BF16)],
        compiler_params=_cparams(("arbitrary",)),
        name="mem_cross_attention",
    )(x, g_x.reshape(1, d), w_q, kv, w_o, g_ffn.reshape(1, d))


def _ffn_up_kernel(h_ref, wg_ref, wu_ref, o_ref):
    h = h_ref[...]
    g = jnp.dot(h, wg_ref[...], preferred_element_type=F32)
    u = jnp.dot(h, wu_ref[...], preferred_element_type=F32)
    o_ref[...] = (g * _sigmoid(g) * u).astype(o_ref.dtype)


def _ffn_up(h, w_g, w_u, *, bm=1024, bn=512):
    t, d = h.shape
    f = w_g.shape[1]
    return pl.pallas_call(
        _ffn_up_kernel,
        grid=(t // bm, f // bn),
        in_specs=[pl.BlockSpec((bm, d), lambda m, n: (m, 0)),
                  pl.BlockSpec((d, bn), lambda m, n: (0, n)),
                  pl.BlockSpec((d, bn), lambda m, n: (0, n))],
        out_specs=pl.BlockSpec((bm, bn), lambda m, n: (m, n)),
        out_shape=jax.ShapeDtypeStruct((t, f), BF16),
        compiler_params=_cparams(("arbitrary", "arbitrary")),
        name="ffn_up",
    )(h, w_g, w_u)


def _ffn_down_kernel(a_ref, w_ref, x_ref, g_ref, o_ref, *, final_norm):
    k = pl.program_id(1)
    bm, d = o_ref.shape

    @pl.when(k == 0)
    def _residual():
        o_ref[...] = x_ref[...]

    a = a_ref[...]
    for c in range(d // DOWN_COLS):
        cols = slice(c * DOWN_COLS, (c + 1) * DOWN_COLS)
        o_ref[:, cols] += jnp.dot(a, w_ref[:, cols], preferred_element_type=F32)

    if final_norm:
        @pl.when(k == pl.num_programs(1) - 1)
        def _final_norm():
            for r in range(bm // NORM_ROWS):
                rows = slice(r * NORM_ROWS, (r + 1) * NORM_ROWS)
                o_ref[rows, :] = _rmsnorm_rows(o_ref[rows, :], g_ref[...])


def _ffn_down(a, w, x, g, *, final_norm, bm=512, bk=1024):
    t, f = a.shape
    d = w.shape[1]
    return pl.pallas_call(
        functools.partial(_ffn_down_kernel, final_norm=final_norm),
        grid=(t // bm, f // bk),
        in_specs=[pl.BlockSpec((bm, bk), lambda m, k: (m, k)),
                  pl.BlockSpec((bk, d), lambda m, k: (k, 0)),
                  pl.BlockSpec((bm, d), lambda m, k: (m, 0)),
                  pl.BlockSpec((1, d), lambda m, k: (0, 0))],
        out_specs=pl.BlockSpec((bm, d), lambda m, k: (m, 0)),
        out_shape=jax.ShapeDtypeStruct((t, d), F32),
        compiler_params=_cparams(("arbitrary", "arbitrary")),
        name="ffn_down",
    )(a, w, x, g.reshape(1, d))


@jax.jit
def kernel(x, mem, norm_mix_g, w_in, sgu_ln_g, sgu_ln_b, w_sgu, b_sgu, w_branch_a, w_branch_b, w_out, norm_xattn_g, norm_mem_g, w_xq, w_xkv, w_xo, norm_ffn_g, w_ff_gate, w_ff_up, w_ff_down, norm_final_g):
    batch, seq, d = x.shape
    n_mem = mem.shape[1]
    depth = w_in.shape[0]
    sgu_width = sgu_ln_g.shape[1]
    d_ff = w_ff_gate.shape[2]
    ff_pad = (-d_ff) % FF_PAD

    xt = x.reshape(batch * seq, d)
    memt = mem.reshape(batch * n_mem, d)
    cos, sin = _rope_table(seq)

    for l in range(depth):
        h = _rmsnorm(xt, norm_mix_g[l], BF16)
        qkv_cols = 3 * MOBA_WIDTH
        blocks = MOBA_WIDTH // INPROJ_COLS
        z_qkv = _inproj(h, w_in[l][:, :qkv_cols].astype(BF16), (cos, sin), seq=seq, name="in_proj_qkv",
                        segments=(("rope_q", blocks), ("rope", blocks), ("plain", blocks)), bn=INPROJ_COLS)
        f_pad = d_ff + ff_pad
        rest_cols = w_in.shape[2] - qkv_cols
        o_a, w_rest, w_g, w_u, w_d, w_pa, w_pb, w_o, w_q, w_kv, w_xout = _moba(
            z_qkv, batch=batch, seq=seq, stage=(
                (w_in[l], (d, rest_cols), qkv_cols),
                (w_ff_gate[l], (d, f_pad), 0), (w_ff_up[l], (d, f_pad), 0), (w_ff_down[l], (f_pad, d), 0),
                (w_branch_a[l], w_branch_a[l].shape, 0), (w_branch_b[l], w_branch_b[l].shape, 0),
                (w_out[l], w_out[l].shape, 0), (w_xq[l], w_xq[l].shape, 0), (w_xkv[l], w_xkv[l].shape, 0),
                (w_xo[l], w_xo[l].shape, 0)))
        z = _inproj(h, w_rest, (), seq=seq, name="in_proj_rest", bn=INPROJ_COLS,
                    segments=(("gelu", 2 * sgu_width // INPROJ_COLS), ("sigmoid", 2 * d // INPROJ_COLS)))
        o_b = _sgu(z, sgu_ln_g[l], sgu_ln_b[l], w_sgu[l], jnp.transpose(b_sgu[l]), width=sgu_width, u_col=0)
        merged = _merge(o_a, o_b, w_pa, w_pb, z, gate_col=2 * sgu_width)
        xt = _outproj(merged, w_o, xt)

        kv = _memkv(memt, norm_mem_g[l], w_kv, n_mem=n_mem)
        xt, h = _xattn(xt, norm_xattn_g[l], w_q, kv, w_xout, norm_ffn_g[l], seq=seq, n_mem=n_mem)

        a = _ffn_up(h, w_g, w_u)
        xt = _ffn_down(a, w_d, xt, norm_final_g, final_norm=(l == depth - 1))
    return xt.reshape(batch, seq, d)
```

```python
import functools
import math

import jax
import jax.numpy as jnp
from jax import lax
from jax.experimental import pallas as pl
from jax.experimental.pallas import tpu as pltpu

F32 = jnp.float32
BF16 = jnp.bfloat16

HEAD_DIM = 128
MOBA_HEADS = 16
MOBA_WIDTH = MOBA_HEADS * HEAD_DIM
MOBA_BLOCK = 256
MOBA_TOPK = 3
SGU_CHUNK = 128
SGU_GROUPS = 16
XATTN_HEADS = 4
XATTN_WIDTH = XATTN_HEADS * HEAD_DIM
RMS_EPS = 1e-6
LN_EPS = 1e-5
ROPE_THETA = 10000.0
NEG_INF = -1e30
REMOVED = -3e38
Q_SCALE = HEAD_DIM ** -0.5 * math.log2(math.e)

LANES = 128
BF16_ROWS = 16
FF_PAD = 512
INPROJ_ROWS = 1024
NORMED_ROWS = 512
INPROJ_COLS = 1024
PROJ_COLS = 256
DOWN_COLS = 1024
NORM_ROWS = 64
VMEM_LIMIT = 60 * 1024 * 1024


def _cparams(sem):
    return pltpu.CompilerParams(dimension_semantics=sem, vmem_limit_bytes=VMEM_LIMIT)


def _rmsnorm_rows(x, g):
    ms = jnp.mean(x * x, axis=-1, keepdims=True)
    return x * lax.rsqrt(ms + RMS_EPS) * g


def _rmsnorm_kernel(x_ref, g_ref, o_ref):
    o_ref[...] = _rmsnorm_rows(x_ref[...], g_ref[...]).astype(o_ref.dtype)


def _rmsnorm(x, g, out_dtype, bm=256):
    t, d = x.shape
    return pl.pallas_call(
        _rmsnorm_kernel,
        grid=(t // bm,),
        in_specs=[pl.BlockSpec((bm, d), lambda i: (i, 0)),
                  pl.BlockSpec((1, d), lambda i: (0, 0))],
        out_specs=pl.BlockSpec((bm, d), lambda i: (i, 0)),
        out_shape=jax.ShapeDtypeStruct((t, d), out_dtype),
        compiler_params=_cparams(("arbitrary",)),
        name="rmsnorm",
    )(x, g.reshape(1, d))


def _rope_kernel(freq_ref, cos_ref, sin_ref, *, rows):
    i = pl.program_id(0)
    pos = (i * rows + lax.broadcasted_iota(jnp.int32, (rows, LANES), 0)).astype(F32)
    lane = lax.broadcasted_iota(jnp.int32, (rows, LANES), 1)
    ang = pos * freq_ref[...]
    cos_ref[...] = jnp.cos(ang)
    s = jnp.sin(ang)
    sin_ref[...] = jnp.where(lane < HEAD_DIM // 2, -s, s)


def _rope_table(seq, rows=1024):
    half = HEAD_DIM // 2
    inv_freq = jnp.power(ROPE_THETA, -(jnp.arange(half, dtype=F32) * 2.0 / HEAD_DIM))
    freq = jnp.concatenate([inv_freq, inv_freq]).reshape(1, HEAD_DIM)
    return pl.pallas_call(
        functools.partial(_rope_kernel, rows=rows),
        grid=(seq // rows,),
        in_specs=[pl.BlockSpec((1, HEAD_DIM), lambda i: (0, 0))],
        out_specs=[pl.BlockSpec((rows, HEAD_DIM), lambda i: (i, 0))] * 2,
        out_shape=[jax.ShapeDtypeStruct((seq, HEAD_DIM), F32)] * 2,
        compiler_params=_cparams(("arbitrary",)),
        name="rope_table",
    )(freq)


def _gelu_tanh(x):
    c = math.sqrt(2.0 / math.pi)
    half = 0.5 * x
    return half + half * jnp.tanh(x * (c + (c * 0.044715) * (x * x)))


def _sigmoid(x):
    return 1.0 / (1.0 + jnp.exp(-x))


def _inproj_kernel(*refs, bn, segments, normed):
    n = pl.program_id(1)
    if normed:
        x_ref, g_ref, w_ref, *rope_refs, z_ref, h_ref = refs

        @pl.when(n == 0)
        def _normalize():
            for r in range(x_ref.shape[0] // NORM_ROWS):
                rows = slice(r * NORM_ROWS, (r + 1) * NORM_ROWS)
                h_ref[rows, :] = _rmsnorm_rows(x_ref[rows, :], g_ref[...]).astype(h_ref.dtype)
    else:
        h_ref, w_ref, *rope_refs, z_ref = refs

    def project(epilogue):
        h = h_ref[...]
        for c in range(bn // PROJ_COLS):
            cols = slice(c * PROJ_COLS, (c + 1) * PROJ_COLS)
            blk = jnp.dot(h, w_ref[:, cols], preferred_element_type=F32)
            z_ref[:, cols] = epilogue(blk).astype(z_ref.dtype)

    def rotary(scale):
        def rotate(blk):
            cos, sin = rope_refs[0][...], rope_refs[1][...]
            heads = []
            for hh in range(PROJ_COLS // HEAD_DIM):
                x = blk[:, hh * HEAD_DIM:(hh + 1) * HEAD_DIM]
                heads.append(x * cos + pltpu.roll(x, HEAD_DIM // 2, axis=1) * sin)
            out = jnp.concatenate(heads, axis=1)
            return out if scale is None else out * scale
        return rotate

    epilogues = {
        "rope_q": rotary(Q_SCALE),
        "rope": rotary(None),
        "plain": lambda blk: blk,
        "gelu": _gelu_tanh,
        "sigmoid": lambda blk: 0.5 + 0.5 * jnp.tanh(0.5 * blk),
    }
    start = 0
    for kind, count in segments:
        pl.when((n >= start) & (n < start + count))(functools.partial(project, epilogues[kind]))
        start += count


def _inproj(h, w, rope, *, seq, segments, name, bm, bn, norm_gain=None):
    t, d = h.shape
    cols = w.shape[1]
    assert sum(count for _, count in segments) * bn == cols
    seq_blocks = seq // bm
    normed = norm_gain is not None
    row_spec = pl.BlockSpec((bm, d), lambda m, n: (m, 0))
    rope_spec = pl.BlockSpec((bm, HEAD_DIM), lambda m, n: (m % seq_blocks, 0))
    z_spec = pl.BlockSpec((bm, bn), lambda m, n: (m, n))
    z_shape = jax.ShapeDtypeStruct((t, cols), BF16)
    gain = [norm_gain.reshape(1, d)] if normed else []
    return pl.pallas_call(
        functools.partial(_inproj_kernel, bn=bn, segments=segments, normed=normed),
        grid=(t // bm, cols // bn),
        in_specs=([row_spec] + [pl.BlockSpec((1, d), lambda m, n: (0, 0))] * normed
                  + [pl.BlockSpec((d, bn), lambda m, n: (0, n))] + [rope_spec] * len(rope)),
        out_specs=[z_spec, row_spec] if normed else z_spec,
        out_shape=[z_shape, jax.ShapeDtypeStruct((t, d), BF16)] if normed else z_shape,
        compiler_params=_cparams(("arbitrary", "arbitrary")),
        name=name,
    )(h, *gain, w, *rope)


def _moba_kernel(q_ref, k_ref, v_ref, *refs, tile, n_blocks, n_sub, heads, stage_args, groups, tiles):
    n_stage = len(stage_args)
    stage_src, o_ref, stage_dst = refs[:n_stage], refs[n_stage], refs[n_stage + 1:2 * n_stage + 1]
    onehot_ref, ones_ref, kmean_ref, p_ref, m_ref, alpha_ref, acc_ref = refs[2 * n_stage + 1:]
    step = (pl.program_id(0) * groups + pl.program_id(1)) * tiles + pl.program_id(2)
    pieces = [p for src, dst, args in zip(stage_src, stage_dst, stage_args)
              for p in _stage_pieces(step, src, dst, **args)]

    def stage_weights(slot, n_slots):
        for p in pieces[slot::n_slots]:
            p()

    _moba_body(q_ref, k_ref, v_ref, o_ref, onehot_ref, ones_ref, kmean_ref, p_ref, m_ref, alpha_ref, acc_ref,
               tile=tile, n_blocks=n_blocks, n_sub=n_sub, heads=heads, side_work=stage_weights)


def _moba_body(q_ref, k_ref, v_ref, o_ref, onehot_ref, ones_ref, kmean_ref, p_ref, m_ref, alpha_ref, acc_ref,
               *, tile, n_blocks, n_sub, heads, side_work):
    qi = pl.program_id(2)
    blocks_per_tile = tile // MOBA_BLOCK
    contract_last = (((1,), (1,)), ((), ()))
    sub = tile // n_sub

    chains = [(hd, i) for hd in range(heads) for i in range(n_sub)]

    def head_cols(hd):
        return slice(hd * HEAD_DIM, (hd + 1) * HEAD_DIM)

    @pl.when(qi == 0)
    def _prepare_keys():
        kmean_ref[...] = jnp.zeros_like(kmean_ref)
        lane = lax.broadcasted_iota(jnp.int32, (MOBA_BLOCK, LANES), 1)
        ones_ref[...] = (lax.broadcasted_iota(jnp.int32, ones_ref.shape, 1) == 0).astype(BF16)
        for j in range(n_blocks):
            rows = pl.ds(j * MOBA_BLOCK, MOBA_BLOCK)
            onehot_ref[rows, :] = (lane == j).astype(BF16)
            for hd in range(heads):
                kb = k_ref[rows, head_cols(hd)]
                kmean_ref[hd, j:j + 1, :] = jnp.mean(kb.astype(F32), axis=0, keepdims=True)

    blk = lax.broadcasted_iota(jnp.int32, (n_blocks, tile), 0)
    col = lax.broadcasted_iota(jnp.int32, (n_blocks, tile), 1)
    own_block = qi * blocks_per_tile + col // MOBA_BLOCK
    q_ext = []
    for hd in range(heads):
        q = q_ref[:, head_cols(hd)]
        gate = lax.dot_general(kmean_ref[hd].astype(BF16), q, contract_last,
                               preferred_element_type=F32)[:n_blocks, :]
        gate = jnp.where(blk < own_block, gate, NEG_INF)
        allowed = blk == own_block
        for _ in range(MOBA_TOPK):
            best = jnp.max(gate, axis=0, keepdims=True)
            first = jnp.min(jnp.where(gate == best, blk, n_blocks), axis=0, keepdims=True)
            hit = blk == first
            allowed = allowed | (hit & (best > 0.5 * NEG_INF))
            gate = jnp.where(hit, REMOVED, gate)
        bias_t = jnp.where(allowed, 0.0, NEG_INF)
        bias_t = jnp.concatenate([bias_t, jnp.zeros((LANES - n_blocks, tile), F32)], axis=0)
        bias = jnp.transpose(bias_t).astype(BF16)
        q_ext.append(jnp.concatenate([q, bias], axis=1))

    def keys(hd, j, rows=tile):
        at = pl.ds(pl.multiple_of(j * tile, tile), rows)
        return jnp.concatenate([k_ref[at, head_cols(hd)], onehot_ref[at, :]], axis=1)

    def values(hd, j):
        at = pl.ds(pl.multiple_of(j * tile, tile), tile)
        return jnp.concatenate([v_ref[at, head_cols(hd)], ones_ref[...]], axis=1)

    def scores(hd, i, k_tile):
        return lax.dot_general(q_ext[hd][i * sub:(i + 1) * sub], k_tile, contract_last,
                               preferred_element_type=F32)

    def weighted_values(hd, i, v_tile):
        return jnp.dot(p_ref[hd, i * sub:(i + 1) * sub, :], v_tile, preferred_element_type=F32)

    for hd, i in chains:
        rows = slice(i * sub, (i + 1) * sub)
        seen = (i + 1) * sub
        r = i * sub + lax.broadcasted_iota(jnp.int32, (sub, seen), 0)
        c = lax.broadcasted_iota(jnp.int32, (sub, seen), 1)
        s = jnp.where(c <= r, scores(hd, i, keys(hd, qi, seen)), NEG_INF)
        m0 = jnp.max(s, axis=1, keepdims=True)
        p_ref[hd, rows, :seen] = jnp.exp2(s - m0).astype(BF16)
        if seen < tile:
            p_ref[hd, rows, seen:] = jnp.zeros((sub, tile - seen), BF16)
        m_ref[hd, rows, :] = m0
        alpha_ref[hd, rows, :] = jnp.ones((sub, 1), F32)
        acc_ref[hd, rows, :] = jnp.zeros((sub, 2 * HEAD_DIM), F32)

    def accumulate(hd, i, v_tile):
        rows = slice(i * sub, (i + 1) * sub)
        return alpha_ref[hd, rows, :] * acc_ref[hd, rows, :] + weighted_values(hd, i, v_tile)

    def body(j, carry):
        j_prev = jnp.where(j == 0, qi, j - 1)
        k_tiles = [keys(hd, j) for hd in range(heads)]
        v_tiles = [values(hd, j_prev) for hd in range(heads)]
        for hd, i in chains:
            rows = slice(i * sub, (i + 1) * sub)
            acc_ref[hd, rows, :] = accumulate(hd, i, v_tiles[hd])
            s = scores(hd, i, k_tiles[hd])
            m = m_ref[hd, rows, :]
            m_new = jnp.maximum(m, jnp.max(s, axis=1, keepdims=True))
            p_ref[hd, rows, :] = jnp.exp2(s - m_new).astype(BF16)
            alpha_ref[hd, rows, :] = jnp.exp2(m - m_new)
            m_ref[hd, rows, :] = m_new
        return carry

    lax.fori_loop(0, qi, body, 0)
    j_last = jnp.where(qi == 0, qi, qi - 1)
    for n, (hd, i) in enumerate(chains):
        acc = accumulate(hd, i, values(hd, j_last))
        o_ref[i * sub:(i + 1) * sub, head_cols(hd)] = (
            acc[:, :HEAD_DIM] / acc[:, HEAD_DIM:HEAD_DIM + 1]).astype(o_ref.dtype)
        side_work(n, len(chains))


def _stage_plan(cols_used, col_start, out_shape, steps):
    rows_out, cols_out = out_shape
    for col_blocks in (1, 2, 4):
        row_blocks = steps // col_blocks
        rb = rows_out // row_blocks
        if steps % col_blocks or rows_out % row_blocks or rb % BF16_ROWS:
            continue
        cb_in = cols_used if col_blocks == 1 else cols_out // col_blocks
        if col_blocks > 1 and (cols_used != cols_out or cb_in % LANES):
            continue
        if col_start % cb_in:
            continue
        return rb, col_blocks, cb_in
    raise ValueError(f"no staging plan for {cols_used} columns at {col_start} -> {out_shape} in {steps} steps")


def _stage_pieces(step, src_ref, dst_ref, *, rows, rows_out, col_blocks):
    rb, cb_in = src_ref.shape
    cb_out = dst_ref.shape[1]
    row0 = (step // col_blocks) * rb

    def piece(r0, nr, c0, nc):
        def run():
            x = src_ref[r0:r0 + nr, c0:c0 + nc]
            if rows_out > rows:
                row = row0 + r0 + lax.broadcasted_iota(jnp.int32, x.shape, 0)
                x = jnp.where(row < rows, x, 0.0)
            dst_ref[r0:r0 + nr, c0:c0 + nc] = x.astype(dst_ref.dtype)
            if c0 + nc == cb_in and cb_out > cb_in:
                dst_ref[r0:r0 + nr, cb_in:] = jnp.zeros((nr, cb_out - cb_in), dst_ref.dtype)
        return run

    if (rb // 2) % BF16_ROWS == 0:
        return [piece(0, rb // 2, 0, cb_in), piece(rb // 2, rb // 2, 0, cb_in)]
    assert (cb_in // 2) % LANES == 0
    return [piece(0, rb, 0, cb_in // 2), piece(0, rb, cb_in // 2, cb_in // 2)]


def _moba(z, *, batch, seq, stage=(), tile=1024, n_sub=4, heads=2):
    t = z.shape[0]
    n_blocks = seq // MOBA_BLOCK
    tiles = seq // tile
    groups = MOBA_HEADS // heads
    width = heads * HEAD_DIM
    steps = batch * groups * tiles

    def step_of(b, g, i):
        return (b * groups + g) * tiles + i

    stage_in, stage_out, stage_shapes, stage_args = [], [], [], []
    for w, out_shape, col_start in stage:
        cols_used = min(w.shape[1] - col_start, out_shape[1])
        rb, col_blocks, cb_in = _stage_plan(cols_used, col_start, out_shape, steps)
        cb_out = out_shape[1] // col_blocks
        last = (w.shape[0] - 1) // rb
        stage_in.append(pl.BlockSpec(
            (rb, cb_in), lambda b, g, i, cbs=col_blocks, last=last, c0=col_start // cb_in:
            (jnp.minimum(step_of(b, g, i) // cbs, last), c0 + step_of(b, g, i) % cbs)))
        stage_out.append(pl.BlockSpec(
            (rb, cb_out), lambda b, g, i, cbs=col_blocks: (step_of(b, g, i) // cbs, step_of(b, g, i) % cbs)))
        stage_shapes.append(jax.ShapeDtypeStruct(out_shape, BF16))
        stage_args.append(dict(rows=w.shape[0], rows_out=out_shape[0], col_blocks=col_blocks))

    kern = functools.partial(_moba_kernel, tile=tile, n_blocks=n_blocks, n_sub=n_sub, heads=heads,
                             stage_args=tuple(stage_args), groups=groups, tiles=tiles)
    outs = pl.pallas_call(
        kern,
        grid=(batch, groups, tiles),
        in_specs=[pl.BlockSpec((tile, width), lambda b, g, i: (b * tiles + i, g)),
                  pl.BlockSpec((seq, width), lambda b, g, i: (b, groups + g)),
                  pl.BlockSpec((seq, width), lambda b, g, i: (b, 2 * groups + g))] + stage_in,
        out_specs=[pl.BlockSpec((tile, width), lambda b, g, i: (b * tiles + i, g))] + stage_out,
        out_shape=[jax.ShapeDtypeStruct((t, MOBA_WIDTH), BF16)] + stage_shapes,
        scratch_shapes=[pltpu.VMEM((seq, LANES), BF16),
                        pltpu.VMEM((tile, LANES), BF16),
                        pltpu.VMEM((heads, LANES, HEAD_DIM), F32),
                        pltpu.VMEM((heads, tile, tile), BF16),
                        pltpu.VMEM((heads, tile, 1), F32),
                        pltpu.VMEM((heads, tile, 1), F32),
                        pltpu.VMEM((heads, tile, 2 * HEAD_DIM), F32)],
        compiler_params=_cparams(("arbitrary", "arbitrary", "arbitrary")),
        name="moba_attention",
    )(z, z, z, *[item[0] for item in stage])
    return outs[0] if not stage else tuple(outs)


def _sgu_kernel(u_ref, v_ref, g_ref, b_ref, w_ref, bs_ref, o_ref, *, rows):
    v = v_ref[...].astype(F32)
    mu = jnp.mean(v, axis=-1, keepdims=True)
    var = jnp.mean(jnp.square(v - mu), axis=-1, keepdims=True)
    vln = ((v - mu) * lax.rsqrt(var + LN_EPS) * g_ref[...] + b_ref[...]).astype(BF16)
    t_idx = lax.broadcasted_iota(jnp.int32, (SGU_CHUNK, SGU_CHUNK), 0)
    s_idx = lax.broadcasted_iota(jnp.int32, (SGU_CHUNK, SGU_CHUNK), 1)
    causal = s_idx <= t_idx
    for g in range(SGU_GROUPS):
        w = jnp.where(causal, w_ref[g], 0.0).astype(BF16)
        bias = bs_ref[:, g:g + 1]
        cols = slice(g * LANES, (g + 1) * LANES)
        for ch in range(rows // SGU_CHUNK):
            rws = slice(ch * SGU_CHUNK, (ch + 1) * SGU_CHUNK)
            mixed = jnp.dot(w, vln[rws, cols], preferred_element_type=F32) + bias
            o_ref[rws, cols] = (u_ref[rws, cols].astype(F32) * mixed).astype(o_ref.dtype)


def _sgu(z, ln_g, ln_b, w_s, b_s_t, *, width, u_col, rows=512):
    t = z.shape[0]
    u_blk = u_col // width
    kern = functools.partial(_sgu_kernel, rows=rows)
    return pl.pallas_call(
        kern,
        grid=(t // rows,),
        in_specs=[pl.BlockSpec((rows, width), lambda i: (i, u_blk)),
                  pl.BlockSpec((rows, width), lambda i: (i, u_blk + 1)),
                  pl.BlockSpec((1, width), lambda i: (0, 0)),
                  pl.BlockSpec((1, width), lambda i: (0, 0)),
                  pl.BlockSpec((SGU_GROUPS, SGU_CHUNK, SGU_CHUNK), lambda i: (0, 0, 0)),
                  pl.BlockSpec((SGU_CHUNK, SGU_GROUPS), lambda i: (0, 0))],
        out_specs=pl.BlockSpec((rows, width), lambda i: (i, 0)),
        out_shape=jax.ShapeDtypeStruct((t, width), BF16),
        compiler_params=_cparams(("arbitrary",)),
        name="spatial_gating",
    )(z, z, ln_g.reshape(1, width), ln_b.reshape(1, width), w_s, b_s_t)


def _merge_kernel(oa_ref, ob_ref, pa_ref, pb_ref, ga_ref, gb_ref, o_ref):
    oa, ob = oa_ref[...], ob_ref[...]
    for c in range(o_ref.shape[1] // PROJ_COLS):
        cols = slice(c * PROJ_COLS, (c + 1) * PROJ_COLS)
        a = jnp.dot(oa, pa_ref[:, cols], preferred_element_type=F32)
        b = jnp.dot(ob, pb_ref[:, cols], preferred_element_type=F32)
        o_ref[:, cols] = (ga_ref[:, cols].astype(F32) * a + gb_ref[:, cols].astype(F32) * b).astype(o_ref.dtype)


def _merge(o_a, o_b, p_a, p_b, z, *, gate_col, bm=1024, bn=1024):
    t = o_a.shape[0]
    d = p_a.shape[1]
    ga_blk = gate_col // bn
    gb_blk = (gate_col + d) // bn
    return pl.pallas_call(
        _merge_kernel,
        grid=(t // bm, d // bn),
        in_specs=[pl.BlockSpec((bm, o_a.shape[1]), lambda m, n: (m, 0)),
                  pl.BlockSpec((bm, o_b.shape[1]), lambda m, n: (m, 0)),
                  pl.BlockSpec((p_a.shape[0], bn), lambda m, n: (0, n)),
                  pl.BlockSpec((p_b.shape[0], bn), lambda m, n: (0, n)),
                  pl.BlockSpec((bm, bn), lambda m, n: (m, ga_blk + n)),
                  pl.BlockSpec((bm, bn), lambda m, n: (m, gb_blk + n))],
        out_specs=pl.BlockSpec((bm, bn), lambda m, n: (m, n)),
        out_shape=jax.ShapeDtypeStruct((t, d), BF16),
        compiler_params=_cparams(("arbitrary", "arbitrary")),
        name="branch_merge",
    )(o_a, o_b, p_a, p_b, z, z)


def _outproj_kernel(a_ref, w_ref, x_ref, o_ref):
    o_ref[...] = x_ref[...] + jnp.dot(a_ref[...], w_ref[...], preferred_element_type=F32)


def _outproj(a, w, x, *, bm=1024, bn=1024):
    t, k = a.shape
    d = w.shape[1]
    return pl.pallas_call(
        _outproj_kernel,
        grid=(t // bm, d // bn),
        in_specs=[pl.BlockSpec((bm, k), lambda m, n: (m, 0)),
                  pl.BlockSpec((k, bn), lambda m, n: (0, n)),
                  pl.BlockSpec((bm, bn), lambda m, n: (m, n))],
        out_specs=pl.BlockSpec((bm, bn), lambda m, n: (m, n)),
        out_shape=jax.ShapeDtypeStruct((t, d), F32),
        compiler_params=_cparams(("arbitrary", "arbitrary")),
        name="out_proj",
    )(a, w, x)


def _memkv_kernel(mem_ref, g_ref, w_ref, o_ref):
    mem_n = _rmsnorm_rows(mem_ref[...], g_ref[...]).astype(BF16)
    o_ref[...] = jnp.dot(mem_n, w_ref[...], preferred_element_type=F32).astype(o_ref.dtype)


def _memkv(mem, g, w, *, n_mem):
    t, d = mem.shape
    cols = w.shape[1]
    return pl.pallas_call(
        _memkv_kernel,
        grid=(t // n_mem,),
        in_specs=[pl.BlockSpec((n_mem, d), lambda i: (i, 0)),
                  pl.BlockSpec((1, d), lambda i: (0, 0)),
                  pl.BlockSpec((d, cols), lambda i: (0, 0))],
        out_specs=pl.BlockSpec((n_mem, cols), lambda i: (i, 0)),
        out_shape=jax.ShapeDtypeStruct((t, cols), BF16),
        compiler_params=_cparams(("arbitrary",)),
        name="mem_kv",
    )(mem, g.reshape(1, d), w)


def _xattn_kernel(x_ref, gx_ref, wq_ref, kv_ref, wo_ref, gf_ref, x_out_ref, h_out_ref, *, chain_rows):
    for c in range(x_ref.shape[0] // chain_rows):
        rows = slice(c * chain_rows, (c + 1) * chain_rows)
        x = x_ref[rows, :]
        h = _rmsnorm_rows(x, gx_ref[...]).astype(BF16)
        q = jnp.dot(h, wq_ref[...], preferred_element_type=F32) * (HEAD_DIM ** -0.5)
        q = q.astype(BF16)
        outs = []
        for hh in range(XATTN_HEADS):
            cols = slice(hh * HEAD_DIM, (hh + 1) * HEAD_DIM)
            k = kv_ref[:, hh * HEAD_DIM:(hh + 1) * HEAD_DIM]
            v = kv_ref[:, XATTN_WIDTH + hh * HEAD_DIM:XATTN_WIDTH + (hh + 1) * HEAD_DIM]
            s = lax.dot_general(q[:, cols], k, (((1,), (1,)), ((), ())), preferred_element_type=F32)
            s = s - jnp.max(s, axis=-1, keepdims=True)
            p = jnp.exp(s)
            p = p / jnp.sum(p, axis=-1, keepdims=True)
            outs.append(jnp.dot(p.astype(BF16), v, preferred_element_type=F32).astype(BF16))
        o = jnp.concatenate(outs, axis=1)
        x2 = x + jnp.dot(o, wo_ref[...], preferred_element_type=F32)
        x_out_ref[rows, :] = x2
        h_out_ref[rows, :] = _rmsnorm_rows(x2, gf_ref[...]).astype(h_out_ref.dtype)


def _xattn(x, g_x, w_q, kv, w_o, g_ffn, *, seq, n_mem, bm=512, chain_rows=256):
    t, d = x.shape
    per_batch = seq // bm
    resident = pl.Buffered(1)
    return pl.pallas_call(
        functools.partial(_xattn_kernel, chain_rows=chain_rows),
        grid=(t // bm,),
        in_specs=[pl.BlockSpec((bm, d), lambda i: (i, 0)),
                  pl.BlockSpec((1, d), lambda i: (0, 0)),
                  pl.BlockSpec(w_q.shape, lambda i: (0, 0), pipeline_mode=resident),
                  pl.BlockSpec((n_mem, kv.shape[1]), lambda i: (i // per_batch, 0)),
                  pl.BlockSpec(w_o.shape, lambda i: (0, 0), pipeline_mode=resident),
                  pl.BlockSpec((1, d), lambda i: (0, 0))],
        out_specs=[pl.BlockSpec((bm, d), lambda i: (i, 0)),
                   pl.BlockSpec((bm, d), lambda i: (i, 0))],
        out_shape=[jax.ShapeDtypeStruct((t, d), F32),
                   jax.ShapeDtypeStruct((t, d), BF16)],
        compiler_params=_cparams(("arbitrary",)),
        name="mem_cross_attention",
    )(x, g_x.reshape(1, d), w_q, kv, w_o, g_ffn.reshape(1, d))


def _ffn_up_kernel(h_ref, wg_ref, wu_ref, o_ref):
    h = h_ref[...]
    g = jnp.dot(h, wg_ref[...], preferred_element_type=F32)
    u = jnp.dot(h, wu_ref[...], preferred_element_type=F32)
    o_ref[...] = (g * _sigmoid(g) * u).astype(o_ref.dtype)


def _ffn_up(h, w_g, w_u, *, bm=1024, bn=512):
    t, d = h.shape
    f = w_g.shape[1]
    return pl.pallas_call(
        _ffn_up_kernel,
        grid=(t // bm, f // bn),
        in_specs=[pl.BlockSpec((bm, d), lambda m, n: (m, 0)),
                  pl.BlockSpec((d, bn), lambda m, n: (0, n)),
                  pl.BlockSpec((d, bn), lambda m, n: (0, n))],
        out_specs=pl.BlockSpec((bm, bn), lambda m, n: (m, n)),
        out_shape=jax.ShapeDtypeStruct((t, f), BF16),
        compiler_params=_cparams(("arbitrary", "arbitrary")),
        name="ffn_up",
    )(h, w_g, w_u)


def _ffn_down_kernel(a_ref, w_ref, x_ref, g_ref, o_ref, *, final_norm):
    k = pl.program_id(1)
    bm, d = o_ref.shape

    @pl.when(k == 0)
    def _residual():
        o_ref[...] = x_ref[...]

    a = a_ref[...]
    for c in range(d // DOWN_COLS):
        cols = slice(c * DOWN_COLS, (c + 1) * DOWN_COLS)
        o_ref[:, cols] += jnp.dot(a, w_ref[:, cols], preferred_element_type=F32)

    if final_norm:
        @pl.when(k == pl.num_programs(1) - 1)
        def _final_norm():
            for r in range(bm // NORM_ROWS):
                rows = slice(r * NORM_ROWS, (r + 1) * NORM_ROWS)
                o_ref[rows, :] = _rmsnorm_rows(o_ref[rows, :], g_ref[...])


def _ffn_down(a, w, x, g, *, final_norm, bm=512, bk=1024):
    t, f = a.shape
    d = w.shape[1]
    return pl.pallas_call(
        functools.partial(_ffn_down_kernel, final_norm=final_norm),
        grid=(t // bm, f // bk),
        in_specs=[pl.BlockSpec((bm, bk), lambda m, k: (m, k)),
                  pl.BlockSpec((bk, d), lambda m, k: (k, 0)),
                  pl.BlockSpec((bm, d), lambda m, k: (m, 0)),
                  pl.BlockSpec((1, d), lambda m, k: (0, 0))],
        out_specs=pl.BlockSpec((bm, d), lambda m, k: (m, 0)),
        out_shape=jax.ShapeDtypeStruct((t, d), F32),
        compiler_params=_cparams(("arbitrary", "arbitrary")),
        name="ffn_down",
    )(a, w, x, g.reshape(1, d))


@jax.jit
def kernel(x, mem, norm_mix_g, w_in, sgu_ln_g, sgu_ln_b, w_sgu, b_sgu, w_branch_a, w_branch_b, w_out, norm_xattn_g, norm_mem_g, w_xq, w_xkv, w_xo, norm_ffn_g, w_ff_gate, w_ff_up, w_ff_down, norm_final_g):
    batch, seq, d = x.shape
    n_mem = mem.shape[1]
    depth = w_in.shape[0]
    sgu_width = sgu_ln_g.shape[1]
    d_ff = w_ff_gate.shape[2]
    ff_pad = (-d_ff) % FF_PAD

    xt = x.reshape(batch * seq, d)
    memt = mem.reshape(batch * n_mem, d)
    cos, sin = _rope_table(seq)

    for l in range(depth):
        qkv_cols = 3 * MOBA_WIDTH
        blocks = MOBA_WIDTH // INPROJ_COLS
        z_qkv, h = _inproj(xt, w_in[l][:, :qkv_cols].astype(BF16), (cos, sin), seq=seq, name="in_proj_qkv",
                           segments=(("rope_q", blocks), ("rope", blocks), ("plain", blocks)),
                           norm_gain=norm_mix_g[l], bm=NORMED_ROWS, bn=INPROJ_COLS)
        f_pad = d_ff + ff_pad
        rest_cols = w_in.shape[2] - qkv_cols
        o_a, w_rest, w_g, w_u, w_d, w_pa, w_pb, w_o, w_q, w_kv, w_xout = _moba(
            z_qkv, batch=batch, seq=seq, stage=(
                (w_in[l], (d, rest_cols), qkv_cols),
                (w_ff_gate[l], (d, f_pad), 0), (w_ff_up[l], (d, f_pad), 0), (w_ff_down[l], (f_pad, d), 0),
                (w_branch_a[l], w_branch_a[l].shape, 0), (w_branch_b[l], w_branch_b[l].shape, 0),
                (w_out[l], w_out[l].shape, 0), (w_xq[l], w_xq[l].shape, 0), (w_xkv[l], w_xkv[l].shape, 0),
                (w_xo[l], w_xo[l].shape, 0)))
        z = _inproj(h, w_rest, (), seq=seq, name="in_proj_rest", bm=INPROJ_ROWS, bn=INPROJ_COLS,
                    segments=(("gelu", 2 * sgu_width // INPROJ_COLS), ("sigmoid", 2 * d // INPROJ_COLS)))
        o_b = _sgu(z, sgu_ln_g[l], sgu_ln_b[l], w_sgu[l], jnp.transpose(b_sgu[l]), width=sgu_width, u_col=0)
        merged = _merge(o_a, o_b, w_pa, w_pb, z, gate_col=2 * sgu_width)
        xt = _outproj(merged, w_o, xt)

        kv = _memkv(memt, norm_mem_g[l], w_kv, n_mem=n_mem)
        xt, h = _xattn(xt, norm_xattn_g[l], w_q, kv, w_xout, norm_ffn_g[l], seq=seq, n_mem=n_mem)

        a = _ffn_up(h, w_g, w_u)
        xt = _ffn_down(a, w_d, xt, norm_final_g, final_norm=(l == depth - 1))
    return xt.reshape(batch, seq, d)
```

```python
import functools
import math

import jax
import jax.numpy as jnp
from jax import lax
from jax.experimental import pallas as pl
from jax.experimental.pallas import tpu as pltpu

F32 = jnp.float32
BF16 = jnp.bfloat16

HEAD_DIM = 128
MOBA_HEADS = 16
MOBA_WIDTH = MOBA_HEADS * HEAD_DIM
MOBA_BLOCK = 256
MOBA_TOPK = 3
SGU_CHUNK = 128
SGU_GROUPS = 16
XATTN_HEADS = 4
XATTN_WIDTH = XATTN_HEADS * HEAD_DIM
RMS_EPS = 1e-6
LN_EPS = 1e-5
ROPE_THETA = 10000.0
NEG_INF = -1e30
REMOVED = -3e38
Q_SCALE = HEAD_DIM ** -0.5 * math.log2(math.e)

LANES = 128
BF16_ROWS = 16
FF_PAD = 512
INPROJ_ROWS = 1024
INPROJ_COLS = 1024
PROJ_COLS = 256
DOWN_COLS = 1024
NORM_ROWS = 64
VMEM_LIMIT = 60 * 1024 * 1024


def _cparams(sem):
    return pltpu.CompilerParams(dimension_semantics=sem, vmem_limit_bytes=VMEM_LIMIT)


def _rmsnorm_rows(x, g):
    ms = jnp.mean(x * x, axis=-1, keepdims=True)
    return x * lax.rsqrt(ms + RMS_EPS) * g


def _rmsnorm_kernel(x_ref, g_ref, o_ref):
    o_ref[...] = _rmsnorm_rows(x_ref[...], g_ref[...]).astype(o_ref.dtype)


def _rmsnorm(x, g, out_dtype, bm=256):
    t, d = x.shape
    return pl.pallas_call(
        _rmsnorm_kernel,
        grid=(t // bm,),
        in_specs=[pl.BlockSpec((bm, d), lambda i: (i, 0)),
                  pl.BlockSpec((1, d), lambda i: (0, 0))],
        out_specs=pl.BlockSpec((bm, d), lambda i: (i, 0)),
        out_shape=jax.ShapeDtypeStruct((t, d), out_dtype),
        compiler_params=_cparams(("arbitrary",)),
        name="rmsnorm",
    )(x, g.reshape(1, d))


def _rope_kernel(freq_ref, cos_ref, sin_ref, *, rows):
    i = pl.program_id(0)
    pos = (i * rows + lax.broadcasted_iota(jnp.int32, (rows, LANES), 0)).astype(F32)
    lane = lax.broadcasted_iota(jnp.int32, (rows, LANES), 1)
    ang = pos * freq_ref[...]
    cos_ref[...] = jnp.cos(ang)
    s = jnp.sin(ang)
    sin_ref[...] = jnp.where(lane < HEAD_DIM // 2, -s, s)


def _rope_table(seq, rows=1024):
    half = HEAD_DIM // 2
    inv_freq = jnp.power(ROPE_THETA, -(jnp.arange(half, dtype=F32) * 2.0 / HEAD_DIM))
    freq = jnp.concatenate([inv_freq, inv_freq]).reshape(1, HEAD_DIM)
    return pl.pallas_call(
        functools.partial(_rope_kernel, rows=rows),
        grid=(seq // rows,),
        in_specs=[pl.BlockSpec((1, HEAD_DIM), lambda i: (0, 0))],
        out_specs=[pl.BlockSpec((rows, HEAD_DIM), lambda i: (i, 0))] * 2,
        out_shape=[jax.ShapeDtypeStruct((seq, HEAD_DIM), F32)] * 2,
        compiler_params=_cparams(("arbitrary",)),
        name="rope_table",
    )(freq)


def _gelu_tanh(x):
    c = math.sqrt(2.0 / math.pi)
    half = 0.5 * x
    return half + half * jnp.tanh(x * (c + (c * 0.044715) * (x * x)))


def _sigmoid(x):
    return 1.0 / (1.0 + jnp.exp(-x))


def _inproj_kernel(h_ref, w_ref, *refs, bn, segments):
    *rope_refs, z_ref = refs
    n = pl.program_id(1)

    def project(epilogue):
        h = h_ref[...]
        for c in range(bn // PROJ_COLS):
            cols = slice(c * PROJ_COLS, (c + 1) * PROJ_COLS)
            blk = jnp.dot(h, w_ref[:, cols], preferred_element_type=F32)
            z_ref[:, cols] = epilogue(blk).astype(z_ref.dtype)

    def rotary(scale):
        def rotate(blk):
            cos, sin = rope_refs[0][...], rope_refs[1][...]
            heads = []
            for hh in range(PROJ_COLS // HEAD_DIM):
                x = blk[:, hh * HEAD_DIM:(hh + 1) * HEAD_DIM]
                heads.append(x * cos + pltpu.roll(x, HEAD_DIM // 2, axis=1) * sin)
            out = jnp.concatenate(heads, axis=1)
            return out if scale is None else out * scale
        return rotate

    epilogues = {
        "rope_q": rotary(Q_SCALE),
        "rope": rotary(None),
        "plain": lambda blk: blk,
        "gelu": _gelu_tanh,
        "sigmoid": lambda blk: 0.5 + 0.5 * jnp.tanh(0.5 * blk),
    }
    start = 0
    for kind, count in segments:
        pl.when((n >= start) & (n < start + count))(functools.partial(project, epilogues[kind]))
        start += count


def _inproj(h, w, rope, *, seq, segments, name, bm, bn):
    t, d = h.shape
    cols = w.shape[1]
    assert sum(count for _, count in segments) * bn == cols
    seq_blocks = seq // bm
    rope_spec = pl.BlockSpec((bm, HEAD_DIM), lambda m, n: (m % seq_blocks, 0))
    return pl.pallas_call(
        functools.partial(_inproj_kernel, bn=bn, segments=segments),
        grid=(t // bm, cols // bn),
        in_specs=[pl.BlockSpec((bm, d), lambda m, n: (m, 0)),
                  pl.BlockSpec((d, bn), lambda m, n: (0, n))] + [rope_spec] * len(rope),
        out_specs=pl.BlockSpec((bm, bn), lambda m, n: (m, n)),
        out_shape=jax.ShapeDtypeStruct((t, cols), BF16),
        compiler_params=_cparams(("arbitrary", "arbitrary")),
        name=name,
    )(h, w, *rope)


def _moba_kernel(q_ref, k_ref, v_ref, *refs, tile, n_blocks, n_sub, heads, stage_args, groups, tiles):
    n_stage = len(stage_args)
    stage_src, o_ref, stage_dst = refs[:n_stage], refs[n_stage], refs[n_stage + 1:2 * n_stage + 1]
    onehot_ref, ones_ref, kmean_ref, p_ref, m_ref, alpha_ref, acc_ref = refs[2 * n_stage + 1:]
    step = (pl.program_id(0) * groups + pl.program_id(1)) * tiles + pl.program_id(2)
    pieces = [p for src, dst, args in zip(stage_src, stage_dst, stage_args)
              for p in _stage_pieces(step, src, dst, **args)]

    def stage_weights(slot, n_slots):
        for p in pieces[slot::n_slots]:
            p()

    _moba_body(q_ref, k_ref, v_ref, o_ref, onehot_ref, ones_ref, kmean_ref, p_ref, m_ref, alpha_ref, acc_ref,
               tile=tile, n_blocks=n_blocks, n_sub=n_sub, heads=heads, side_work=stage_weights)


def _moba_body(q_ref, k_ref, v_ref, o_ref, onehot_ref, ones_ref, kmean_ref, p_ref, m_ref, alpha_ref, acc_ref,
               *, tile, n_blocks, n_sub, heads, side_work):
    qi = pl.program_id(2)
    blocks_per_tile = tile // MOBA_BLOCK
    contract_last = (((1,), (1,)), ((), ()))
    sub = tile // n_sub

    chains = [(hd, i) for hd in range(heads) for i in range(n_sub)]

    def head_cols(hd):
        return slice(hd * HEAD_DIM, (hd + 1) * HEAD_DIM)

    @pl.when(qi == 0)
    def _prepare_keys():
        kmean_ref[...] = jnp.zeros_like(kmean_ref)
        lane = lax.broadcasted_iota(jnp.int32, (MOBA_BLOCK, LANES), 1)
        ones_ref[...] = (lax.broadcasted_iota(jnp.int32, ones_ref.shape, 1) == 0).astype(BF16)
        for j in range(n_blocks):
            rows = pl.ds(j * MOBA_BLOCK, MOBA_BLOCK)
            onehot_ref[rows, :] = (lane == j).astype(BF16)
            for hd in range(heads):
                kb = k_ref[rows, head_cols(hd)]
                kmean_ref[hd, j:j + 1, :] = jnp.mean(kb.astype(F32), axis=0, keepdims=True)

    blk = lax.broadcasted_iota(jnp.int32, (n_blocks, tile), 0)
    col = lax.broadcasted_iota(jnp.int32, (n_blocks, tile), 1)
    own_block = qi * blocks_per_tile + col // MOBA_BLOCK
    q_ext = []
    for hd in range(heads):
        q = q_ref[:, head_cols(hd)]
        gate = lax.dot_general(kmean_ref[hd].astype(BF16), q, contract_last,
                               preferred_element_type=F32)[:n_blocks, :]
        gate = jnp.where(blk < own_block, gate, NEG_INF)
        allowed = blk == own_block
        for _ in range(MOBA_TOPK):
            best = jnp.max(gate, axis=0, keepdims=True)
            first = jnp.min(jnp.where(gate == best, blk, n_blocks), axis=0, keepdims=True)
            hit = blk == first
            allowed = allowed | (hit & (best > 0.5 * NEG_INF))
            gate = jnp.where(hit, REMOVED, gate)
        bias_t = jnp.where(allowed, 0.0, NEG_INF)
        bias_t = jnp.concatenate([bias_t, jnp.zeros((LANES - n_blocks, tile), F32)], axis=0)
        bias = jnp.transpose(bias_t).astype(BF16)
        q_ext.append(jnp.concatenate([q, bias], axis=1))

    def keys(hd, j, rows=tile):
        at = pl.ds(pl.multiple_of(j * tile, tile), rows)
        return jnp.concatenate([k_ref[at, head_cols(hd)], onehot_ref[at, :]], axis=1)

    def values(hd, j):
        at = pl.ds(pl.multiple_of(j * tile, tile), tile)
        return jnp.concatenate([v_ref[at, head_cols(hd)], ones_ref[...]], axis=1)

    def scores(hd, i, k_tile):
        return lax.dot_general(q_ext[hd][i * sub:(i + 1) * sub], k_tile, contract_last,
                               preferred_element_type=F32)

    def weighted_values(hd, i, v_tile):
        return jnp.dot(p_ref[hd, i * sub:(i + 1) * sub, :], v_tile, preferred_element_type=F32)

    for hd, i in chains:
        rows = slice(i * sub, (i + 1) * sub)
        seen = (i + 1) * sub
        r = i * sub + lax.broadcasted_iota(jnp.int32, (sub, seen), 0)
        c = lax.broadcasted_iota(jnp.int32, (sub, seen), 1)
        s = jnp.where(c <= r, scores(hd, i, keys(hd, qi, seen)), NEG_INF)
        m0 = jnp.max(s, axis=1, keepdims=True)
        p_ref[hd, rows, :seen] = jnp.exp2(s - m0).astype(BF16)
        if seen < tile:
            p_ref[hd, rows, seen:] = jnp.zeros((sub, tile - seen), BF16)
        m_ref[hd, rows, :] = m0
        alpha_ref[hd, rows, :] = jnp.ones((sub, 1), F32)
        acc_ref[hd, rows, :] = jnp.zeros((sub, 2 * HEAD_DIM), F32)

    def accumulate(hd, i, v_tile):
        rows = slice(i * sub, (i + 1) * sub)
        return alpha_ref[hd, rows, :] * acc_ref[hd, rows, :] + weighted_values(hd, i, v_tile)

    def body(j, carry):
        j_prev = jnp.where(j == 0, qi, j - 1)
        k_tiles = [keys(hd, j) for hd in range(heads)]
        v_tiles = [values(hd, j_prev) for hd in range(heads)]
        for hd, i in chains:
            rows = slice(i * sub, (i + 1) * sub)
            acc_ref[hd, rows, :] = accumulate(hd, i, v_tiles[hd])
            s = scores(hd, i, k_tiles[hd])
            m = m_ref[hd, rows, :]
            m_new = jnp.maximum(m, jnp.max(s, axis=1, keepdims=True))
            p_ref[hd, rows, :] = jnp.exp2(s - m_new).astype(BF16)
            alpha_ref[hd, rows, :] = jnp.exp2(m - m_new)
            m_ref[hd, rows, :] = m_new
        return carry

    lax.fori_loop(0, qi, body, 0)
    j_last = jnp.where(qi == 0, qi, qi - 1)
    for n, (hd, i) in enumerate(chains):
        acc = accumulate(hd, i, values(hd, j_last))
        o_ref[i * sub:(i + 1) * sub, head_cols(hd)] = (
            acc[:, :HEAD_DIM] / acc[:, HEAD_DIM:HEAD_DIM + 1]).astype(o_ref.dtype)
        side_work(n, len(chains))


def _stage_plan(cols_used, col_start, out_shape, steps):
    rows_out, cols_out = out_shape
    for col_blocks in (1, 2, 4):
        row_blocks = steps // col_blocks
        rb = rows_out // row_blocks
        if steps % col_blocks or rows_out % row_blocks or rb % BF16_ROWS:
            continue
        cb_in = cols_used if col_blocks == 1 else cols_out // col_blocks
        if col_blocks > 1 and (cols_used != cols_out or cb_in % LANES):
            continue
        if col_start % cb_in:
            continue
        return rb, col_blocks, cb_in
    raise ValueError(f"no staging plan for {cols_used} columns at {col_start} -> {out_shape} in {steps} steps")


def _stage_pieces(step, src_ref, dst_ref, *, rows, rows_out, col_blocks):
    rb, cb_in = src_ref.shape
    cb_out = dst_ref.shape[1]
    row0 = (step // col_blocks) * rb

    def piece(r0, nr, c0, nc):
        def run():
            x = src_ref[r0:r0 + nr, c0:c0 + nc]
            if rows_out > rows:
                row = row0 + r0 + lax.broadcasted_iota(jnp.int32, x.shape, 0)
                x = jnp.where(row < rows, x, 0.0)
            dst_ref[r0:r0 + nr, c0:c0 + nc] = x.astype(dst_ref.dtype)
            if c0 + nc == cb_in and cb_out > cb_in:
                dst_ref[r0:r0 + nr, cb_in:] = jnp.zeros((nr, cb_out - cb_in), dst_ref.dtype)
        return run

    if (rb // 2) % BF16_ROWS == 0:
        return [piece(0, rb // 2, 0, cb_in), piece(rb // 2, rb // 2, 0, cb_in)]
    assert (cb_in // 2) % LANES == 0
    return [piece(0, rb, 0, cb_in // 2), piece(0, rb, cb_in // 2, cb_in // 2)]


def _moba(z, *, batch, seq, stage=(), tile=1024, n_sub=4, heads=2):
    t = z.shape[0]
    n_blocks = seq // MOBA_BLOCK
    tiles = seq // tile
    groups = MOBA_HEADS // heads
    width = heads * HEAD_DIM
    steps = batch * groups * tiles

    def step_of(b, g, i):
        return (b * groups + g) * tiles + i

    stage_in, stage_out, stage_shapes, stage_args = [], [], [], []
    for w, out_shape, col_start in stage:
        cols_used = min(w.shape[1] - col_start, out_shape[1])
        rb, col_blocks, cb_in = _stage_plan(cols_used, col_start, out_shape, steps)
        cb_out = out_shape[1] // col_blocks
        last = (w.shape[0] - 1) // rb
        stage_in.append(pl.BlockSpec(
            (rb, cb_in), lambda b, g, i, cbs=col_blocks, last=last, c0=col_start // cb_in:
            (jnp.minimum(step_of(b, g, i) // cbs, last), c0 + step_of(b, g, i) % cbs)))
        stage_out.append(pl.BlockSpec(
            (rb, cb_out), lambda b, g, i, cbs=col_blocks: (step_of(b, g, i) // cbs, step_of(b, g, i) % cbs)))
        stage_shapes.append(jax.ShapeDtypeStruct(out_shape, BF16))
        stage_args.append(dict(rows=w.shape[0], rows_out=out_shape[0], col_blocks=col_blocks))

    kern = functools.partial(_moba_kernel, tile=tile, n_blocks=n_blocks, n_sub=n_sub, heads=heads,
                             stage_args=tuple(stage_args), groups=groups, tiles=tiles)
    outs = pl.pallas_call(
        kern,
        grid=(batch, groups, tiles),
        in_specs=[pl.BlockSpec((tile, width), lambda b, g, i: (b * tiles + i, g)),
                  pl.BlockSpec((seq, width), lambda b, g, i: (b, groups + g)),
                  pl.BlockSpec((seq, width), lambda b, g, i: (b, 2 * groups + g))] + stage_in,
        out_specs=[pl.BlockSpec((tile, width), lambda b, g, i: (b * tiles + i, g))] + stage_out,
        out_shape=[jax.ShapeDtypeStruct((t, MOBA_WIDTH), BF16)] + stage_shapes,
        scratch_shapes=[pltpu.VMEM((seq, LANES), BF16),
                        pltpu.VMEM((tile, LANES), BF16),
                        pltpu.VMEM((heads, LANES, HEAD_DIM), F32),
                        pltpu.VMEM((heads, tile, tile), BF16),
                        pltpu.VMEM((heads, tile, 1), F32),
                        pltpu.VMEM((heads, tile, 1), F32),
                        pltpu.VMEM((heads, tile, 2 * HEAD_DIM), F32)],
        compiler_params=_cparams(("arbitrary", "arbitrary", "arbitrary")),
        name="moba_attention",
    )(z, z, z, *[item[0] for item in stage])
    return outs[0] if not stage else tuple(outs)


def _sgu_kernel(u_ref, v_ref, g_ref, b_ref, w_ref, bs_ref, o_ref, *, rows):
    v = v_ref[...].astype(F32)
    mu = jnp.mean(v, axis=-1, keepdims=True)
    var = jnp.mean(jnp.square(v - mu), axis=-1, keepdims=True)
    vln = ((v - mu) * lax.rsqrt(var + LN_EPS) * g_ref[...] + b_ref[...]).astype(BF16)
    t_idx = lax.broadcasted_iota(jnp.int32, (SGU_CHUNK, SGU_CHUNK), 0)
    s_idx = lax.broadcasted_iota(jnp.int32, (SGU_CHUNK, SGU_CHUNK), 1)
    causal = s_idx <= t_idx
    for g in range(SGU_GROUPS):
        w = jnp.where(causal, w_ref[g], 0.0).astype(BF16)
        bias = bs_ref[:, g:g + 1]
        cols = slice(g * LANES, (g + 1) * LANES)
        for ch in range(rows // SGU_CHUNK):
            rws = slice(ch * SGU_CHUNK, (ch + 1) * SGU_CHUNK)
            mixed = jnp.dot(w, vln[rws, cols], preferred_element_type=F32) + bias
            o_ref[rws, cols] = (u_ref[rws, cols].astype(F32) * mixed).astype(o_ref.dtype)


def _sgu(z, ln_g, ln_b, w_s, b_s_t, *, width, u_col, rows=512):
    t = z.shape[0]
    u_blk = u_col // width
    kern = functools.partial(_sgu_kernel, rows=rows)
    return pl.pallas_call(
        kern,
        grid=(t // rows,),
        in_specs=[pl.BlockSpec((rows, width), lambda i: (i, u_blk)),
                  pl.BlockSpec((rows, width), lambda i: (i, u_blk + 1)),
                  pl.BlockSpec((1, width), lambda i: (0, 0)),
                  pl.BlockSpec((1, width), lambda i: (0, 0)),
                  pl.BlockSpec((SGU_GROUPS, SGU_CHUNK, SGU_CHUNK), lambda i: (0, 0, 0)),
                  pl.BlockSpec((SGU_CHUNK, SGU_GROUPS), lambda i: (0, 0))],
        out_specs=pl.BlockSpec((rows, width), lambda i: (i, 0)),
        out_shape=jax.ShapeDtypeStruct((t, width), BF16),
        compiler_params=_cparams(("arbitrary",)),
        name="spatial_gating",
    )(z, z, ln_g.reshape(1, width), ln_b.reshape(1, width), w_s, b_s_t)


def _merge_kernel(oa_ref, ob_ref, pa_ref, pb_ref, ga_ref, gb_ref, o_ref):
    oa, ob = oa_ref[...], ob_ref[...]
    for c in range(o_ref.shape[1] // PROJ_COLS):
        cols = slice(c * PROJ_COLS, (c + 1) * PROJ_COLS)
        a = jnp.dot(oa, pa_ref[:, cols], preferred_element_type=F32)
        b = jnp.dot(ob, pb_ref[:, cols], preferred_element_type=F32)
        o_ref[:, cols] = (ga_ref[:, cols].astype(F32) * a + gb_ref[:, cols].astype(F32) * b).astype(o_ref.dtype)


def _merge(o_a, o_b, p_a, p_b, z, *, gate_col, bm=1024, bn=1024):
    t = o_a.shape[0]
    d = p_a.shape[1]
    ga_blk = gate_col // bn
    gb_blk = (gate_col + d) // bn
    return pl.pallas_call(
        _merge_kernel,
        grid=(t // bm, d // bn),
        in_specs=[pl.BlockSpec((bm, o_a.shape[1]), lambda m, n: (m, 0)),
                  pl.BlockSpec((bm, o_b.shape[1]), lambda m, n: (m, 0)),
                  pl.BlockSpec((p_a.shape[0], bn), lambda m, n: (0, n)),
                  pl.BlockSpec((p_b.shape[0], bn), lambda m, n: (0, n)),
                  pl.BlockSpec((bm, bn), lambda m, n: (m, ga_blk + n)),
                  pl.BlockSpec((bm, bn), lambda m, n: (m, gb_blk + n))],
        out_specs=pl.BlockSpec((bm, bn), lambda m, n: (m, n)),
        out_shape=jax.ShapeDtypeStruct((t, d), BF16),
        compiler_params=_cparams(("arbitrary", "arbitrary")),
        name="branch_merge",
    )(o_a, o_b, p_a, p_b, z, z)


def _outproj_kernel(a_ref, w_ref, x_ref, o_ref):
    o_ref[...] = x_ref[...] + jnp.dot(a_ref[...], w_ref[...], preferred_element_type=F32)


def _outproj(a, w, x, *, bm=1024, bn=1024):
    t, k = a.shape
    d = w.shape[1]
    return pl.pallas_call(
        _outproj_kernel,
        grid=(t // bm, d // bn),
        in_specs=[pl.BlockSpec((bm, k), lambda m, n: (m, 0)),
                  pl.BlockSpec((k, bn), lambda m, n: (0, n)),
                  pl.BlockSpec((bm, bn), lambda m, n: (m, n))],
        out_specs=pl.BlockSpec((bm, bn), lambda m, n: (m, n)),
        out_shape=jax.ShapeDtypeStruct((t, d), F32),
        compiler_params=_cparams(("arbitrary", "arbitrary")),
        name="out_proj",
    )(a, w, x)


def _memkv_kernel(mem_ref, g_ref, w_ref, o_ref):
    mem_n = _rmsnorm_rows(mem_ref[...], g_ref[...]).astype(BF16)
    o_ref[...] = jnp.dot(mem_n, w_ref[...], preferred_element_type=F32).astype(o_ref.dtype)


def _memkv(mem, g, w, *, n_mem):
    t, d = mem.shape
    cols = w.shape[1]
    return pl.pallas_call(
        _memkv_kernel,
        grid=(t // n_mem,),
        in_specs=[pl.BlockSpec((n_mem, d), lambda i: (i, 0)),
                  pl.BlockSpec((1, d), lambda i: (0, 0)),
                  pl.BlockSpec((d, cols), lambda i: (0, 0))],
        out_specs=pl.BlockSpec((n_mem, cols), lambda i: (i, 0)),
        out_shape=jax.ShapeDtypeStruct((t, cols), BF16),
        compiler_params=_cparams(("arbitrary",)),
        name="mem_kv",
    )(mem, g.reshape(1, d), w)


def _xattn_kernel(x_ref, gx_ref, wq_ref, kv_ref, wo_ref, gf_ref, x_out_ref, h_out_ref, *, chain_rows):
    chains = [slice(c * chain_rows, (c + 1) * chain_rows) for c in range(x_ref.shape[0] // chain_rows)]

    def attend(q):
        outs = []
        for hh in range(XATTN_HEADS):
            cols = slice(hh * HEAD_DIM, (hh + 1) * HEAD_DIM)
            k = kv_ref[:, hh * HEAD_DIM:(hh + 1) * HEAD_DIM]
            v = kv_ref[:, XATTN_WIDTH + hh * HEAD_DIM:XATTN_WIDTH + (hh + 1) * HEAD_DIM]
            s = lax.dot_general(q[:, cols], k, (((1,), (1,)), ((), ())), preferred_element_type=F32)
            s = s - jnp.max(s, axis=-1, keepdims=True)
            p = jnp.exp(s)
            p = p / jnp.sum(p, axis=-1, keepdims=True)
            outs.append(jnp.dot(p.astype(BF16), v, preferred_element_type=F32).astype(BF16))
        return jnp.concatenate(outs, axis=1)

    hs = [_rmsnorm_rows(x_ref[rows, :], gx_ref[...]).astype(BF16) for rows in chains]
    qs = [(jnp.dot(h, wq_ref[...], preferred_element_type=F32) * (HEAD_DIM ** -0.5)).astype(BF16) for h in hs]
    os = [attend(q) for q in qs]
    for rows, o in zip(chains, os):
        x_out_ref[rows, :] = x_ref[rows, :] + jnp.dot(o, wo_ref[...], preferred_element_type=F32)
    for rows in chains:
        h_out_ref[rows, :] = _rmsnorm_rows(x_out_ref[rows, :], gf_ref[...]).astype(h_out_ref.dtype)


def _xattn(x, g_x, w_q, kv, w_o, g_ffn, *, seq, n_mem, bm=512, chain_rows=256):
    t, d = x.shape
    per_batch = seq // bm
    resident = pl.Buffered(1)
    return pl.pallas_call(
        functools.partial(_xattn_kernel, chain_rows=chain_rows),
        grid=(t // bm,),
        in_specs=[pl.BlockSpec((bm, d), lambda i: (i, 0)),
                  pl.BlockSpec((1, d), lambda i: (0, 0)),
                  pl.BlockSpec(w_q.shape, lambda i: (0, 0), pipeline_mode=resident),
                  pl.BlockSpec((n_mem, kv.shape[1]), lambda i: (i // per_batch, 0)),
                  pl.BlockSpec(w_o.shape, lambda i: (0, 0), pipeline_mode=resident),
                  pl.BlockSpec((1, d), lambda i: (0, 0))],
        out_specs=[pl.BlockSpec((bm, d), lambda i: (i, 0)),
                   pl.BlockSpec((bm, d), lambda i: (i, 0))],
        out_shape=[jax.ShapeDtypeStruct((t, d), F32),
                   jax.ShapeDtypeStruct((t, d), BF16)],
        compiler_params=_cparams(("arbitrary",)),
        name="mem_cross_attention",
    )(x, g_x.reshape(1, d), w_q, kv, w_o, g_ffn.reshape(1, d))


def _ffn_up_kernel(h_ref, wg_ref, wu_ref, o_ref, *, last_cols):
    n = pl.program_id(1)
    bn = o_ref.shape[1]

    def swiglu(cols):
        h = h_ref[...]
        g = jnp.dot(h, wg_ref[:, :cols], preferred_element_type=F32)
        u = jnp.dot(h, wu_ref[:, :cols], preferred_element_type=F32)
        o_ref[:, :cols] = (g * _sigmoid(g) * u).astype(o_ref.dtype)
        if cols < bn:
            o_ref[:, cols:] = jnp.zeros((o_ref.shape[0], bn - cols), o_ref.dtype)

    last = pl.num_programs(1) - 1
    pl.when(n < last)(functools.partial(swiglu, bn))
    pl.when(n == last)(functools.partial(swiglu, last_cols))


def _ffn_up(h, w_g, w_u, *, d_ff, bm=1024, bn=512):
    t, d = h.shape
    f = w_g.shape[1]
    return pl.pallas_call(
        functools.partial(_ffn_up_kernel, last_cols=d_ff - (f // bn - 1) * bn),
        grid=(t // bm, f // bn),
        in_specs=[pl.BlockSpec((bm, d), lambda m, n: (m, 0)),
                  pl.BlockSpec((d, bn), lambda m, n: (0, n)),
                  pl.BlockSpec((d, bn), lambda m, n: (0, n))],
        out_specs=pl.BlockSpec((bm, bn), lambda m, n: (m, n)),
        out_shape=jax.ShapeDtypeStruct((t, f), BF16),
        compiler_params=_cparams(("arbitrary", "arbitrary")),
        name="ffn_up",
    )(h, w_g, w_u)


def _ffn_down_kernel(a_ref, w_ref, x_ref, g_ref, o_ref, *, final_norm, last_rows):
    k = pl.program_id(1)
    last = pl.num_programs(1) - 1
    bm, d = o_ref.shape

    @pl.when(k == 0)
    def _residual():
        o_ref[...] = x_ref[...]

    def accumulate(depth):
        a = a_ref[:, :depth]
        for c in range(d // DOWN_COLS):
            cols = slice(c * DOWN_COLS, (c + 1) * DOWN_COLS)
            o_ref[:, cols] += jnp.dot(a, w_ref[:depth, cols], preferred_element_type=F32)

    pl.when(k < last)(functools.partial(accumulate, a_ref.shape[1]))
    pl.when(k == last)(functools.partial(accumulate, last_rows))

    if final_norm:
        @pl.when(k == last)
        def _final_norm():
            for r in range(bm // NORM_ROWS):
                rows = slice(r * NORM_ROWS, (r + 1) * NORM_ROWS)
                o_ref[rows, :] = _rmsnorm_rows(o_ref[rows, :], g_ref[...])


def _ffn_down(a, w, x, g, *, final_norm, d_ff, bm=512, bk=1024):
    t, f = a.shape
    d = w.shape[1]
    return pl.pallas_call(
        functools.partial(_ffn_down_kernel, final_norm=final_norm, last_rows=d_ff - (f // bk - 1) * bk),
        grid=(t // bm, f // bk),
        in_specs=[pl.BlockSpec((bm, bk), lambda m, k: (m, k)),
                  pl.BlockSpec((bk, d), lambda m, k: (k, 0)),
                  pl.BlockSpec((bm, d), lambda m, k: (m, 0)),
                  pl.BlockSpec((1, d), lambda m, k: (0, 0))],
        out_specs=pl.BlockSpec((bm, d), lambda m, k: (m, 0)),
        out_shape=jax.ShapeDtypeStruct((t, d), F32),
        compiler_params=_cparams(("arbitrary", "arbitrary")),
        name="ffn_down",
    )(a, w, x, g.reshape(1, d))


@jax.jit
def kernel(x, mem, norm_mix_g, w_in, sgu_ln_g, sgu_ln_b, w_sgu, b_sgu, w_branch_a, w_branch_b, w_out, norm_xattn_g, norm_mem_g, w_xq, w_xkv, w_xo, norm_ffn_g, w_ff_gate, w_ff_up, w_ff_down, norm_final_g):
    batch, seq, d = x.shape
    n_mem = mem.shape[1]
    depth = w_in.shape[0]
    sgu_width = sgu_ln_g.shape[1]
    d_ff = w_ff_gate.shape[2]
    ff_pad = (-d_ff) % FF_PAD

    xt = x.reshape(batch * seq, d)
    memt = mem.reshape(batch * n_mem, d)
    cos, sin = _rope_table(seq)

    for l in range(depth):
        h = _rmsnorm(xt, norm_mix_g[l], BF16)
        qkv_cols = 3 * MOBA_WIDTH
        blocks = MOBA_WIDTH // INPROJ_COLS
        z_qkv = _inproj(h, w_in[l][:, :qkv_cols].astype(BF16), (cos, sin), seq=seq, name="in_proj_qkv",
                        segments=(("rope_q", blocks), ("rope", blocks), ("plain", blocks)),
                        bm=INPROJ_ROWS, bn=INPROJ_COLS)
        f_pad = d_ff + ff_pad
        rest_cols = w_in.shape[2] - qkv_cols
        o_a, w_rest, w_g, w_u, w_d, w_pa, w_pb, w_o, w_q, w_kv, w_xout = _moba(
            z_qkv, batch=batch, seq=seq, stage=(
                (w_in[l], (d, rest_cols), qkv_cols),
                (w_ff_gate[l], (d, f_pad), 0), (w_ff_up[l], (d, f_pad), 0), (w_ff_down[l], (f_pad, d), 0),
                (w_branch_a[l], w_branch_a[l].shape, 0), (w_branch_b[l], w_branch_b[l].shape, 0),
                (w_out[l], w_out[l].shape, 0), (w_xq[l], w_xq[l].shape, 0), (w_xkv[l], w_xkv[l].shape, 0),
                (w_xo[l], w_xo[l].shape, 0)))
        z = _inproj(h, w_rest, (), seq=seq, name="in_proj_rest", bm=INPROJ_ROWS, bn=INPROJ_COLS,
                    segments=(("gelu", 2 * sgu_width // INPROJ_COLS), ("sigmoid", 2 * d // INPROJ_COLS)))
        o_b = _sgu(z, sgu_ln_g[l], sgu_ln_b[l], w_sgu[l], jnp.transpose(b_sgu[l]), width=sgu_width, u_col=0)
        merged = _merge(o_a, o_b, w_pa, w_pb, z, gate_col=2 * sgu_width)
        xt = _outproj(merged, w_o, xt)

        kv = _memkv(memt, norm_mem_g[l], w_kv, n_mem=n_mem)
        xt, h = _xattn(xt, norm_xattn_g[l], w_q, kv, w_xout, norm_ffn_g[l], seq=seq, n_mem=n_mem)

        a = _ffn_up(h, w_g, w_u, d_ff=d_ff)
        xt = _ffn_down(a, w_d, xt, norm_final_g, final_norm=(l == depth - 1), d_ff=d_ff)
    return xt.reshape(batch, seq, d)
```

```python
import functools
import math

import jax
import jax.numpy as jnp
from jax import lax
from jax.experimental import pallas as pl
from jax.experimental.pallas import tpu as pltpu

F32 = jnp.float32
BF16 = jnp.bfloat16

HEAD_DIM = 128
MOBA_HEADS = 16
MOBA_WIDTH = MOBA_HEADS * HEAD_DIM
MOBA_BLOCK = 256
MOBA_TOPK = 3
SGU_CHUNK = 128
SGU_GROUPS = 16
XATTN_HEADS = 4
XATTN_WIDTH = XATTN_HEADS * HEAD_DIM
RMS_EPS = 1e-6
LN_EPS = 1e-5
ROPE_THETA = 10000.0
NEG_INF = -1e30
REMOVED = -3e38
Q_SCALE = HEAD_DIM ** -0.5 * math.log2(math.e)

LANES = 128
BF16_ROWS = 16
FF_PAD = 512
INPROJ_ROWS = 1024
INPROJ_COLS = 1024
PROJ_COLS = 256
DOWN_COLS = 1024
NORM_ROWS = 64
VMEM_LIMIT = 60 * 1024 * 1024


def _cparams(sem):
    return pltpu.CompilerParams(dimension_semantics=sem, vmem_limit_bytes=VMEM_LIMIT)


def _rmsnorm_rows(x, g):
    ms = jnp.mean(x * x, axis=-1, keepdims=True)
    return x * lax.rsqrt(ms + RMS_EPS) * g


def _rmsnorm_kernel(x_ref, g_ref, o_ref):
    o_ref[...] = _rmsnorm_rows(x_ref[...], g_ref[...]).astype(o_ref.dtype)


def _rmsnorm(x, g, out_dtype, bm=512):
    t, d = x.shape
    return pl.pallas_call(
        _rmsnorm_kernel,
        grid=(t // bm,),
        in_specs=[pl.BlockSpec((bm, d), lambda i: (i, 0)),
                  pl.BlockSpec((1, d), lambda i: (0, 0))],
        out_specs=pl.BlockSpec((bm, d), lambda i: (i, 0)),
        out_shape=jax.ShapeDtypeStruct((t, d), out_dtype),
        compiler_params=_cparams(("arbitrary",)),
        name="rmsnorm",
    )(x, g.reshape(1, d))


def _rope_kernel(freq_ref, cos_ref, sin_ref, *, rows):
    i = pl.program_id(0)
    pos = (i * rows + lax.broadcasted_iota(jnp.int32, (rows, LANES), 0)).astype(F32)
    lane = lax.broadcasted_iota(jnp.int32, (rows, LANES), 1)
    ang = pos * freq_ref[...]
    cos_ref[...] = jnp.cos(ang)
    s = jnp.sin(ang)
    sin_ref[...] = jnp.where(lane < HEAD_DIM // 2, -s, s)


def _rope_table(seq, rows=1024):
    half = HEAD_DIM // 2
    inv_freq = jnp.power(ROPE_THETA, -(jnp.arange(half, dtype=F32) * 2.0 / HEAD_DIM))
    freq = jnp.concatenate([inv_freq, inv_freq]).reshape(1, HEAD_DIM)
    return pl.pallas_call(
        functools.partial(_rope_kernel, rows=rows),
        grid=(seq // rows,),
        in_specs=[pl.BlockSpec((1, HEAD_DIM), lambda i: (0, 0))],
        out_specs=[pl.BlockSpec((rows, HEAD_DIM), lambda i: (i, 0))] * 2,
        out_shape=[jax.ShapeDtypeStruct((seq, HEAD_DIM), F32)] * 2,
        compiler_params=_cparams(("arbitrary",)),
        name="rope_table",
    )(freq)


def _gelu_tanh(x):
    c = math.sqrt(2.0 / math.pi)
    half = 0.5 * x
    return half + half * jnp.tanh(x * (c + (c * 0.044715) * (x * x)))


def _sigmoid(x):
    return 1.0 / (1.0 + jnp.exp(-x))


def _inproj_kernel(h_ref, w_ref, *refs, bn, segments):
    *rope_refs, z_ref = refs
    n = pl.program_id(1)

    def project(epilogue):
        h = h_ref[...]
        for c in range(bn // PROJ_COLS):
            cols = slice(c * PROJ_COLS, (c + 1) * PROJ_COLS)
            blk = jnp.dot(h, w_ref[:, cols], preferred_element_type=F32)
            z_ref[:, cols] = epilogue(blk).astype(z_ref.dtype)

    def rotary(scale):
        def rotate(blk):
            cos, sin = rope_refs[0][...], rope_refs[1][...]
            heads = []
            for hh in range(PROJ_COLS // HEAD_DIM):
                x = blk[:, hh * HEAD_DIM:(hh + 1) * HEAD_DIM]
                heads.append(x * cos + pltpu.roll(x, HEAD_DIM // 2, axis=1) * sin)
            out = jnp.concatenate(heads, axis=1)
            return out if scale is None else out * scale
        return rotate

    epilogues = {
        "rope_q": rotary(Q_SCALE),
        "rope": rotary(None),
        "plain": lambda blk: blk,
        "gelu": _gelu_tanh,
        "sigmoid": lambda blk: 0.5 + 0.5 * jnp.tanh(0.5 * blk),
    }
    start = 0
    for kind, count in segments:
        pl.when((n >= start) & (n < start + count))(functools.partial(project, epilogues[kind]))
        start += count


def _inproj(h, w, rope, *, seq, segments, name, bm, bn):
    t, d = h.shape
    cols = w.shape[1]
    assert sum(count for _, count in segments) * bn == cols
    seq_blocks = seq // bm
    rope_spec = pl.BlockSpec((bm, HEAD_DIM), lambda m, n: (m % seq_blocks, 0))
    return pl.pallas_call(
        functools.partial(_inproj_kernel, bn=bn, segments=segments),
        grid=(t // bm, cols // bn),
        in_specs=[pl.BlockSpec((bm, d), lambda m, n: (m, 0)),
                  pl.BlockSpec((d, bn), lambda m, n: (0, n))] + [rope_spec] * len(rope),
        out_specs=pl.BlockSpec((bm, bn), lambda m, n: (m, n)),
        out_shape=jax.ShapeDtypeStruct((t, cols), BF16),
        compiler_params=_cparams(("arbitrary", "arbitrary")),
        name=name,
    )(h, w, *rope)


def _moba_kernel(q_ref, k_ref, v_ref, *refs, tile, n_blocks, n_sub, heads, stage_args, groups, tiles):
    n_stage = len(stage_args)
    stage_src, o_ref, stage_dst = refs[:n_stage], refs[n_stage], refs[n_stage + 1:2 * n_stage + 1]
    onehot_ref, ones_ref, kmean_ref, p_ref, m_ref, alpha_ref, acc_ref = refs[2 * n_stage + 1:]
    step = (pl.program_id(0) * groups + pl.program_id(1)) * tiles + pl.program_id(2)
    pieces = [p for src, dst, args in zip(stage_src, stage_dst, stage_args)
              for p in _stage_pieces(step, src, dst, **args)]

    def stage_weights(slot, n_slots):
        for p in pieces[slot::n_slots]:
            p()

    _moba_body(q_ref, k_ref, v_ref, o_ref, onehot_ref, ones_ref, kmean_ref, p_ref, m_ref, alpha_ref, acc_ref,
               tile=tile, n_blocks=n_blocks, n_sub=n_sub, heads=heads, side_work=stage_weights)


def _moba_body(q_ref, k_ref, v_ref, o_ref, onehot_ref, ones_ref, kmean_ref, p_ref, m_ref, alpha_ref, acc_ref,
               *, tile, n_blocks, n_sub, heads, side_work):
    qi = pl.program_id(2)
    blocks_per_tile = tile // MOBA_BLOCK
    contract_last = (((1,), (1,)), ((), ()))
    sub = tile // n_sub

    chains = [(hd, i) for hd in range(heads) for i in range(n_sub)]

    def head_cols(hd):
        return slice(hd * HEAD_DIM, (hd + 1) * HEAD_DIM)

    @pl.when(qi == 0)
    def _prepare_keys():
        kmean_ref[...] = jnp.zeros_like(kmean_ref)
        lane = lax.broadcasted_iota(jnp.int32, (MOBA_BLOCK, LANES), 1)
        ones_ref[...] = (lax.broadcasted_iota(jnp.int32, ones_ref.shape, 1) == 0).astype(BF16)
        for j in range(n_blocks):
            rows = pl.ds(j * MOBA_BLOCK, MOBA_BLOCK)
            onehot_ref[rows, :] = (lane == j).astype(BF16)
            for hd in range(heads):
                kb = k_ref[rows, head_cols(hd)]
                kmean_ref[hd, j:j + 1, :] = jnp.mean(kb.astype(F32), axis=0, keepdims=True)

    blk = lax.broadcasted_iota(jnp.int32, (n_blocks, tile), 0)
    col = lax.broadcasted_iota(jnp.int32, (n_blocks, tile), 1)
    own_block = qi * blocks_per_tile + col // MOBA_BLOCK
    q_ext = []
    for hd in range(heads):
        q = q_ref[:, head_cols(hd)]
        gate = lax.dot_general(kmean_ref[hd].astype(BF16), q, contract_last,
                               preferred_element_type=F32)[:n_blocks, :]
        gate = jnp.where(blk < own_block, gate, NEG_INF)
        allowed = blk == own_block
        for _ in range(MOBA_TOPK):
            best = jnp.max(gate, axis=0, keepdims=True)
            first = jnp.min(jnp.where(gate == best, blk, n_blocks), axis=0, keepdims=True)
            hit = blk == first
            allowed = allowed | (hit & (best > 0.5 * NEG_INF))
            gate = jnp.where(hit, REMOVED, gate)
        bias_t = jnp.where(allowed, 0.0, NEG_INF)
        bias_t = jnp.concatenate([bias_t, jnp.zeros((LANES - n_blocks, tile), F32)], axis=0)
        bias = jnp.transpose(bias_t).astype(BF16)
        q_ext.append(jnp.concatenate([q, bias], axis=1))

    def keys(hd, j, rows=tile):
        at = pl.ds(pl.multiple_of(j * tile, tile), rows)
        return jnp.concatenate([k_ref[at, head_cols(hd)], onehot_ref[at, :]], axis=1)

    def values(hd, j):
        at = pl.ds(pl.multiple_of(j * tile, tile), tile)
        return jnp.concatenate([v_ref[at, head_cols(hd)], ones_ref[...]], axis=1)

    def scores(hd, i, k_tile):
        return lax.dot_general(q_ext[hd][i * sub:(i + 1) * sub], k_tile, contract_last,
                               preferred_element_type=F32)

    def weighted_values(hd, i, v_tile):
        return jnp.dot(p_ref[hd, i * sub:(i + 1) * sub, :], v_tile, preferred_element_type=F32)

    for hd, i in chains:
        rows = slice(i * sub, (i + 1) * sub)
        seen = (i + 1) * sub
        r = i * sub + lax.broadcasted_iota(jnp.int32, (sub, seen), 0)
        c = lax.broadcasted_iota(jnp.int32, (sub, seen), 1)
        s = jnp.where(c <= r, scores(hd, i, keys(hd, qi, seen)), NEG_INF)
        m0 = jnp.max(s, axis=1, keepdims=True)
        p_ref[hd, rows, :seen] = jnp.exp2(s - m0).astype(BF16)
        if seen < tile:
            p_ref[hd, rows, seen:] = jnp.zeros((sub, tile - seen), BF16)
        m_ref[hd, rows, :] = m0
        alpha_ref[hd, rows, :] = jnp.ones((sub, 1), F32)
        acc_ref[hd, rows, :] = jnp.zeros((sub, 2 * HEAD_DIM), F32)

    def accumulate(hd, i, v_tile):
        rows = slice(i * sub, (i + 1) * sub)
        return alpha_ref[hd, rows, :] * acc_ref[hd, rows, :] + weighted_values(hd, i, v_tile)

    def trip(j):
        j_prev = jnp.where(j == 0, qi, j - 1)
        k_tiles = [keys(hd, j) for hd in range(heads)]
        v_tiles = [values(hd, j_prev) for hd in range(heads)]
        for hd, i in chains:
            rows = slice(i * sub, (i + 1) * sub)
            acc_ref[hd, rows, :] = accumulate(hd, i, v_tiles[hd])
            s = scores(hd, i, k_tiles[hd])
            m = m_ref[hd, rows, :]
            m_new = jnp.maximum(m, jnp.max(s, axis=1, keepdims=True))
            p_ref[hd, rows, :] = jnp.exp2(s - m_new).astype(BF16)
            alpha_ref[hd, rows, :] = jnp.exp2(m - m_new)
            m_ref[hd, rows, :] = m_new

    odd = qi % 2
    pl.when(odd == 1)(functools.partial(trip, 0))

    def pair(t, carry):
        trip(odd + 2 * t)
        trip(odd + 2 * t + 1)
        return carry

    lax.fori_loop(0, qi // 2, pair, 0)
    j_last = jnp.where(qi == 0, qi, qi - 1)
    for n, (hd, i) in enumerate(chains):
        acc = accumulate(hd, i, values(hd, j_last))
        o_ref[i * sub:(i + 1) * sub, head_cols(hd)] = (
            acc[:, :HEAD_DIM] / acc[:, HEAD_DIM:HEAD_DIM + 1]).astype(o_ref.dtype)
        side_work(n, len(chains))


def _stage_plan(cols_used, col_start, out_shape, steps):
    rows_out, cols_out = out_shape
    for col_blocks in (1, 2, 4):
        row_blocks = steps // col_blocks
        rb = rows_out // row_blocks
        if steps % col_blocks or rows_out % row_blocks or rb % BF16_ROWS:
            continue
        cb_in = cols_used if col_blocks == 1 else cols_out // col_blocks
        if col_blocks > 1 and (cols_used != cols_out or cb_in % LANES):
            continue
        if col_start % cb_in:
            continue
        return rb, col_blocks, cb_in
    raise ValueError(f"no staging plan for {cols_used} columns at {col_start} -> {out_shape} in {steps} steps")


def _stage_pieces(step, src_ref, dst_ref, *, rows, rows_out, col_blocks):
    rb, cb_in = src_ref.shape
    cb_out = dst_ref.shape[1]
    row0 = (step // col_blocks) * rb

    def piece(r0, nr, c0, nc):
        def run():
            x = src_ref[r0:r0 + nr, c0:c0 + nc]
            if rows_out > rows:
                row = row0 + r0 + lax.broadcasted_iota(jnp.int32, x.shape, 0)
                x = jnp.where(row < rows, x, 0.0)
            dst_ref[r0:r0 + nr, c0:c0 + nc] = x.astype(dst_ref.dtype)
            if c0 + nc == cb_in and cb_out > cb_in:
                dst_ref[r0:r0 + nr, cb_in:] = jnp.zeros((nr, cb_out - cb_in), dst_ref.dtype)
        return run

    if (rb // 2) % BF16_ROWS == 0:
        return [piece(0, rb // 2, 0, cb_in), piece(rb // 2, rb // 2, 0, cb_in)]
    assert (cb_in // 2) % LANES == 0
    return [piece(0, rb, 0, cb_in // 2), piece(0, rb, cb_in // 2, cb_in // 2)]


def _moba(z, *, batch, seq, stage=(), tile=1024, n_sub=4, heads=2):
    t = z.shape[0]
    n_blocks = seq // MOBA_BLOCK
    tiles = seq // tile
    groups = MOBA_HEADS // heads
    width = heads * HEAD_DIM
    steps = batch * groups * tiles

    def step_of(b, g, i):
        return (b * groups + g) * tiles + i

    stage_in, stage_out, stage_shapes, stage_args = [], [], [], []
    for w, out_shape, col_start in stage:
        cols_used = min(w.shape[1] - col_start, out_shape[1])
        rb, col_blocks, cb_in = _stage_plan(cols_used, col_start, out_shape, steps)
        cb_out = out_shape[1] // col_blocks
        last = (w.shape[0] - 1) // rb
        stage_in.append(pl.BlockSpec(
            (rb, cb_in), lambda b, g, i, cbs=col_blocks, last=last, c0=col_start // cb_in:
            (jnp.minimum(step_of(b, g, i) // cbs, last), c0 + step_of(b, g, i) % cbs)))
        stage_out.append(pl.BlockSpec(
            (rb, cb_out), lambda b, g, i, cbs=col_blocks: (step_of(b, g, i) // cbs, step_of(b, g, i) % cbs)))
        stage_shapes.append(jax.ShapeDtypeStruct(out_shape, BF16))
        stage_args.append(dict(rows=w.shape[0], rows_out=out_shape[0], col_blocks=col_blocks))

    kern = functools.partial(_moba_kernel, tile=tile, n_blocks=n_blocks, n_sub=n_sub, heads=heads,
                             stage_args=tuple(stage_args), groups=groups, tiles=tiles)
    outs = pl.pallas_call(
        kern,
        grid=(batch, groups, tiles),
        in_specs=[pl.BlockSpec((tile, width), lambda b, g, i: (b * tiles + i, g)),
                  pl.BlockSpec((seq, width), lambda b, g, i: (b, groups + g)),
                  pl.BlockSpec((seq, width), lambda b, g, i: (b, 2 * groups + g))] + stage_in,
        out_specs=[pl.BlockSpec((tile, width), lambda b, g, i: (b * tiles + i, g))] + stage_out,
        out_shape=[jax.ShapeDtypeStruct((t, MOBA_WIDTH), BF16)] + stage_shapes,
        scratch_shapes=[pltpu.VMEM((seq, LANES), BF16),
                        pltpu.VMEM((tile, LANES), BF16),
                        pltpu.VMEM((heads, LANES, HEAD_DIM), F32),
                        pltpu.VMEM((heads, tile, tile), BF16),
                        pltpu.VMEM((heads, tile, 1), F32),
                        pltpu.VMEM((heads, tile, 1), F32),
                        pltpu.VMEM((heads, tile, 2 * HEAD_DIM), F32)],
        compiler_params=_cparams(("arbitrary", "arbitrary", "arbitrary")),
        name="moba_attention",
    )(z, z, z, *[item[0] for item in stage])
    return outs[0] if not stage else tuple(outs)


def _sgu_kernel(u_ref, v_ref, g_ref, b_ref, w_ref, bs_ref, o_ref, *, rows):
    v = v_ref[...].astype(F32)
    mu = jnp.mean(v, axis=-1, keepdims=True)
    var = jnp.mean(jnp.square(v - mu), axis=-1, keepdims=True)
    vln = ((v - mu) * lax.rsqrt(var + LN_EPS) * g_ref[...] + b_ref[...]).astype(BF16)
    t_idx = lax.broadcasted_iota(jnp.int32, (SGU_CHUNK, SGU_CHUNK), 0)
    s_idx = lax.broadcasted_iota(jnp.int32, (SGU_CHUNK, SGU_CHUNK), 1)
    causal = s_idx <= t_idx
    for g in range(SGU_GROUPS):
        w = jnp.where(causal, w_ref[g], 0.0).astype(BF16)
        bias = bs_ref[:, g:g + 1]
        cols = slice(g * LANES, (g + 1) * LANES)
        for ch in range(rows // SGU_CHUNK):
            rws = slice(ch * SGU_CHUNK, (ch + 1) * SGU_CHUNK)
            mixed = jnp.dot(w, vln[rws, cols], preferred_element_type=F32) + bias
            o_ref[rws, cols] = (u_ref[rws, cols].astype(F32) * mixed).astype(o_ref.dtype)


def _sgu(z, ln_g, ln_b, w_s, b_s_t, *, width, u_col, rows=512):
    t = z.shape[0]
    u_blk = u_col // width
    kern = functools.partial(_sgu_kernel, rows=rows)
    return pl.pallas_call(
        kern,
        grid=(t // rows,),
        in_specs=[pl.BlockSpec((rows, width), lambda i: (i, u_blk)),
                  pl.BlockSpec((rows, width), lambda i: (i, u_blk + 1)),
                  pl.BlockSpec((1, width), lambda i: (0, 0)),
                  pl.BlockSpec((1, width), lambda i: (0, 0)),
                  pl.BlockSpec((SGU_GROUPS, SGU_CHUNK, SGU_CHUNK), lambda i: (0, 0, 0)),
                  pl.BlockSpec((SGU_CHUNK, SGU_GROUPS), lambda i: (0, 0))],
        out_specs=pl.BlockSpec((rows, width), lambda i: (i, 0)),
        out_shape=jax.ShapeDtypeStruct((t, width), BF16),
        compiler_params=_cparams(("arbitrary",)),
        name="spatial_gating",
    )(z, z, ln_g.reshape(1, width), ln_b.reshape(1, width), w_s, b_s_t)


def _merge_kernel(oa_ref, ob_ref, pa_ref, pb_ref, ga_ref, gb_ref, o_ref):
    oa, ob = oa_ref[...], ob_ref[...]
    for c in range(o_ref.shape[1] // PROJ_COLS):
        cols = slice(c * PROJ_COLS, (c + 1) * PROJ_COLS)
        a = jnp.dot(oa, pa_ref[:, cols], preferred_element_type=F32)
        b = jnp.dot(ob, pb_ref[:, cols], preferred_element_type=F32)
        o_ref[:, cols] = (ga_ref[:, cols].astype(F32) * a + gb_ref[:, cols].astype(F32) * b).astype(o_ref.dtype)


def _merge(o_a, o_b, p_a, p_b, z, *, gate_col, bm=1024, bn=1024):
    t = o_a.shape[0]
    d = p_a.shape[1]
    ga_blk = gate_col // bn
    gb_blk = (gate_col + d) // bn
    return pl.pallas_call(
        _merge_kernel,
        grid=(t // bm, d // bn),
        in_specs=[pl.BlockSpec((bm, o_a.shape[1]), lambda m, n: (m, 0)),
                  pl.BlockSpec((bm, o_b.shape[1]), lambda m, n: (m, 0)),
                  pl.BlockSpec((p_a.shape[0], bn), lambda m, n: (0, n)),
                  pl.BlockSpec((p_b.shape[0], bn), lambda m, n: (0, n)),
                  pl.BlockSpec((bm, bn), lambda m, n: (m, ga_blk + n)),
                  pl.BlockSpec((bm, bn), lambda m, n: (m, gb_blk + n))],
        out_specs=pl.BlockSpec((bm, bn), lambda m, n: (m, n)),
        out_shape=jax.ShapeDtypeStruct((t, d), BF16),
        compiler_params=_cparams(("arbitrary", "arbitrary")),
        name="branch_merge",
    )(o_a, o_b, p_a, p_b, z, z)


def _outproj_kernel(a_ref, w_ref, x_ref, o_ref):
    o_ref[...] = x_ref[...] + jnp.dot(a_ref[...], w_ref[...], preferred_element_type=F32)


def _outproj(a, w, x, *, bm=1024, bn=1024):
    t, k = a.shape
    d = w.shape[1]
    return pl.pallas_call(
        _outproj_kernel,
        grid=(t // bm, d // bn),
        in_specs=[pl.BlockSpec((bm, k), lambda m, n: (m, 0)),
                  pl.BlockSpec((k, bn), lambda m, n: (0, n)),
                  pl.BlockSpec((bm, bn), lambda m, n: (m, n))],
        out_specs=pl.BlockSpec((bm, bn), lambda m, n: (m, n)),
        out_shape=jax.ShapeDtypeStruct((t, d), F32),
        compiler_params=_cparams(("arbitrary", "arbitrary")),
        name="out_proj",
    )(a, w, x)


def _memkv_kernel(mem_ref, g_ref, w_ref, o_ref):
    mem_n = _rmsnorm_rows(mem_ref[...], g_ref[...]).astype(BF16)
    o_ref[...] = jnp.dot(mem_n, w_ref[...], preferred_element_type=F32).astype(o_ref.dtype)


def _memkv(mem, g, w, *, n_mem):
    t, d = mem.shape
    cols = w.shape[1]
    return pl.pallas_call(
        _memkv_kernel,
        grid=(t // n_mem,),
        in_specs=[pl.BlockSpec((n_mem, d), lambda i: (i, 0)),
                  pl.BlockSpec((1, d), lambda i: (0, 0)),
                  pl.BlockSpec((d, cols), lambda i: (0, 0))],
        out_specs=pl.BlockSpec((n_mem, cols), lambda i: (i, 0)),
        out_shape=jax.ShapeDtypeStruct((t, cols), BF16),
        compiler_params=_cparams(("arbitrary",)),
        name="mem_kv",
    )(mem, g.reshape(1, d), w)


def _xattn_kernel(x_ref, gx_ref, wq_ref, kv_ref, wo_ref, gf_ref, x_out_ref, h_out_ref, *, chain_rows):
    chains = [slice(c * chain_rows, (c + 1) * chain_rows) for c in range(x_ref.shape[0] // chain_rows)]

    def attend(q):
        outs = []
        for hh in range(XATTN_HEADS):
            cols = slice(hh * HEAD_DIM, (hh + 1) * HEAD_DIM)
            k = kv_ref[:, hh * HEAD_DIM:(hh + 1) * HEAD_DIM]
            v = kv_ref[:, XATTN_WIDTH + hh * HEAD_DIM:XATTN_WIDTH + (hh + 1) * HEAD_DIM]
            s = lax.dot_general(q[:, cols], k, (((1,), (1,)), ((), ())), preferred_element_type=F32)
            s = s - jnp.max(s, axis=-1, keepdims=True)
            p = jnp.exp(s)
            p = p / jnp.sum(p, axis=-1, keepdims=True)
            outs.append(jnp.dot(p.astype(BF16), v, preferred_element_type=F32).astype(BF16))
        return jnp.concatenate(outs, axis=1)

    hs = [_rmsnorm_rows(x_ref[rows, :], gx_ref[...]).astype(BF16) for rows in chains]
    qs = [(jnp.dot(h, wq_ref[...], preferred_element_type=F32) * (HEAD_DIM ** -0.5)).astype(BF16) for h in hs]
    os = [attend(q) for q in qs]
    for rows, o in zip(chains, os):
        x_out_ref[rows, :] = x_ref[rows, :] + jnp.dot(o, wo_ref[...], preferred_element_type=F32)
    for rows in chains:
        h_out_ref[rows, :] = _rmsnorm_rows(x_out_ref[rows, :], gf_ref[...]).astype(h_out_ref.dtype)


def _xattn(x, g_x, w_q, kv, w_o, g_ffn, *, seq, n_mem, bm=512, chain_rows=256):
    t, d = x.shape
    per_batch = seq // bm
    resident = pl.Buffered(1)
    return pl.pallas_call(
        functools.partial(_xattn_kernel, chain_rows=chain_rows),
        grid=(t // bm,),
        in_specs=[pl.BlockSpec((bm, d), lambda i: (i, 0)),
                  pl.BlockSpec((1, d), lambda i: (0, 0)),
                  pl.BlockSpec(w_q.shape, lambda i: (0, 0), pipeline_mode=resident),
                  pl.BlockSpec((n_mem, kv.shape[1]), lambda i: (i // per_batch, 0)),
                  pl.BlockSpec(w_o.shape, lambda i: (0, 0), pipeline_mode=resident),
                  pl.BlockSpec((1, d), lambda i: (0, 0))],
        out_specs=[pl.BlockSpec((bm, d), lambda i: (i, 0)),
                   pl.BlockSpec((bm, d), lambda i: (i, 0))],
        out_shape=[jax.ShapeDtypeStruct((t, d), F32),
                   jax.ShapeDtypeStruct((t, d), BF16)],
        compiler_params=_cparams(("arbitrary",)),
        name="mem_cross_attention",
    )(x, g_x.reshape(1, d), w_q, kv, w_o, g_ffn.reshape(1, d))


def _ffn_up_kernel(h_ref, wg_ref, wu_ref, o_ref, *, last_cols):
    n = pl.program_id(1)
    bn = o_ref.shape[1]

    def swiglu(cols):
        h = h_ref[...]
        g = jnp.dot(h, wg_ref[:, :cols], preferred_element_type=F32)
        u = jnp.dot(h, wu_ref[:, :cols], preferred_element_type=F32)
        o_ref[:, :cols] = (g * _sigmoid(g) * u).astype(o_ref.dtype)
        if cols < bn:
            o_ref[:, cols:] = jnp.zeros((o_ref.shape[0], bn - cols), o_ref.dtype)

    last = pl.num_programs(1) - 1
    pl.when(n < last)(functools.partial(swiglu, bn))
    pl.when(n == last)(functools.partial(swiglu, last_cols))


def _ffn_up(h, w_g, w_u, *, d_ff, bm=1024, bn=512):
    t, d = h.shape
    f = w_g.shape[1]
    return pl.pallas_call(
        functools.partial(_ffn_up_kernel, last_cols=d_ff - (f // bn - 1) * bn),
        grid=(t // bm, f // bn),
        in_specs=[pl.BlockSpec((bm, d), lambda m, n: (m, 0)),
                  pl.BlockSpec((d, bn), lambda m, n: (0, n)),
                  pl.BlockSpec((d, bn), lambda m, n: (0, n))],
        out_specs=pl.BlockSpec((bm, bn), lambda m, n: (m, n)),
        out_shape=jax.ShapeDtypeStruct((t, f), BF16),
        compiler_params=_cparams(("arbitrary", "arbitrary")),
        name="ffn_up",
    )(h, w_g, w_u)


def _ffn_down_kernel(a_ref, w_ref, x_ref, g_ref, o_ref, *, final_norm, last_rows):
    k = pl.program_id(1)
    last = pl.num_programs(1) - 1
    bm, d = o_ref.shape

    @pl.when(k == 0)
    def _residual():
        o_ref[...] = x_ref[...]

    def accumulate(depth):
        a = a_ref[:, :depth]
        for c in range(d // DOWN_COLS):
            cols = slice(c * DOWN_COLS, (c + 1) * DOWN_COLS)
            o_ref[:, cols] += jnp.dot(a, w_ref[:depth, cols], preferred_element_type=F32)

    pl.when(k < last)(functools.partial(accumulate, a_ref.shape[1]))
    pl.when(k == last)(functools.partial(accumulate, last_rows))

    if final_norm:
        @pl.when(k == last)
        def _final_norm():
            for r in range(bm // NORM_ROWS):
                rows = slice(r * NORM_ROWS, (r + 1) * NORM_ROWS)
                o_ref[rows, :] = _rmsnorm_rows(o_ref[rows, :], g_ref[...])


def _ffn_down(a, w, x, g, *, final_norm, d_ff, bm=512, bk=1024):
    t, f = a.shape
    d = w.shape[1]
    return pl.pallas_call(
        functools.partial(_ffn_down_kernel, final_norm=final_norm, last_rows=d_ff - (f // bk - 1) * bk),
        grid=(t // bm, f // bk),
        in_specs=[pl.BlockSpec((bm, bk), lambda m, k: (m, k)),
                  pl.BlockSpec((bk, d), lambda m, k: (k, 0)),
                  pl.BlockSpec((bm, d), lambda m, k: (m, 0)),
                  pl.BlockSpec((1, d), lambda m, k: (0, 0))],
        out_specs=pl.BlockSpec((bm, d), lambda m, k: (m, 0)),
        out_shape=jax.ShapeDtypeStruct((t, d), F32),
        compiler_params=_cparams(("arbitrary", "arbitrary")),
        name="ffn_down",
    )(a, w, x, g.reshape(1, d))


@jax.jit
def kernel(x, mem, norm_mix_g, w_in, sgu_ln_g, sgu_ln_b, w_sgu, b_sgu, w_branch_a, w_branch_b, w_out, norm_xattn_g, norm_mem_g, w_xq, w_xkv, w_xo, norm_ffn_g, w_ff_gate, w_ff_up, w_ff_down, norm_final_g):
    batch, seq, d = x.shape
    n_mem = mem.shape[1]
    depth = w_in.shape[0]
    sgu_width = sgu_ln_g.shape[1]
    d_ff = w_ff_gate.shape[2]
    ff_pad = (-d_ff) % FF_PAD

    xt = x.reshape(batch * seq, d)
    memt = mem.reshape(batch * n_mem, d)
    cos, sin = _rope_table(seq)

    for l in range(depth):
        h = _rmsnorm(xt, norm_mix_g[l], BF16)
        qkv_cols = 3 * MOBA_WIDTH
        blocks = MOBA_WIDTH // INPROJ_COLS
        z_qkv = _inproj(h, w_in[l][:, :qkv_cols].astype(BF16), (cos, sin), seq=seq, name="in_proj_qkv",
                        segments=(("rope_q", blocks), ("rope", blocks), ("plain", blocks)),
                        bm=INPROJ_ROWS, bn=INPROJ_COLS)
        f_pad = d_ff + ff_pad
        rest_cols = w_in.shape[2] - qkv_cols
        o_a, w_rest, w_g, w_u, w_d, w_pa, w_pb, w_o, w_q, w_kv, w_xout = _moba(
            z_qkv, batch=batch, seq=seq, stage=(
                (w_in[l], (d, rest_cols), qkv_cols),
                (w_ff_gate[l], (d, f_pad), 0), (w_ff_up[l], (d, f_pad), 0), (w_ff_down[l], (f_pad, d), 0),
                (w_branch_a[l], w_branch_a[l].shape, 0), (w_branch_b[l], w_branch_b[l].shape, 0),
                (w_out[l], w_out[l].shape, 0), (w_xq[l], w_xq[l].shape, 0), (w_xkv[l], w_xkv[l].shape, 0),
                (w_xo[l], w_xo[l].shape, 0)))
        z = _inproj(h, w_rest, (), seq=seq, name="in_proj_rest", bm=INPROJ_ROWS, bn=INPROJ_COLS,
                    segments=(("gelu", 2 * sgu_width // INPROJ_COLS), ("sigmoid", 2 * d // INPROJ_COLS)))
        o_b = _sgu(z, sgu_ln_g[l], sgu_ln_b[l], w_sgu[l], jnp.transpose(b_sgu[l]), width=sgu_width, u_col=0)
        merged = _merge(o_a, o_b, w_pa, w_pb, z, gate_col=2 * sgu_width)
        xt = _outproj(merged, w_o, xt)

        kv = _memkv(memt, norm_mem_g[l], w_kv, n_mem=n_mem)
        xt, h = _xattn(xt, norm_xattn_g[l], w_q, kv, w_xout, norm_ffn_g[l], seq=seq, n_mem=n_mem)

        a = _ffn_up(h, w_g, w_u, d_ff=d_ff)
        xt = _ffn_down(a, w_d, xt, norm_final_g, final_norm=(l == depth - 1), d_ff=d_ff)
    return xt.reshape(batch, seq, d)
```

```python
import functools
import math

import jax
import jax.numpy as jnp
from jax import lax
from jax.experimental import pallas as pl
from jax.experimental.pallas import tpu as pltpu

F32 = jnp.float32
BF16 = jnp.bfloat16

HEAD_DIM = 128
MOBA_HEADS = 16
MOBA_WIDTH = MOBA_HEADS * HEAD_DIM
MOBA_BLOCK = 256
MOBA_TOPK = 3
SGU_CHUNK = 128
SGU_GROUPS = 16
XATTN_HEADS = 4
XATTN_WIDTH = XATTN_HEADS * HEAD_DIM
RMS_EPS = 1e-6
LN_EPS = 1e-5
ROPE_THETA = 10000.0
NEG_INF = -1e30
REMOVED = -3e38
Q_SCALE = HEAD_DIM ** -0.5 * math.log2(math.e)

LANES = 128
BF16_ROWS = 16
FF_PAD = 512
INPROJ_ROWS = 1024
INPROJ_COLS = 1024
PROJ_COLS = 256
DOWN_COLS = 1024
NORM_ROWS = 64
VMEM_LIMIT = 60 * 1024 * 1024


def _cparams(sem):
    return pltpu.CompilerParams(dimension_semantics=sem, vmem_limit_bytes=VMEM_LIMIT)


def _rmsnorm_rows(x, g):
    ms = jnp.mean(x * x, axis=-1, keepdims=True)
    return x * lax.rsqrt(ms + RMS_EPS) * g


def _rmsnorm_kernel(x_ref, g_ref, o_ref):
    o_ref[...] = _rmsnorm_rows(x_ref[...], g_ref[...]).astype(o_ref.dtype)


def _rmsnorm(x, g, out_dtype, bm=512):
    t, d = x.shape
    return pl.pallas_call(
        _rmsnorm_kernel,
        grid=(t // bm,),
        in_specs=[pl.BlockSpec((bm, d), lambda i: (i, 0)),
                  pl.BlockSpec((1, d), lambda i: (0, 0))],
        out_specs=pl.BlockSpec((bm, d), lambda i: (i, 0)),
        out_shape=jax.ShapeDtypeStruct((t, d), out_dtype),
        compiler_params=_cparams(("arbitrary",)),
        name="rmsnorm",
    )(x, g.reshape(1, d))


def _rope_kernel(freq_ref, cos_ref, sin_ref, *, rows):
    i = pl.program_id(0)
    pos = (i * rows + lax.broadcasted_iota(jnp.int32, (rows, LANES), 0)).astype(F32)
    lane = lax.broadcasted_iota(jnp.int32, (rows, LANES), 1)
    ang = pos * freq_ref[...]
    cos_ref[...] = jnp.cos(ang)
    s = jnp.sin(ang)
    sin_ref[...] = jnp.where(lane < HEAD_DIM // 2, -s, s)


def _rope_table(seq, rows=1024):
    half = HEAD_DIM // 2
    inv_freq = jnp.power(ROPE_THETA, -(jnp.arange(half, dtype=F32) * 2.0 / HEAD_DIM))
    freq = jnp.concatenate([inv_freq, inv_freq]).reshape(1, HEAD_DIM)
    return pl.pallas_call(
        functools.partial(_rope_kernel, rows=rows),
        grid=(seq // rows,),
        in_specs=[pl.BlockSpec((1, HEAD_DIM), lambda i: (0, 0))],
        out_specs=[pl.BlockSpec((rows, HEAD_DIM), lambda i: (i, 0))] * 2,
        out_shape=[jax.ShapeDtypeStruct((seq, HEAD_DIM), F32)] * 2,
        compiler_params=_cparams(("arbitrary",)),
        name="rope_table",
    )(freq)


def _gelu_tanh(x):
    c = math.sqrt(2.0 / math.pi)
    half = 0.5 * x
    return half + half * jnp.tanh(x * (c + (c * 0.044715) * (x * x)))


def _silu(x):
    half = 0.5 * x
    return half + half * jnp.tanh(half)


def _inproj_kernel(h_ref, w_ref, *refs, bn, segments):
    *rope_refs, z_ref = refs
    n = pl.program_id(1)

    def project(epilogue):
        h = h_ref[...]
        for c in range(bn // PROJ_COLS):
            cols = slice(c * PROJ_COLS, (c + 1) * PROJ_COLS)
            blk = jnp.dot(h, w_ref[:, cols], preferred_element_type=F32)
            z_ref[:, cols] = epilogue(blk).astype(z_ref.dtype)

    def rotary(scale):
        def rotate(blk):
            cos, sin = rope_refs[0][...], rope_refs[1][...]
            heads = []
            for hh in range(PROJ_COLS // HEAD_DIM):
                x = blk[:, hh * HEAD_DIM:(hh + 1) * HEAD_DIM]
                heads.append(x * cos + pltpu.roll(x, HEAD_DIM // 2, axis=1) * sin)
            out = jnp.concatenate(heads, axis=1)
            return out if scale is None else out * scale
        return rotate

    epilogues = {
        "rope_q": rotary(Q_SCALE),
        "rope": rotary(None),
        "plain": lambda blk: blk,
        "gelu": _gelu_tanh,
        "sigmoid": lambda blk: 0.5 + 0.5 * jnp.tanh(0.5 * blk),
    }
    start = 0
    for kind, count in segments:
        pl.when((n >= start) & (n < start + count))(functools.partial(project, epilogues[kind]))
        start += count


def _inproj(h, w, rope, *, seq, segments, name, bm, bn):
    t, d = h.shape
    cols = w.shape[1]
    assert sum(count for _, count in segments) * bn == cols
    seq_blocks = seq // bm
    rope_spec = pl.BlockSpec((bm, HEAD_DIM), lambda m, n: (m % seq_blocks, 0))
    return pl.pallas_call(
        functools.partial(_inproj_kernel, bn=bn, segments=segments),
        grid=(t // bm, cols // bn),
        in_specs=[pl.BlockSpec((bm, d), lambda m, n: (m, 0)),
                  pl.BlockSpec((d, bn), lambda m, n: (0, n))] + [rope_spec] * len(rope),
        out_specs=pl.BlockSpec((bm, bn), lambda m, n: (m, n)),
        out_shape=jax.ShapeDtypeStruct((t, cols), BF16),
        compiler_params=_cparams(("arbitrary", "arbitrary")),
        name=name,
    )(h, w, *rope)


def _moba_kernel(q_ref, k_ref, v_ref, *refs, tile, n_blocks, n_sub, heads, stage_args, groups, tiles):
    n_stage = len(stage_args)
    stage_src, o_ref, stage_dst = refs[:n_stage], refs[n_stage], refs[n_stage + 1:2 * n_stage + 1]
    onehot_ref, ones_ref, kmean_ref, p_ref, m_ref, alpha_ref, acc_ref = refs[2 * n_stage + 1:]
    step = (pl.program_id(0) * groups + pl.program_id(1)) * tiles + pl.program_id(2)
    pieces = [p for src, dst, args in zip(stage_src, stage_dst, stage_args)
              for p in _stage_pieces(step, src, dst, **args)]

    def stage_weights(slot, n_slots):
        for p in pieces[slot::n_slots]:
            p()

    _moba_body(q_ref, k_ref, v_ref, o_ref, onehot_ref, ones_ref, kmean_ref, p_ref, m_ref, alpha_ref, acc_ref,
               tile=tile, n_blocks=n_blocks, n_sub=n_sub, heads=heads, side_work=stage_weights)


def _moba_body(q_ref, k_ref, v_ref, o_ref, onehot_ref, ones_ref, kmean_ref, p_ref, m_ref, alpha_ref, acc_ref,
               *, tile, n_blocks, n_sub, heads, side_work):
    qi = pl.program_id(2)
    blocks_per_tile = tile // MOBA_BLOCK
    contract_last = (((1,), (1,)), ((), ()))
    sub = tile // n_sub

    chains = [(hd, i) for hd in range(heads) for i in range(n_sub)]

    def head_cols(hd):
        return slice(hd * HEAD_DIM, (hd + 1) * HEAD_DIM)

    @pl.when(qi == 0)
    def _prepare_keys():
        kmean_ref[...] = jnp.zeros_like(kmean_ref)
        lane = lax.broadcasted_iota(jnp.int32, (MOBA_BLOCK, LANES), 1)
        ones_ref[...] = (lax.broadcasted_iota(jnp.int32, ones_ref.shape, 1) == 0).astype(BF16)
        for j in range(n_blocks):
            rows = pl.ds(j * MOBA_BLOCK, MOBA_BLOCK)
            onehot_ref[rows, :] = (lane == j).astype(BF16)
            for hd in range(heads):
                kb = k_ref[rows, head_cols(hd)]
                kmean_ref[hd, j:j + 1, :] = jnp.mean(kb.astype(F32), axis=0, keepdims=True)

    blk = lax.broadcasted_iota(jnp.int32, (n_blocks, tile), 0)
    col = lax.broadcasted_iota(jnp.int32, (n_blocks, tile), 1)
    own_block = qi * blocks_per_tile + col // MOBA_BLOCK
    q_ext = []
    for hd in range(heads):
        q = q_ref[:, head_cols(hd)]
        gate = lax.dot_general(kmean_ref[hd].astype(BF16), q, contract_last,
                               preferred_element_type=F32)[:n_blocks, :]
        gate = jnp.where(blk < own_block, gate, NEG_INF)
        allowed = blk == own_block
        for _ in range(MOBA_TOPK):
            best = jnp.max(gate, axis=0, keepdims=True)
            first = jnp.min(jnp.where(gate == best, blk, n_blocks), axis=0, keepdims=True)
            hit = blk == first
            allowed = allowed | (hit & (best > 0.5 * NEG_INF))
            gate = jnp.where(hit, REMOVED, gate)
        bias_t = jnp.where(allowed, 0.0, NEG_INF)
        bias_t = jnp.concatenate([bias_t, jnp.zeros((LANES - n_blocks, tile), F32)], axis=0)
        bias = jnp.transpose(bias_t).astype(BF16)
        q_ext.append(jnp.concatenate([q, bias], axis=1))

    def keys(hd, j, rows=tile):
        at = pl.ds(pl.multiple_of(j * tile, tile), rows)
        return jnp.concatenate([k_ref[at, head_cols(hd)], onehot_ref[at, :]], axis=1)

    def values(hd, j):
        at = pl.ds(pl.multiple_of(j * tile, tile), tile)
        return jnp.concatenate([v_ref[at, head_cols(hd)], ones_ref[...]], axis=1)

    def scores(hd, i, k_tile):
        return lax.dot_general(q_ext[hd][i * sub:(i + 1) * sub], k_tile, contract_last,
                               preferred_element_type=F32)

    def weighted_values(hd, i, v_tile):
        return jnp.dot(p_ref[hd, i * sub:(i + 1) * sub, :], v_tile, preferred_element_type=F32)

    for hd, i in chains:
        rows = slice(i * sub, (i + 1) * sub)
        seen = (i + 1) * sub
        r = i * sub + lax.broadcasted_iota(jnp.int32, (sub, seen), 0)
        c = lax.broadcasted_iota(jnp.int32, (sub, seen), 1)
        s = jnp.where(c <= r, scores(hd, i, keys(hd, qi, seen)), NEG_INF)
        m0 = jnp.max(s, axis=1, keepdims=True)
        p_ref[hd, rows, :seen] = jnp.exp2(s - m0).astype(BF16)
        if seen < tile:
            p_ref[hd, rows, seen:] = jnp.zeros((sub, tile - seen), BF16)
        m_ref[hd, rows, :] = m0
        alpha_ref[hd, rows, :] = jnp.ones((sub, 1), F32)
        acc_ref[hd, rows, :] = jnp.zeros((sub, 2 * HEAD_DIM), F32)

    def accumulate(hd, i, v_tile):
        rows = slice(i * sub, (i + 1) * sub)
        return alpha_ref[hd, rows, :] * acc_ref[hd, rows, :] + weighted_values(hd, i, v_tile)

    def trip(j):
        j_prev = jnp.where(j == 0, qi, j - 1)
        k_tiles = [keys(hd, j) for hd in range(heads)]
        v_tiles = [values(hd, j_prev) for hd in range(heads)]
        for hd, i in chains:
            rows = slice(i * sub, (i + 1) * sub)
            acc_ref[hd, rows, :] = accumulate(hd, i, v_tiles[hd])
            s = scores(hd, i, k_tiles[hd])
            m = m_ref[hd, rows, :]
            m_new = jnp.maximum(m, jnp.max(s, axis=1, keepdims=True))
            p_ref[hd, rows, :] = jnp.exp2(s - m_new).astype(BF16)
            alpha_ref[hd, rows, :] = jnp.exp2(m - m_new)
            m_ref[hd, rows, :] = m_new

    odd = qi % 2
    pl.when(odd == 1)(functools.partial(trip, 0))

    def pair(t, carry):
        trip(odd + 2 * t)
        trip(odd + 2 * t + 1)
        return carry

    lax.fori_loop(0, qi // 2, pair, 0)
    j_last = jnp.where(qi == 0, qi, qi - 1)
    for n, (hd, i) in enumerate(chains):
        acc = accumulate(hd, i, values(hd, j_last))
        o_ref[i * sub:(i + 1) * sub, head_cols(hd)] = (
            acc[:, :HEAD_DIM] / acc[:, HEAD_DIM:HEAD_DIM + 1]).astype(o_ref.dtype)
        side_work(n, len(chains))


def _stage_plan(cols_used, col_start, out_shape, steps):
    rows_out, cols_out = out_shape
    for col_blocks in (1, 2, 4):
        row_blocks = steps // col_blocks
        rb = rows_out // row_blocks
        if steps % col_blocks or rows_out % row_blocks or rb % BF16_ROWS:
            continue
        cb_in = cols_used if col_blocks == 1 else cols_out // col_blocks
        if col_blocks > 1 and (cols_used != cols_out or cb_in % LANES):
            continue
        if col_start % cb_in:
            continue
        return rb, col_blocks, cb_in
    raise ValueError(f"no staging plan for {cols_used} columns at {col_start} -> {out_shape} in {steps} steps")


def _stage_pieces(step, src_ref, dst_ref, *, rows, rows_out, col_blocks):
    rb, cb_in = src_ref.shape
    cb_out = dst_ref.shape[1]
    row0 = (step // col_blocks) * rb

    def piece(r0, nr, c0, nc):
        def run():
            x = src_ref[r0:r0 + nr, c0:c0 + nc]
            if rows_out > rows:
                row = row0 + r0 + lax.broadcasted_iota(jnp.int32, x.shape, 0)
                x = jnp.where(row < rows, x, 0.0)
            dst_ref[r0:r0 + nr, c0:c0 + nc] = x.astype(dst_ref.dtype)
            if c0 + nc == cb_in and cb_out > cb_in:
                dst_ref[r0:r0 + nr, cb_in:] = jnp.zeros((nr, cb_out - cb_in), dst_ref.dtype)
        return run

    if (rb // 2) % BF16_ROWS == 0:
        return [piece(0, rb // 2, 0, cb_in), piece(rb // 2, rb // 2, 0, cb_in)]
    assert (cb_in // 2) % LANES == 0
    return [piece(0, rb, 0, cb_in // 2), piece(0, rb, cb_in // 2, cb_in // 2)]


def _moba(z, *, batch, seq, stage=(), tile=1024, n_sub=4, heads=2):
    t = z.shape[0]
    n_blocks = seq // MOBA_BLOCK
    tiles = seq // tile
    groups = MOBA_HEADS // heads
    width = heads * HEAD_DIM
    steps = batch * groups * tiles

    def step_of(b, g, i):
        return (b * groups + g) * tiles + i

    stage_in, stage_out, stage_shapes, stage_args = [], [], [], []
    for w, out_shape, col_start in stage:
        cols_used = min(w.shape[1] - col_start, out_shape[1])
        rb, col_blocks, cb_in = _stage_plan(cols_used, col_start, out_shape, steps)
        cb_out = out_shape[1] // col_blocks
        last = (w.shape[0] - 1) // rb
        stage_in.append(pl.BlockSpec(
            (rb, cb_in), lambda b, g, i, cbs=col_blocks, last=last, c0=col_start // cb_in:
            (jnp.minimum(step_of(b, g, i) // cbs, last), c0 + step_of(b, g, i) % cbs)))
        stage_out.append(pl.BlockSpec(
            (rb, cb_out), lambda b, g, i, cbs=col_blocks: (step_of(b, g, i) // cbs, step_of(b, g, i) % cbs)))
        stage_shapes.append(jax.ShapeDtypeStruct(out_shape, BF16))
        stage_args.append(dict(rows=w.shape[0], rows_out=out_shape[0], col_blocks=col_blocks))

    kern = functools.partial(_moba_kernel, tile=tile, n_blocks=n_blocks, n_sub=n_sub, heads=heads,
                             stage_args=tuple(stage_args), groups=groups, tiles=tiles)
    outs = pl.pallas_call(
        kern,
        grid=(batch, groups, tiles),
        in_specs=[pl.BlockSpec((tile, width), lambda b, g, i: (b * tiles + i, g)),
                  pl.BlockSpec((seq, width), lambda b, g, i: (b, groups + g)),
                  pl.BlockSpec((seq, width), lambda b, g, i: (b, 2 * groups + g))] + stage_in,
        out_specs=[pl.BlockSpec((tile, width), lambda b, g, i: (b * tiles + i, g))] + stage_out,
        out_shape=[jax.ShapeDtypeStruct((t, MOBA_WIDTH), BF16)] + stage_shapes,
        scratch_shapes=[pltpu.VMEM((seq, LANES), BF16),
                        pltpu.VMEM((tile, LANES), BF16),
                        pltpu.VMEM((heads, LANES, HEAD_DIM), F32),
                        pltpu.VMEM((heads, tile, tile), BF16),
                        pltpu.VMEM((heads, tile, 1), F32),
                        pltpu.VMEM((heads, tile, 1), F32),
                        pltpu.VMEM((heads, tile, 2 * HEAD_DIM), F32)],
        compiler_params=_cparams(("arbitrary", "arbitrary", "arbitrary")),
        name="moba_attention",
    )(z, z, z, *[item[0] for item in stage])
    return outs[0] if not stage else tuple(outs)


def _sgu_kernel(u_ref, v_ref, g_ref, b_ref, w_ref, bs_ref, o_ref, *, rows):
    v = v_ref[...].astype(F32)
    mu = jnp.mean(v, axis=-1, keepdims=True)
    var = jnp.mean(jnp.square(v - mu), axis=-1, keepdims=True)
    vln = ((v - mu) * lax.rsqrt(var + LN_EPS) * g_ref[...] + b_ref[...]).astype(BF16)
    t_idx = lax.broadcasted_iota(jnp.int32, (SGU_CHUNK, SGU_CHUNK), 0)
    s_idx = lax.broadcasted_iota(jnp.int32, (SGU_CHUNK, SGU_CHUNK), 1)
    causal = s_idx <= t_idx
    for g in range(SGU_GROUPS):
        w = jnp.where(causal, w_ref[g], 0.0).astype(BF16)
        bias = bs_ref[:, g:g + 1]
        cols = slice(g * LANES, (g + 1) * LANES)
        for ch in range(rows // SGU_CHUNK):
            rws = slice(ch * SGU_CHUNK, (ch + 1) * SGU_CHUNK)
            mixed = jnp.dot(w, vln[rws, cols], preferred_element_type=F32) + bias
            o_ref[rws, cols] = (u_ref[rws, cols].astype(F32) * mixed).astype(o_ref.dtype)


def _sgu(z, ln_g, ln_b, w_s, b_s_t, *, width, u_col, rows=1024):
    t = z.shape[0]
    u_blk = u_col // width
    kern = functools.partial(_sgu_kernel, rows=rows)
    return pl.pallas_call(
        kern,
        grid=(t // rows,),
        in_specs=[pl.BlockSpec((rows, width), lambda i: (i, u_blk)),
                  pl.BlockSpec((rows, width), lambda i: (i, u_blk + 1)),
                  pl.BlockSpec((1, width), lambda i: (0, 0)),
                  pl.BlockSpec((1, width), lambda i: (0, 0)),
                  pl.BlockSpec((SGU_GROUPS, SGU_CHUNK, SGU_CHUNK), lambda i: (0, 0, 0)),
                  pl.BlockSpec((SGU_CHUNK, SGU_GROUPS), lambda i: (0, 0))],
        out_specs=pl.BlockSpec((rows, width), lambda i: (i, 0)),
        out_shape=jax.ShapeDtypeStruct((t, width), BF16),
        compiler_params=_cparams(("arbitrary",)),
        name="spatial_gating",
    )(z, z, ln_g.reshape(1, width), ln_b.reshape(1, width), w_s, b_s_t)


def _merge_kernel(oa_ref, ob_ref, pa_ref, pb_ref, ga_ref, gb_ref, o_ref):
    oa, ob = oa_ref[...], ob_ref[...]
    for c in range(o_ref.shape[1] // PROJ_COLS):
        cols = slice(c * PROJ_COLS, (c + 1) * PROJ_COLS)
        a = jnp.dot(oa, pa_ref[:, cols], preferred_element_type=F32)
        b = jnp.dot(ob, pb_ref[:, cols], preferred_element_type=F32)
        o_ref[:, cols] = (ga_ref[:, cols].astype(F32) * a + gb_ref[:, cols].astype(F32) * b).astype(o_ref.dtype)


def _merge(o_a, o_b, p_a, p_b, z, *, gate_col, bm=1024, bn=1024):
    t = o_a.shape[0]
    d = p_a.shape[1]
    ga_blk = gate_col // bn
    gb_blk = (gate_col + d) // bn
    return pl.pallas_call(
        _merge_kernel,
        grid=(t // bm, d // bn),
        in_specs=[pl.BlockSpec((bm, o_a.shape[1]), lambda m, n: (m, 0)),
                  pl.BlockSpec((bm, o_b.shape[1]), lambda m, n: (m, 0)),
                  pl.BlockSpec((p_a.shape[0], bn), lambda m, n: (0, n)),
                  pl.BlockSpec((p_b.shape[0], bn), lambda m, n: (0, n)),
                  pl.BlockSpec((bm, bn), lambda m, n: (m, ga_blk + n)),
                  pl.BlockSpec((bm, bn), lambda m, n: (m, gb_blk + n))],
        out_specs=pl.BlockSpec((bm, bn), lambda m, n: (m, n)),
        out_shape=jax.ShapeDtypeStruct((t, d), BF16),
        compiler_params=_cparams(("arbitrary", "arbitrary")),
        name="branch_merge",
    )(o_a, o_b, p_a, p_b, z, z)


def _outproj_kernel(a_ref, w_ref, x_ref, o_ref):
    o_ref[...] = x_ref[...] + jnp.dot(a_ref[...], w_ref[...], preferred_element_type=F32)


def _outproj(a, w, x, *, bm=1024, bn=1024):
    t, k = a.shape
    d = w.shape[1]
    return pl.pallas_call(
        _outproj_kernel,
        grid=(t // bm, d // bn),
        in_specs=[pl.BlockSpec((bm, k), lambda m, n: (m, 0)),
                  pl.BlockSpec((k, bn), lambda m, n: (0, n)),
                  pl.BlockSpec((bm, bn), lambda m, n: (m, n))],
        out_specs=pl.BlockSpec((bm, bn), lambda m, n: (m, n)),
        out_shape=jax.ShapeDtypeStruct((t, d), F32),
        compiler_params=_cparams(("arbitrary", "arbitrary")),
        name="out_proj",
    )(a, w, x)


def _memkv_kernel(mem_ref, g_ref, w_ref, o_ref):
    mem_n = _rmsnorm_rows(mem_ref[...], g_ref[...]).astype(BF16)
    o_ref[...] = jnp.dot(mem_n, w_ref[...], preferred_element_type=F32).astype(o_ref.dtype)


def _memkv(mem, g, w, *, n_mem):
    t, d = mem.shape
    cols = w.shape[1]
    return pl.pallas_call(
        _memkv_kernel,
        grid=(t // n_mem,),
        in_specs=[pl.BlockSpec((n_mem, d), lambda i: (i, 0)),
                  pl.BlockSpec((1, d), lambda i: (0, 0)),
                  pl.BlockSpec((d, cols), lambda i: (0, 0))],
        out_specs=pl.BlockSpec((n_mem, cols), lambda i: (i, 0)),
        out_shape=jax.ShapeDtypeStruct((t, cols), BF16),
        compiler_params=_cparams(("arbitrary",)),
        name="mem_kv",
    )(mem, g.reshape(1, d), w)


def _xattn_kernel(x_ref, gx_ref, wq_ref, kv_ref, wo_ref, gf_ref, x_out_ref, h_out_ref, *, chain_rows):
    chains = [slice(c * chain_rows, (c + 1) * chain_rows) for c in range(x_ref.shape[0] // chain_rows)]

    def attend(q):
        outs = []
        for hh in range(XATTN_HEADS):
            cols = slice(hh * HEAD_DIM, (hh + 1) * HEAD_DIM)
            k = kv_ref[:, hh * HEAD_DIM:(hh + 1) * HEAD_DIM]
            v = kv_ref[:, XATTN_WIDTH + hh * HEAD_DIM:XATTN_WIDTH + (hh + 1) * HEAD_DIM]
            s = lax.dot_general(q[:, cols], k, (((1,), (1,)), ((), ())), preferred_element_type=F32)
            s = s - jnp.max(s, axis=-1, keepdims=True)
            p = jnp.exp(s)
            p = p / jnp.sum(p, axis=-1, keepdims=True)
            outs.append(jnp.dot(p.astype(BF16), v, preferred_element_type=F32).astype(BF16))
        return jnp.concatenate(outs, axis=1)

    hs = [_rmsnorm_rows(x_ref[rows, :], gx_ref[...]).astype(BF16) for rows in chains]
    qs = [(jnp.dot(h, wq_ref[...], preferred_element_type=F32) * (HEAD_DIM ** -0.5)).astype(BF16) for h in hs]
    os = [attend(q) for q in qs]
    for rows, o in zip(chains, os):
        x_out_ref[rows, :] = x_ref[rows, :] + jnp.dot(o, wo_ref[...], preferred_element_type=F32)
    for rows in chains:
        h_out_ref[rows, :] = _rmsnorm_rows(x_out_ref[rows, :], gf_ref[...]).astype(h_out_ref.dtype)


def _xattn(x, g_x, w_q, kv, w_o, g_ffn, *, seq, n_mem, bm=512, chain_rows=256):
    t, d = x.shape
    per_batch = seq // bm
    resident = pl.Buffered(1)
    return pl.pallas_call(
        functools.partial(_xattn_kernel, chain_rows=chain_rows),
        grid=(t // bm,),
        in_specs=[pl.BlockSpec((bm, d), lambda i: (i, 0)),
                  pl.BlockSpec((1, d), lambda i: (0, 0)),
                  pl.BlockSpec(w_q.shape, lambda i: (0, 0), pipeline_mode=resident),
                  pl.BlockSpec((n_mem, kv.shape[1]), lambda i: (i // per_batch, 0)),
                  pl.BlockSpec(w_o.shape, lambda i: (0, 0), pipeline_mode=resident),
                  pl.BlockSpec((1, d), lambda i: (0, 0))],
        out_specs=[pl.BlockSpec((bm, d), lambda i: (i, 0)),
                   pl.BlockSpec((bm, d), lambda i: (i, 0))],
        out_shape=[jax.ShapeDtypeStruct((t, d), F32),
                   jax.ShapeDtypeStruct((t, d), BF16)],
        compiler_params=_cparams(("arbitrary",)),
        name="mem_cross_attention",
    )(x, g_x.reshape(1, d), w_q, kv, w_o, g_ffn.reshape(1, d))


def _ffn_up_kernel(h_ref, wg_ref, wu_ref, o_ref, *, last_cols):
    n = pl.program_id(1)
    bn = o_ref.shape[1]

    def swiglu(cols):
        h = h_ref[...]
        g = jnp.dot(h, wg_ref[:, :cols], preferred_element_type=F32)
        u = jnp.dot(h, wu_ref[:, :cols], preferred_element_type=F32)
        o_ref[:, :cols] = (_silu(g) * u).astype(o_ref.dtype)
        if cols < bn:
            o_ref[:, cols:] = jnp.zeros((o_ref.shape[0], bn - cols), o_ref.dtype)

    last = pl.num_programs(1) - 1
    pl.when(n < last)(functools.partial(swiglu, bn))
    pl.when(n == last)(functools.partial(swiglu, last_cols))


def _ffn_up(h, w_g, w_u, *, d_ff, bm=1024, bn=512):
    t, d = h.shape
    f = w_g.shape[1]
    return pl.pallas_call(
        functools.partial(_ffn_up_kernel, last_cols=d_ff - (f // bn - 1) * bn),
        grid=(t // bm, f // bn),
        in_specs=[pl.BlockSpec((bm, d), lambda m, n: (m, 0)),
                  pl.BlockSpec((d, bn), lambda m, n: (0, n)),
                  pl.BlockSpec((d, bn), lambda m, n: (0, n))],
        out_specs=pl.BlockSpec((bm, bn), lambda m, n: (m, n)),
        out_shape=jax.ShapeDtypeStruct((t, f), BF16),
        compiler_params=_cparams(("arbitrary", "arbitrary")),
        name="ffn_up",
    )(h, w_g, w_u)


def _ffn_down_kernel(a_ref, w_ref, x_ref, g_ref, o_ref, *, final_norm, last_rows):
    k = pl.program_id(1)
    last = pl.num_programs(1) - 1
    bm, d = o_ref.shape

    def accumulate(base_ref, depth):
        a = a_ref[:, :depth]
        for c in range(d // DOWN_COLS):
            cols = slice(c * DOWN_COLS, (c + 1) * DOWN_COLS)
            o_ref[:, cols] = base_ref[:, cols] + jnp.dot(a, w_ref[:depth, cols], preferred_element_type=F32)

    pl.when(k == 0)(functools.partial(accumulate, x_ref, a_ref.shape[1]))
    pl.when((k > 0) & (k < last))(functools.partial(accumulate, o_ref, a_ref.shape[1]))
    pl.when(k == last)(functools.partial(accumulate, o_ref, last_rows))

    if final_norm:
        @pl.when(k == last)
        def _final_norm():
            for r in range(bm // NORM_ROWS):
                rows = slice(r * NORM_ROWS, (r + 1) * NORM_ROWS)
                o_ref[rows, :] = _rmsnorm_rows(o_ref[rows, :], g_ref[...])


def _ffn_down(a, w, x, g, *, final_norm, d_ff, bm=512, bk=1024):
    t, f = a.shape
    d = w.shape[1]
    assert f // bk >= 2, "first and last contraction steps must differ"
    return pl.pallas_call(
        functools.partial(_ffn_down_kernel, final_norm=final_norm, last_rows=d_ff - (f // bk - 1) * bk),
        grid=(t // bm, f // bk),
        in_specs=[pl.BlockSpec((bm, bk), lambda m, k: (m, k)),
                  pl.BlockSpec((bk, d), lambda m, k: (k, 0)),
                  pl.BlockSpec((bm, d), lambda m, k: (m, 0)),
                  pl.BlockSpec((1, d), lambda m, k: (0, 0))],
        out_specs=pl.BlockSpec((bm, d), lambda m, k: (m, 0)),
        out_shape=jax.ShapeDtypeStruct((t, d), F32),
        compiler_params=_cparams(("arbitrary", "arbitrary")),
        name="ffn_down",
    )(a, w, x, g.reshape(1, d))


@jax.jit
def kernel(x, mem, norm_mix_g, w_in, sgu_ln_g, sgu_ln_b, w_sgu, b_sgu, w_branch_a, w_branch_b, w_out, norm_xattn_g, norm_mem_g, w_xq, w_xkv, w_xo, norm_ffn_g, w_ff_gate, w_ff_up, w_ff_down, norm_final_g):
    batch, seq, d = x.shape
    n_mem = mem.shape[1]
    depth = w_in.shape[0]
    sgu_width = sgu_ln_g.shape[1]
    d_ff = w_ff_gate.shape[2]
    ff_pad = (-d_ff) % FF_PAD

    xt = x.reshape(batch * seq, d)
    memt = mem.reshape(batch * n_mem, d)
    cos, sin = _rope_table(seq)

    for l in range(depth):
        h = _rmsnorm(xt, norm_mix_g[l], BF16)
        qkv_cols = 3 * MOBA_WIDTH
        blocks = MOBA_WIDTH // INPROJ_COLS
        z_qkv = _inproj(h, w_in[l][:, :qkv_cols].astype(BF16), (cos, sin), seq=seq, name="in_proj_qkv",
                        segments=(("rope_q", blocks), ("rope", blocks), ("plain", blocks)),
                        bm=INPROJ_ROWS, bn=INPROJ_COLS)
        f_pad = d_ff + ff_pad
        rest_cols = w_in.shape[2] - qkv_cols
        o_a, w_rest, w_g, w_u, w_d, w_pa, w_pb, w_o, w_q, w_kv, w_xout = _moba(
            z_qkv, batch=batch, seq=seq, stage=(
                (w_in[l], (d, rest_cols), qkv_cols),
                (w_ff_gate[l], (d, f_pad), 0), (w_ff_up[l], (d, f_pad), 0), (w_ff_down[l], (f_pad, d), 0),
                (w_branch_a[l], w_branch_a[l].shape, 0), (w_branch_b[l], w_branch_b[l].shape, 0),
                (w_out[l], w_out[l].shape, 0), (w_xq[l], w_xq[l].shape, 0), (w_xkv[l], w_xkv[l].shape, 0),
                (w_xo[l], w_xo[l].shape, 0)))
        z = _inproj(h, w_rest, (), seq=seq, name="in_proj_rest", bm=INPROJ_ROWS, bn=INPROJ_COLS,
                    segments=(("gelu", 2 * sgu_width // INPROJ_COLS), ("sigmoid", 2 * d // INPROJ_COLS)))
        o_b = _sgu(z, sgu_ln_g[l], sgu_ln_b[l], w_sgu[l], jnp.transpose(b_sgu[l]), width=sgu_width, u_col=0)
        merged = _merge(o_a, o_b, w_pa, w_pb, z, gate_col=2 * sgu_width)
        xt = _outproj(merged, w_o, xt)

        kv = _memkv(memt, norm_mem_g[l], w_kv, n_mem=n_mem)
        xt, h = _xattn(xt, norm_xattn_g[l], w_q, kv, w_xout, norm_ffn_g[l], seq=seq, n_mem=n_mem)

        a = _ffn_up(h, w_g, w_u, d_ff=d_ff)
        xt = _ffn_down(a, w_d, xt, norm_final_g, final_norm=(l == depth - 1), d_ff=d_ff)
    return xt.reshape(batch, seq, d)
```

```python
import functools
import math

import jax
import jax.numpy as jnp
from jax import lax
from jax.experimental import pallas as pl
from jax.experimental.pallas import tpu as pltpu

F32 = jnp.float32
BF16 = jnp.bfloat16

HEAD_DIM = 128
MOBA_HEADS = 16
MOBA_WIDTH = MOBA_HEADS * HEAD_DIM
MOBA_BLOCK = 256
MOBA_TOPK = 3
SGU_CHUNK = 128
SGU_GROUPS = 16
XATTN_HEADS = 4
XATTN_WIDTH = XATTN_HEADS * HEAD_DIM
RMS_EPS = 1e-6
LN_EPS = 1e-5
ROPE_THETA = 10000.0
NEG_INF = -1e30
REMOVED = -3e38
Q_SCALE = HEAD_DIM ** -0.5 * math.log2(math.e)

LANES = 128
BF16_ROWS = 16
FF_PAD = 1024
INPROJ_ROWS = 1024
INPROJ_COLS = 1024
PROJ_COLS = 256
DOWN_COLS = 1024
NORM_ROWS = 64
VMEM_LIMIT = 60 * 1024 * 1024


def _cparams(sem):
    return pltpu.CompilerParams(dimension_semantics=sem, vmem_limit_bytes=VMEM_LIMIT)


def _rmsnorm_rows(x, g):
    ms = jnp.mean(x * x, axis=-1, keepdims=True)
    return x * lax.rsqrt(ms + RMS_EPS) * g


def _rmsnorm_kernel(x_ref, g_ref, o_ref):
    o_ref[...] = _rmsnorm_rows(x_ref[...], g_ref[...]).astype(o_ref.dtype)


def _rmsnorm(x, g, out_dtype, bm=512):
    t, d = x.shape
    return pl.pallas_call(
        _rmsnorm_kernel,
        grid=(t // bm,),
        in_specs=[pl.BlockSpec((bm, d), lambda i: (i, 0)),
                  pl.BlockSpec((1, d), lambda i: (0, 0))],
        out_specs=pl.BlockSpec((bm, d), lambda i: (i, 0)),
        out_shape=jax.ShapeDtypeStruct((t, d), out_dtype),
        compiler_params=_cparams(("arbitrary",)),
        name="rmsnorm",
    )(x, g.reshape(1, d))


def _rope_kernel(freq_ref, cos_ref, sin_ref, *, rows):
    i = pl.program_id(0)
    pos = (i * rows + lax.broadcasted_iota(jnp.int32, (rows, LANES), 0)).astype(F32)
    lane = lax.broadcasted_iota(jnp.int32, (rows, LANES), 1)
    ang = pos * freq_ref[...]
    cos_ref[...] = jnp.cos(ang)
    s = jnp.sin(ang)
    sin_ref[...] = jnp.where(lane < HEAD_DIM // 2, -s, s)


def _rope_table(seq, rows=1024):
    half = HEAD_DIM // 2
    inv_freq = jnp.power(ROPE_THETA, -(jnp.arange(half, dtype=F32) * 2.0 / HEAD_DIM))
    freq = jnp.concatenate([inv_freq, inv_freq]).reshape(1, HEAD_DIM)
    return pl.pallas_call(
        functools.partial(_rope_kernel, rows=rows),
        grid=(seq // rows,),
        in_specs=[pl.BlockSpec((1, HEAD_DIM), lambda i: (0, 0))],
        out_specs=[pl.BlockSpec((rows, HEAD_DIM), lambda i: (i, 0))] * 2,
        out_shape=[jax.ShapeDtypeStruct((seq, HEAD_DIM), F32)] * 2,
        compiler_params=_cparams(("arbitrary",)),
        name="rope_table",
    )(freq)


def _gelu_tanh(x):
    c = math.sqrt(2.0 / math.pi)
    half = 0.5 * x
    return half + half * jnp.tanh(x * (c + (c * 0.044715) * (x * x)))


def _silu(x):
    half = 0.5 * x
    return half + half * jnp.tanh(half)


def _inproj_kernel(h_ref, w_ref, *refs, bn, segments):
    *rope_refs, z_ref = refs
    n = pl.program_id(1)

    def project(epilogue):
        h = h_ref[...]
        for c in range(bn // PROJ_COLS):
            cols = slice(c * PROJ_COLS, (c + 1) * PROJ_COLS)
            blk = jnp.dot(h, w_ref[:, cols], preferred_element_type=F32)
            z_ref[:, cols] = epilogue(blk).astype(z_ref.dtype)

    def rotary(scale):
        def rotate(blk):
            cos, sin = rope_refs[0][...], rope_refs[1][...]
            heads = []
            for hh in range(PROJ_COLS // HEAD_DIM):
                x = blk[:, hh * HEAD_DIM:(hh + 1) * HEAD_DIM]
                heads.append(x * cos + pltpu.roll(x, HEAD_DIM // 2, axis=1) * sin)
            out = jnp.concatenate(heads, axis=1)
            return out if scale is None else out * scale
        return rotate

    epilogues = {
        "rope_q": rotary(Q_SCALE),
        "rope": rotary(None),
        "plain": lambda blk: blk,
        "gelu": _gelu_tanh,
        "sigmoid": lambda blk: 0.5 + 0.5 * jnp.tanh(0.5 * blk),
    }
    start = 0
    for kind, count in segments:
        pl.when((n >= start) & (n < start + count))(functools.partial(project, epilogues[kind]))
        start += count


def _inproj(h, w, rope, *, seq, segments, name, bm, bn):
    t, d = h.shape
    cols = w.shape[1]
    assert sum(count for _, count in segments) * bn == cols
    seq_blocks = seq // bm
    rope_spec = pl.BlockSpec((bm, HEAD_DIM), lambda m, n: (m % seq_blocks, 0))
    return pl.pallas_call(
        functools.partial(_inproj_kernel, bn=bn, segments=segments),
        grid=(t // bm, cols // bn),
        in_specs=[pl.BlockSpec((bm, d), lambda m, n: (m, 0)),
                  pl.BlockSpec((d, bn), lambda m, n: (0, n))] + [rope_spec] * len(rope),
        out_specs=pl.BlockSpec((bm, bn), lambda m, n: (m, n)),
        out_shape=jax.ShapeDtypeStruct((t, cols), BF16),
        compiler_params=_cparams(("arbitrary", "arbitrary")),
        name=name,
    )(h, w, *rope)


def _moba_kernel(q_ref, k_ref, v_ref, *refs, tile, n_blocks, n_sub, heads, stage_args, groups, tiles):
    n_stage = len(stage_args)
    stage_src, o_ref, stage_dst = refs[:n_stage], refs[n_stage], refs[n_stage + 1:2 * n_stage + 1]
    onehot_ref, ones_ref, kmean_ref, p_ref, m_ref, alpha_ref, acc_ref = refs[2 * n_stage + 1:]
    _moba_body(q_ref, k_ref, v_ref, o_ref, onehot_ref, ones_ref, kmean_ref, p_ref, m_ref, alpha_ref, acc_ref,
               tile=tile, n_blocks=n_blocks, n_sub=n_sub, heads=heads)
    step = (pl.program_id(0) * groups + pl.program_id(1)) * tiles + pl.program_id(2)
    for src, dst, args in zip(stage_src, stage_dst, stage_args):
        _stage_block(step, src, dst, **args)


def _moba_body(q_ref, k_ref, v_ref, o_ref, onehot_ref, ones_ref, kmean_ref, p_ref, m_ref, alpha_ref, acc_ref,
               *, tile, n_blocks, n_sub, heads):
    qi = pl.program_id(2)
    blocks_per_tile = tile // MOBA_BLOCK
    contract_last = (((1,), (1,)), ((), ()))
    sub = tile // n_sub

    chains = [(hd, i) for hd in range(heads) for i in range(n_sub)]

    def head_cols(hd):
        return slice(hd * HEAD_DIM, (hd + 1) * HEAD_DIM)

    @pl.when(qi == 0)
    def _prepare_keys():
        kmean_ref[...] = jnp.zeros_like(kmean_ref)
        lane = lax.broadcasted_iota(jnp.int32, (MOBA_BLOCK, LANES), 1)
        ones_ref[...] = (lax.broadcasted_iota(jnp.int32, ones_ref.shape, 1) == 0).astype(BF16)
        for j in range(n_blocks):
            rows = pl.ds(j * MOBA_BLOCK, MOBA_BLOCK)
            onehot_ref[rows, :] = (lane == j).astype(BF16)
            for hd in range(heads):
                kb = k_ref[rows, head_cols(hd)]
                kmean_ref[hd, j:j + 1, :] = jnp.mean(kb.astype(F32), axis=0, keepdims=True)

    blk = lax.broadcasted_iota(jnp.int32, (n_blocks, tile), 0)
    col = lax.broadcasted_iota(jnp.int32, (n_blocks, tile), 1)
    own_block = qi * blocks_per_tile + col // MOBA_BLOCK
    q_ext = []
    for hd in range(heads):
        q = q_ref[:, head_cols(hd)]
        gate = lax.dot_general(kmean_ref[hd].astype(BF16), q, contract_last,
                               preferred_element_type=F32)[:n_blocks, :]
        gate = jnp.where(blk < own_block, gate, NEG_INF)
        allowed = blk == own_block
        for _ in range(MOBA_TOPK):
            best = jnp.max(gate, axis=0, keepdims=True)
            first = jnp.min(jnp.where(gate == best, blk, n_blocks), axis=0, keepdims=True)
            hit = blk == first
            allowed = allowed | (hit & (best > 0.5 * NEG_INF))
            gate = jnp.where(hit, REMOVED, gate)
        bias_t = jnp.where(allowed, 0.0, NEG_INF)
        bias_t = jnp.concatenate([bias_t, jnp.zeros((LANES - n_blocks, tile), F32)], axis=0)
        bias = jnp.transpose(bias_t).astype(BF16)
        q_ext.append(jnp.concatenate([q, bias], axis=1))

    def keys(hd, j, rows=tile):
        at = pl.ds(pl.multiple_of(j * tile, tile), rows)
        return jnp.concatenate([k_ref[at, head_cols(hd)], onehot_ref[at, :]], axis=1)

    def values(hd, j):
        at = pl.ds(pl.multiple_of(j * tile, tile), tile)
        return jnp.concatenate([v_ref[at, head_cols(hd)], ones_ref[...]], axis=1)

    def scores(hd, i, k_tile):
        return lax.dot_general(q_ext[hd][i * sub:(i + 1) * sub], k_tile, contract_last,
                               preferred_element_type=F32)

    def weighted_values(hd, i, v_tile):
        return jnp.dot(p_ref[hd, i * sub:(i + 1) * sub, :], v_tile, preferred_element_type=F32)

    for hd, i in chains:
        rows = slice(i * sub, (i + 1) * sub)
        seen = (i + 1) * sub
        r = i * sub + lax.broadcasted_iota(jnp.int32, (sub, seen), 0)
        c = lax.broadcasted_iota(jnp.int32, (sub, seen), 1)
        s = jnp.where(c <= r, scores(hd, i, keys(hd, qi, seen)), NEG_INF)
        m0 = jnp.max(s, axis=1, keepdims=True)
        p_ref[hd, rows, :seen] = jnp.exp2(s - m0).astype(BF16)
        if seen < tile:
            p_ref[hd, rows, seen:] = jnp.zeros((sub, tile - seen), BF16)
        m_ref[hd, rows, :] = m0
        alpha_ref[hd, rows, :] = jnp.ones((sub, 1), F32)
        acc_ref[hd, rows, :] = jnp.zeros((sub, 2 * HEAD_DIM), F32)

    def accumulate(hd, i, v_tile):
        rows = slice(i * sub, (i + 1) * sub)
        return alpha_ref[hd, rows, :] * acc_ref[hd, rows, :] + weighted_values(hd, i, v_tile)

    def trip(j):
        j_prev = jnp.where(j == 0, qi, j - 1)
        k_tiles = [keys(hd, j) for hd in range(heads)]
        v_tiles = [values(hd, j_prev) for hd in range(heads)]
        for hd, i in chains:
            rows = slice(i * sub, (i + 1) * sub)
            acc_ref[hd, rows, :] = accumulate(hd, i, v_tiles[hd])
            s = scores(hd, i, k_tiles[hd])
            m = m_ref[hd, rows, :]
            m_new = jnp.maximum(m, jnp.max(s, axis=1, keepdims=True))
            p_ref[hd, rows, :] = jnp.exp2(s - m_new).astype(BF16)
            alpha_ref[hd, rows, :] = jnp.exp2(m - m_new)
            m_ref[hd, rows, :] = m_new

    odd = qi % 2
    pl.when(odd == 1)(functools.partial(trip, 0))

    def pair(t, carry):
        trip(odd + 2 * t)
        trip(odd + 2 * t + 1)
        return carry

    lax.fori_loop(0, qi // 2, pair, 0)
    j_last = jnp.where(qi == 0, qi, qi - 1)
    for hd, i in chains:
        acc = accumulate(hd, i, values(hd, j_last))
        o_ref[i * sub:(i + 1) * sub, head_cols(hd)] = (
            acc[:, :HEAD_DIM] / acc[:, HEAD_DIM:HEAD_DIM + 1]).astype(o_ref.dtype)


def _stage_plan(cols_used, col_start, out_shape, steps):
    rows_out, cols_out = out_shape
    for col_blocks in (1, 2, 4):
        row_blocks = steps // col_blocks
        rb = rows_out // row_blocks
        if steps % col_blocks or rows_out % row_blocks or rb % BF16_ROWS:
            continue
        cb_in = cols_used if col_blocks == 1 else cols_out // col_blocks
        if col_blocks > 1 and (cols_used != cols_out or cb_in % LANES):
            continue
        if col_start % cb_in:
            continue
        return rb, col_blocks, cb_in
    raise ValueError(f"no staging plan for {cols_used} columns at {col_start} -> {out_shape} in {steps} steps")


def _stage_block(step, src_ref, dst_ref, *, rows, rows_out, col_blocks):
    rb, cb_in = src_ref.shape
    x = src_ref[...]
    if rows_out > rows:
        row = (step // col_blocks) * rb + lax.broadcasted_iota(jnp.int32, x.shape, 0)
        x = jnp.where(row < rows, x, 0.0)
    dst_ref[:, :cb_in] = x.astype(dst_ref.dtype)
    if dst_ref.shape[1] > cb_in:
        dst_ref[:, cb_in:] = jnp.zeros((rb, dst_ref.shape[1] - cb_in), dst_ref.dtype)


def _moba(z, *, batch, seq, stage=(), tile=1024, n_sub=4, heads=2):
    t = z.shape[0]
    n_blocks = seq // MOBA_BLOCK
    tiles = seq // tile
    groups = MOBA_HEADS // heads
    width = heads * HEAD_DIM
    steps = batch * groups * tiles

    def step_of(b, g, i):
        return (b * groups + g) * tiles + i

    stage_in, stage_out, stage_shapes, stage_args = [], [], [], []
    for w, out_shape, col_start in stage:
        cols_used = min(w.shape[1] - col_start, out_shape[1])
        rb, col_blocks, cb_in = _stage_plan(cols_used, col_start, out_shape, steps)
        cb_out = out_shape[1] // col_blocks
        last = (w.shape[0] - 1) // rb
        stage_in.append(pl.BlockSpec(
            (rb, cb_in), lambda b, g, i, cbs=col_blocks, last=last, c0=col_start // cb_in:
            (jnp.minimum(step_of(b, g, i) // cbs, last), c0 + step_of(b, g, i) % cbs)))
        stage_out.append(pl.BlockSpec(
            (rb, cb_out), lambda b, g, i, cbs=col_blocks: (step_of(b, g, i) // cbs, step_of(b, g, i) % cbs)))
        stage_shapes.append(jax.ShapeDtypeStruct(out_shape, BF16))
        stage_args.append(dict(rows=w.shape[0], rows_out=out_shape[0], col_blocks=col_blocks))

    kern = functools.partial(_moba_kernel, tile=tile, n_blocks=n_blocks, n_sub=n_sub, heads=heads,
                             stage_args=tuple(stage_args), groups=groups, tiles=tiles)
    outs = pl.pallas_call(
        kern,
        grid=(batch, groups, tiles),
        in_specs=[pl.BlockSpec((tile, width), lambda b, g, i: (b * tiles + i, g)),
                  pl.BlockSpec((seq, width), lambda b, g, i: (b, groups + g)),
                  pl.BlockSpec((seq, width), lambda b, g, i: (b, 2 * groups + g))] + stage_in,
        out_specs=[pl.BlockSpec((tile, width), lambda b, g, i: (b * tiles + i, g))] + stage_out,
        out_shape=[jax.ShapeDtypeStruct((t, MOBA_WIDTH), BF16)] + stage_shapes,
        scratch_shapes=[pltpu.VMEM((seq, LANES), BF16),
                        pltpu.VMEM((tile, LANES), BF16),
                        pltpu.VMEM((heads, LANES, HEAD_DIM), F32),
                        pltpu.VMEM((heads, tile, tile), BF16),
                        pltpu.VMEM((heads, tile, 1), F32),
                        pltpu.VMEM((heads, tile, 1), F32),
                        pltpu.VMEM((heads, tile, 2 * HEAD_DIM), F32)],
        compiler_params=_cparams(("arbitrary", "arbitrary", "arbitrary")),
        name="moba_attention",
    )(z, z, z, *[item[0] for item in stage])
    return outs[0] if not stage else tuple(outs)


def _sgu_kernel(u_ref, v_ref, g_ref, b_ref, w_ref, bs_ref, o_ref, *, rows):
    v = v_ref[...].astype(F32)
    mu = jnp.mean(v, axis=-1, keepdims=True)
    var = jnp.mean(jnp.square(v - mu), axis=-1, keepdims=True)
    vln = ((v - mu) * lax.rsqrt(var + LN_EPS) * g_ref[...] + b_ref[...]).astype(BF16)
    t_idx = lax.broadcasted_iota(jnp.int32, (SGU_CHUNK, SGU_CHUNK), 0)
    s_idx = lax.broadcasted_iota(jnp.int32, (SGU_CHUNK, SGU_CHUNK), 1)
    causal = s_idx <= t_idx
    for g in range(SGU_GROUPS):
        w = jnp.where(causal, w_ref[g], 0.0).astype(BF16)
        bias = bs_ref[:, g:g + 1]
        cols = slice(g * LANES, (g + 1) * LANES)
        for ch in range(rows // SGU_CHUNK):
            rws = slice(ch * SGU_CHUNK, (ch + 1) * SGU_CHUNK)
            mixed = jnp.dot(w, vln[rws, cols], preferred_element_type=F32) + bias
            o_ref[rws, cols] = (u_ref[rws, cols].astype(F32) * mixed).astype(o_ref.dtype)


def _sgu(z, ln_g, ln_b, w_s, b_s_t, *, width, u_col, rows=1024):
    t = z.shape[0]
    u_blk = u_col // width
    kern = functools.partial(_sgu_kernel, rows=rows)
    return pl.pallas_call(
        kern,
        grid=(t // rows,),
        in_specs=[pl.BlockSpec((rows, width), lambda i: (i, u_blk)),
                  pl.BlockSpec((rows, width), lambda i: (i, u_blk + 1)),
                  pl.BlockSpec((1, width), lambda i: (0, 0)),
                  pl.BlockSpec((1, width), lambda i: (0, 0)),
                  pl.BlockSpec((SGU_GROUPS, SGU_CHUNK, SGU_CHUNK), lambda i: (0, 0, 0)),
                  pl.BlockSpec((SGU_CHUNK, SGU_GROUPS), lambda i: (0, 0))],
        out_specs=pl.BlockSpec((rows, width), lambda i: (i, 0)),
        out_shape=jax.ShapeDtypeStruct((t, width), BF16),
        compiler_params=_cparams(("arbitrary",)),
        name="spatial_gating",
    )(z, z, ln_g.reshape(1, width), ln_b.reshape(1, width), w_s, b_s_t)


def _merge_kernel(oa_ref, ob_ref, pa_ref, pb_ref, ga_ref, gb_ref, o_ref):
    oa, ob = oa_ref[...], ob_ref[...]
    for c in range(o_ref.shape[1] // PROJ_COLS):
        cols = slice(c * PROJ_COLS, (c + 1) * PROJ_COLS)
        a = jnp.dot(oa, pa_ref[:, cols], preferred_element_type=F32)
        b = jnp.dot(ob, pb_ref[:, cols], preferred_element_type=F32)
        o_ref[:, cols] = (ga_ref[:, cols].astype(F32) * a + gb_ref[:, cols].astype(F32) * b).astype(o_ref.dtype)


def _merge(o_a, o_b, p_a, p_b, z, *, gate_col, bm=1024, bn=1024):
    t = o_a.shape[0]
    d = p_a.shape[1]
    ga_blk = gate_col // bn
    gb_blk = (gate_col + d) // bn
    return pl.pallas_call(
        _merge_kernel,
        grid=(t // bm, d // bn),
        in_specs=[pl.BlockSpec((bm, o_a.shape[1]), lambda m, n: (m, 0)),
                  pl.BlockSpec((bm, o_b.shape[1]), lambda m, n: (m, 0)),
                  pl.BlockSpec((p_a.shape[0], bn), lambda m, n: (0, n)),
                  pl.BlockSpec((p_b.shape[0], bn), lambda m, n: (0, n)),
                  pl.BlockSpec((bm, bn), lambda m, n: (m, ga_blk + n)),
                  pl.BlockSpec((bm, bn), lambda m, n: (m, gb_blk + n))],
        out_specs=pl.BlockSpec((bm, bn), lambda m, n: (m, n)),
        out_shape=jax.ShapeDtypeStruct((t, d), BF16),
        compiler_params=_cparams(("arbitrary", "arbitrary")),
        name="branch_merge",
    )(o_a, o_b, p_a, p_b, z, z)


def _outproj_kernel(a_ref, w_ref, x_ref, o_ref):
    o_ref[...] = x_ref[...] + jnp.dot(a_ref[...], w_ref[...], preferred_element_type=F32)


def _outproj(a, w, x, *, bm=1024, bn=1024):
    t, k = a.shape
    d = w.shape[1]
    return pl.pallas_call(
        _outproj_kernel,
        grid=(t // bm, d // bn),
        in_specs=[pl.BlockSpec((bm, k), lambda m, n: (m, 0)),
                  pl.BlockSpec((k, bn), lambda m, n: (0, n)),
                  pl.BlockSpec((bm, bn), lambda m, n: (m, n))],
        out_specs=pl.BlockSpec((bm, bn), lambda m, n: (m, n)),
        out_shape=jax.ShapeDtypeStruct((t, d), F32),
        compiler_params=_cparams(("arbitrary", "arbitrary")),
        name="out_proj",
    )(a, w, x)


def _memkv_kernel(mem_ref, g_ref, w_ref, o_ref):
    mem_n = _rmsnorm_rows(mem_ref[...], g_ref[...]).astype(BF16)
    o_ref[...] = jnp.dot(mem_n, w_ref[...], preferred_element_type=F32).astype(o_ref.dtype)


def _memkv(mem, g, w, *, n_mem):
    t, d = mem.shape
    cols = w.shape[1]
    return pl.pallas_call(
        _memkv_kernel,
        grid=(t // n_mem,),
        in_specs=[pl.BlockSpec((n_mem, d), lambda i: (i, 0)),
                  pl.BlockSpec((1, d), lambda i: (0, 0)),
                  pl.BlockSpec((d, cols), lambda i: (0, 0))],
        out_specs=pl.BlockSpec((n_mem, cols), lambda i: (i, 0)),
        out_shape=jax.ShapeDtypeStruct((t, cols), BF16),
        compiler_params=_cparams(("arbitrary",)),
        name="mem_kv",
    )(mem, g.reshape(1, d), w)


def _xattn_kernel(x_ref, gx_ref, wq_ref, kv_ref, wo_ref, gf_ref, x_out_ref, h_out_ref, *, chain_rows):
    chains = [slice(c * chain_rows, (c + 1) * chain_rows) for c in range(x_ref.shape[0] // chain_rows)]

    def attend(q):
        outs = []
        for hh in range(XATTN_HEADS):
            cols = slice(hh * HEAD_DIM, (hh + 1) * HEAD_DIM)
            k = kv_ref[:, hh * HEAD_DIM:(hh + 1) * HEAD_DIM]
            v = kv_ref[:, XATTN_WIDTH + hh * HEAD_DIM:XATTN_WIDTH + (hh + 1) * HEAD_DIM]
            s = lax.dot_general(q[:, cols], k, (((1,), (1,)), ((), ())), preferred_element_type=F32)
            s = s - jnp.max(s, axis=-1, keepdims=True)
            p = jnp.exp(s)
            p = p / jnp.sum(p, axis=-1, keepdims=True)
            outs.append(jnp.dot(p.astype(BF16), v, preferred_element_type=F32).astype(BF16))
        return jnp.concatenate(outs, axis=1)

    hs = [_rmsnorm_rows(x_ref[rows, :], gx_ref[...]).astype(BF16) for rows in chains]
    qs = [(jnp.dot(h, wq_ref[...], preferred_element_type=F32) * (HEAD_DIM ** -0.5)).astype(BF16) for h in hs]
    os = [attend(q) for q in qs]
    for rows, o in zip(chains, os):
        x_out_ref[rows, :] = x_ref[rows, :] + jnp.dot(o, wo_ref[...], preferred_element_type=F32)
    for rows in chains:
        h_out_ref[rows, :] = _rmsnorm_rows(x_out_ref[rows, :], gf_ref[...]).astype(h_out_ref.dtype)


def _xattn(x, g_x, w_q, kv, w_o, g_ffn, *, seq, n_mem, bm=512, chain_rows=256):
    t, d = x.shape
    per_batch = seq // bm
    resident = pl.Buffered(1)
    return pl.pallas_call(
        functools.partial(_xattn_kernel, chain_rows=chain_rows),
        grid=(t // bm,),
        in_specs=[pl.BlockSpec((bm, d), lambda i: (i, 0)),
                  pl.BlockSpec((1, d), lambda i: (0, 0)),
                  pl.BlockSpec(w_q.shape, lambda i: (0, 0), pipeline_mode=resident),
                  pl.BlockSpec((n_mem, kv.shape[1]), lambda i: (i // per_batch, 0)),
                  pl.BlockSpec(w_o.shape, lambda i: (0, 0), pipeline_mode=resident),
                  pl.BlockSpec((1, d), lambda i: (0, 0))],
        out_specs=[pl.BlockSpec((bm, d), lambda i: (i, 0)),
                   pl.BlockSpec((bm, d), lambda i: (i, 0))],
        out_shape=[jax.ShapeDtypeStruct((t, d), F32),
                   jax.ShapeDtypeStruct((t, d), BF16)],
        compiler_params=_cparams(("arbitrary",)),
        name="mem_cross_attention",
    )(x, g_x.reshape(1, d), w_q, kv, w_o, g_ffn.reshape(1, d))


def _ffn_up_kernel(h_ref, wg_ref, wu_ref, o_ref, *, last_cols):
    n = pl.program_id(1)
    bn = o_ref.shape[1]

    def swiglu(cols):
        h = h_ref[...]
        for c in range(cols // PROJ_COLS):
            sl = slice(c * PROJ_COLS, (c + 1) * PROJ_COLS)
            g = jnp.dot(h, wg_ref[:, sl], preferred_element_type=F32)
            u = jnp.dot(h, wu_ref[:, sl], preferred_element_type=F32)
            o_ref[:, sl] = (_silu(g) * u).astype(o_ref.dtype)
        if cols < bn:
            o_ref[:, cols:] = jnp.zeros((o_ref.shape[0], bn - cols), o_ref.dtype)

    last = pl.num_programs(1) - 1
    pl.when(n < last)(functools.partial(swiglu, bn))
    pl.when(n == last)(functools.partial(swiglu, last_cols))


def _ffn_up(h, w_g, w_u, *, d_ff, bm=1024, bn=1024):
    t, d = h.shape
    f = w_g.shape[1]
    assert f % bn == 0 and (d_ff - (f // bn - 1) * bn) % PROJ_COLS == 0
    return pl.pallas_call(
        functools.partial(_ffn_up_kernel, last_cols=d_ff - (f // bn - 1) * bn),
        grid=(t // bm, f // bn),
        in_specs=[pl.BlockSpec((bm, d), lambda m, n: (m, 0)),
                  pl.BlockSpec((d, bn), lambda m, n: (0, n)),
                  pl.BlockSpec((d, bn), lambda m, n: (0, n))],
        out_specs=pl.BlockSpec((bm, bn), lambda m, n: (m, n)),
        out_shape=jax.ShapeDtypeStruct((t, f), BF16),
        compiler_params=_cparams(("arbitrary", "arbitrary")),
        name="ffn_up",
    )(h, w_g, w_u)


def _ffn_down_kernel(a_ref, w_ref, x_ref, g_ref, o_ref, *, final_norm, last_rows):
    k = pl.program_id(1)
    last = pl.num_programs(1) - 1
    bm, d = o_ref.shape

    def accumulate(base_ref, depth):
        a = a_ref[:, :depth]
        for c in range(d // DOWN_COLS):
            cols = slice(c * DOWN_COLS, (c + 1) * DOWN_COLS)
            o_ref[:, cols] = base_ref[:, cols] + jnp.dot(a, w_ref[:depth, cols], preferred_element_type=F32)

    pl.when(k == 0)(functools.partial(accumulate, x_ref, a_ref.shape[1]))
    pl.when((k > 0) & (k < last))(functools.partial(accumulate, o_ref, a_ref.shape[1]))
    pl.when(k == last)(functools.partial(accumulate, o_ref, last_rows))

    if final_norm:
        @pl.when(k == last)
        def _final_norm():
            for r in range(bm // NORM_ROWS):
                rows = slice(r * NORM_ROWS, (r + 1) * NORM_ROWS)
                o_ref[rows, :] = _rmsnorm_rows(o_ref[rows, :], g_ref[...])


def _ffn_down(a, w, x, g, *, final_norm, d_ff, bm=512, bk=1024):
    t, f = a.shape
    d = w.shape[1]
    assert f // bk >= 2, "first and last contraction steps must differ"
    return pl.pallas_call(
        functools.partial(_ffn_down_kernel, final_norm=final_norm, last_rows=d_ff - (f // bk - 1) * bk),
        grid=(t // bm, f // bk),
        in_specs=[pl.BlockSpec((bm, bk), lambda m, k: (m, k)),
                  pl.BlockSpec((bk, d), lambda m, k: (k, 0)),
                  pl.BlockSpec((bm, d), lambda m, k: (m, 0)),
                  pl.BlockSpec((1, d), lambda m, k: (0, 0))],
        out_specs=pl.BlockSpec((bm, d), lambda m, k: (m, 0)),
        out_shape=jax.ShapeDtypeStruct((t, d), F32),
        compiler_params=_cparams(("arbitrary", "arbitrary")),
        name="ffn_down",
    )(a, w, x, g.reshape(1, d))


@jax.jit
def kernel(x, mem, norm_mix_g, w_in, sgu_ln_g, sgu_ln_b, w_sgu, b_sgu, w_branch_a, w_branch_b, w_out, norm_xattn_g, norm_mem_g, w_xq, w_xkv, w_xo, norm_ffn_g, w_ff_gate, w_ff_up, w_ff_down, norm_final_g):
    batch, seq, d = x.shape
    n_mem = mem.shape[1]
    depth = w_in.shape[0]
    sgu_width = sgu_ln_g.shape[1]
    d_ff = w_ff_gate.shape[2]
    ff_pad = (-d_ff) % FF_PAD

    xt = x.reshape(batch * seq, d)
    memt = mem.reshape(batch * n_mem, d)
    cos, sin = _rope_table(seq)

    for l in range(depth):
        h = _rmsnorm(xt, norm_mix_g[l], BF16)
        qkv_cols = 3 * MOBA_WIDTH
        blocks = MOBA_WIDTH // INPROJ_COLS
        z_qkv = _inproj(h, w_in[l][:, :qkv_cols].astype(BF16), (cos, sin), seq=seq, name="in_proj_qkv",
                        segments=(("rope_q", blocks), ("rope", blocks), ("plain", blocks)),
                        bm=INPROJ_ROWS, bn=INPROJ_COLS)
        f_pad = d_ff + ff_pad
        rest_cols = w_in.shape[2] - qkv_cols
        o_a, w_rest, w_g, w_u, w_d, w_pa, w_pb, w_o, w_q, w_kv, w_xout = _moba(
            z_qkv, batch=batch, seq=seq, stage=(
                (w_in[l], (d, rest_cols), qkv_cols),
                (w_ff_gate[l], (d, f_pad), 0), (w_ff_up[l], (d, f_pad), 0), (w_ff_down[l], (f_pad, d), 0),
                (w_branch_a[l], w_branch_a[l].shape, 0), (w_branch_b[l], w_branch_b[l].shape, 0),
                (w_out[l], w_out[l].shape, 0), (w_xq[l], w_xq[l].shape, 0), (w_xkv[l], w_xkv[l].shape, 0),
                (w_xo[l], w_xo[l].shape, 0)))
        z = _inproj(h, w_rest, (), seq=seq, name="in_proj_rest", bm=INPROJ_ROWS, bn=INPROJ_COLS,
                    segments=(("gelu", 2 * sgu_width // INPROJ_COLS), ("sigmoid", 2 * d // INPROJ_COLS)))
        o_b = _sgu(z, sgu_ln_g[l], sgu_ln_b[l], w_sgu[l], jnp.transpose(b_sgu[l]), width=sgu_width, u_col=0)
        merged = _merge(o_a, o_b, w_pa, w_pb, z, gate_col=2 * sgu_width)
        xt = _outproj(merged, w_o, xt)

        kv = _memkv(memt, norm_mem_g[l], w_kv, n_mem=n_mem)
        xt, h = _xattn(xt, norm_xattn_g[l], w_q, kv, w_xout, norm_ffn_g[l], seq=seq, n_mem=n_mem)

        a = _ffn_up(h, w_g, w_u, d_ff=d_ff)
        xt = _ffn_down(a, w_d, xt, norm_final_g, final_norm=(l == depth - 1), d_ff=d_ff)
    return xt.reshape(batch, seq, d)
```

```python
import functools
import math

import jax
import jax.numpy as jnp
from jax import lax
from jax.experimental import pallas as pl
from jax.experimental.pallas import tpu as pltpu

F32 = jnp.float32
BF16 = jnp.bfloat16

HEAD_DIM = 128
MOBA_HEADS = 16
MOBA_WIDTH = MOBA_HEADS * HEAD_DIM
MOBA_BLOCK = 256
MOBA_TOPK = 3
SGU_CHUNK = 128
SGU_GROUPS = 16
XATTN_HEADS = 4
XATTN_WIDTH = XATTN_HEADS * HEAD_DIM
RMS_EPS = 1e-6
LN_EPS = 1e-5
ROPE_THETA = 10000.0
NEG_INF = -1e30
REMOVED = -3e38
Q_SCALE = HEAD_DIM ** -0.5 * math.log2(math.e)

LANES = 128
BF16_ROWS = 16
FF_PAD = 1024
INPROJ_ROWS = 1024
INPROJ_COLS = 1024
PROJ_COLS = 256
DOWN_COLS = 1024
NORM_ROWS = 64
VMEM_LIMIT = 60 * 1024 * 1024


def _cparams(sem):
    return pltpu.CompilerParams(dimension_semantics=sem, vmem_limit_bytes=VMEM_LIMIT)


def _rmsnorm_rows(x, g):
    ms = jnp.mean(x * x, axis=-1, keepdims=True)
    return x * lax.rsqrt(ms + RMS_EPS) * g


def _rmsnorm_kernel(x_ref, g_ref, o_ref):
    o_ref[...] = _rmsnorm_rows(x_ref[...], g_ref[...]).astype(o_ref.dtype)


def _rmsnorm(x, g, out_dtype, bm=512):
    t, d = x.shape
    return pl.pallas_call(
        _rmsnorm_kernel,
        grid=(t // bm,),
        in_specs=[pl.BlockSpec((bm, d), lambda i: (i, 0)),
                  pl.BlockSpec((1, d), lambda i: (0, 0))],
        out_specs=pl.BlockSpec((bm, d), lambda i: (i, 0)),
        out_shape=jax.ShapeDtypeStruct((t, d), out_dtype),
        compiler_params=_cparams(("arbitrary",)),
        name="rmsnorm",
    )(x, g.reshape(1, d))


def _rope_kernel(freq_ref, cos_ref, sin_ref, *, rows):
    i = pl.program_id(0)
    pos = (i * rows + lax.broadcasted_iota(jnp.int32, (rows, LANES), 0)).astype(F32)
    lane = lax.broadcasted_iota(jnp.int32, (rows, LANES), 1)
    ang = pos * freq_ref[...]
    cos_ref[...] = jnp.cos(ang)
    s = jnp.sin(ang)
    sin_ref[...] = jnp.where(lane < HEAD_DIM // 2, -s, s)


def _rope_table(seq, rows=1024):
    half = HEAD_DIM // 2
    inv_freq = jnp.power(ROPE_THETA, -(jnp.arange(half, dtype=F32) * 2.0 / HEAD_DIM))
    freq = jnp.concatenate([inv_freq, inv_freq]).reshape(1, HEAD_DIM)
    return pl.pallas_call(
        functools.partial(_rope_kernel, rows=rows),
        grid=(seq // rows,),
        in_specs=[pl.BlockSpec((1, HEAD_DIM), lambda i: (0, 0))],
        out_specs=[pl.BlockSpec((rows, HEAD_DIM), lambda i: (i, 0))] * 2,
        out_shape=[jax.ShapeDtypeStruct((seq, HEAD_DIM), F32)] * 2,
        compiler_params=_cparams(("arbitrary",)),
        name="rope_table",
    )(freq)


def _gelu_tanh(x):
    c = math.sqrt(2.0 / math.pi)
    half = 0.5 * x
    return half + half * jnp.tanh(x * (c + (c * 0.044715) * (x * x)))


def _silu(x):
    half = 0.5 * x
    return half + half * jnp.tanh(half)


def _inproj_kernel(h_ref, w_ref, *refs, bn, segments):
    *rope_refs, z_ref = refs
    n = pl.program_id(1)

    def project(epilogue):
        h = h_ref[...]
        for c in range(bn // PROJ_COLS):
            cols = slice(c * PROJ_COLS, (c + 1) * PROJ_COLS)
            blk = jnp.dot(h, w_ref[:, cols], preferred_element_type=F32)
            z_ref[:, cols] = epilogue(blk).astype(z_ref.dtype)

    def rotary(scale):
        def rotate(blk):
            cos, sin = rope_refs[0][...], rope_refs[1][...]
            heads = []
            for hh in range(PROJ_COLS // HEAD_DIM):
                x = blk[:, hh * HEAD_DIM:(hh + 1) * HEAD_DIM]
                heads.append(x * cos + pltpu.roll(x, HEAD_DIM // 2, axis=1) * sin)
            out = jnp.concatenate(heads, axis=1)
            return out if scale is None else out * scale
        return rotate

    epilogues = {
        "rope_q": rotary(Q_SCALE),
        "rope": rotary(None),
        "plain": lambda blk: blk,
        "gelu": _gelu_tanh,
        "sigmoid": lambda blk: 0.5 + 0.5 * jnp.tanh(0.5 * blk),
    }
    start = 0
    for kind, count in segments:
        pl.when((n >= start) & (n < start + count))(functools.partial(project, epilogues[kind]))
        start += count


def _inproj(h, w, rope, *, seq, segments, name, bm, bn):
    t, d = h.shape
    cols = w.shape[1]
    assert sum(count for _, count in segments) * bn == cols
    seq_blocks = seq // bm
    rope_spec = pl.BlockSpec((bm, HEAD_DIM), lambda m, n: (m % seq_blocks, 0))
    return pl.pallas_call(
        functools.partial(_inproj_kernel, bn=bn, segments=segments),
        grid=(t // bm, cols // bn),
        in_specs=[pl.BlockSpec((bm, d), lambda m, n: (m, 0)),
                  pl.BlockSpec((d, bn), lambda m, n: (0, n))] + [rope_spec] * len(rope),
        out_specs=pl.BlockSpec((bm, bn), lambda m, n: (m, n)),
        out_shape=jax.ShapeDtypeStruct((t, cols), BF16),
        compiler_params=_cparams(("arbitrary", "arbitrary")),
        name=name,
    )(h, w, *rope)


def _moba_kernel(q_ref, k_ref, v_ref, *refs, tile, n_blocks, n_sub, heads, stage_args, groups, tiles):
    n_stage = len(stage_args)
    stage_src, o_ref, stage_dst = refs[:n_stage], refs[n_stage], refs[n_stage + 1:2 * n_stage + 1]
    onehot_ref, ones_ref, kmean_ref, p_ref, m_ref, alpha_ref, acc_ref = refs[2 * n_stage + 1:]
    _moba_body(q_ref, k_ref, v_ref, o_ref, onehot_ref, ones_ref, kmean_ref, p_ref, m_ref, alpha_ref, acc_ref,
               tile=tile, n_blocks=n_blocks, n_sub=n_sub, heads=heads)
    step = (pl.program_id(0) * groups + pl.program_id(1)) * tiles + pl.program_id(2)
    for src, dst, args in zip(stage_src, stage_dst, stage_args):
        _stage_block(step, src, dst, **args)


def _moba_body(q_ref, k_ref, v_ref, o_ref, onehot_ref, ones_ref, kmean_ref, p_ref, m_ref, alpha_ref, acc_ref,
               *, tile, n_blocks, n_sub, heads):
    qi = pl.program_id(2)
    blocks_per_tile = tile // MOBA_BLOCK
    contract_last = (((1,), (1,)), ((), ()))
    sub = tile // n_sub

    chains = [(hd, i) for hd in range(heads) for i in range(n_sub)]

    def head_cols(hd):
        return slice(hd * HEAD_DIM, (hd + 1) * HEAD_DIM)

    @pl.when(qi == 0)
    def _prepare_keys():
        kmean_ref[...] = jnp.zeros_like(kmean_ref)
        lane = lax.broadcasted_iota(jnp.int32, (MOBA_BLOCK, LANES), 1)
        ones_ref[...] = (lax.broadcasted_iota(jnp.int32, ones_ref.shape, 1) == 0).astype(BF16)
        for j in range(n_blocks):
            rows = pl.ds(j * MOBA_BLOCK, MOBA_BLOCK)
            onehot_ref[rows, :] = (lane == j).astype(BF16)
            for hd in range(heads):
                kb = k_ref[rows, head_cols(hd)]
                kmean_ref[hd, j:j + 1, :] = jnp.mean(kb.astype(F32), axis=0, keepdims=True)

    blk = lax.broadcasted_iota(jnp.int32, (n_blocks, tile), 0)
    col = lax.broadcasted_iota(jnp.int32, (n_blocks, tile), 1)
    own_block = qi * blocks_per_tile + col // MOBA_BLOCK
    q_ext = []
    for hd in range(heads):
        q = q_ref[:, head_cols(hd)]
        gate = lax.dot_general(kmean_ref[hd].astype(BF16), q, contract_last,
                               preferred_element_type=F32)[:n_blocks, :]
        gate = jnp.where(blk < own_block, gate, NEG_INF)
        allowed = blk == own_block
        for _ in range(MOBA_TOPK):
            best = jnp.max(gate, axis=0, keepdims=True)
            first = jnp.min(jnp.where(gate == best, blk, n_blocks), axis=0, keepdims=True)
            hit = blk == first
            allowed = allowed | (hit & (best > 0.5 * NEG_INF))
            gate = jnp.where(hit, REMOVED, gate)
        bias_t = jnp.where(allowed, 0.0, NEG_INF)
        bias_t = jnp.concatenate([bias_t, jnp.zeros((LANES - n_blocks, tile), F32)], axis=0)
        bias = jnp.transpose(bias_t).astype(BF16)
        q_ext.append(jnp.concatenate([q, bias], axis=1))

    def keys(hd, j, rows=tile):
        at = pl.ds(pl.multiple_of(j * tile, tile), rows)
        return jnp.concatenate([k_ref[at, head_cols(hd)], onehot_ref[at, :]], axis=1)

    def values(hd, j):
        at = pl.ds(pl.multiple_of(j * tile, tile), tile)
        return jnp.concatenate([v_ref[at, head_cols(hd)], ones_ref[...]], axis=1)

    def scores(hd, i, k_tile):
        return lax.dot_general(q_ext[hd][i * sub:(i + 1) * sub], k_tile, contract_last,
                               preferred_element_type=F32)

    def weighted_values(hd, i, v_tile):
        return jnp.dot(p_ref[hd, i * sub:(i + 1) * sub, :], v_tile, preferred_element_type=F32)

    def diagonal():
        for hd, i in chains:
            rows = slice(i * sub, (i + 1) * sub)
            seen = (i + 1) * sub
            r = i * sub + lax.broadcasted_iota(jnp.int32, (sub, seen), 0)
            c = lax.broadcasted_iota(jnp.int32, (sub, seen), 1)
            s = jnp.where(c <= r, scores(hd, i, keys(hd, qi, seen)), NEG_INF)
            m0 = jnp.max(s, axis=1, keepdims=True)
            p_ref[hd, rows, :seen] = jnp.exp2(s - m0).astype(BF16)
            if seen < tile:
                p_ref[hd, rows, seen:] = jnp.zeros((sub, tile - seen), BF16)
            m_ref[hd, rows, :] = m0
            alpha_ref[hd, rows, :] = jnp.ones((sub, 1), F32)
            acc_ref[hd, rows, :] = jnp.zeros((sub, 2 * HEAD_DIM), F32)

    def accumulate(hd, i, v_tile):
        rows = slice(i * sub, (i + 1) * sub)
        return alpha_ref[hd, rows, :] * acc_ref[hd, rows, :] + weighted_values(hd, i, v_tile)

    def trip(j):
        j_prev = jnp.where(j == 0, qi, j - 1)
        k_tiles = [keys(hd, j) for hd in range(heads)]
        v_tiles = [values(hd, j_prev) for hd in range(heads)]
        for hd, i in chains:
            rows = slice(i * sub, (i + 1) * sub)
            acc_ref[hd, rows, :] = accumulate(hd, i, v_tiles[hd])
            s = scores(hd, i, k_tiles[hd])
            m = m_ref[hd, rows, :]
            m_new = jnp.maximum(m, jnp.max(s, axis=1, keepdims=True))
            p_ref[hd, rows, :] = jnp.exp2(s - m_new).astype(BF16)
            alpha_ref[hd, rows, :] = jnp.exp2(m - m_new)
            m_ref[hd, rows, :] = m_new

    def finish():
        j_last = jnp.where(qi == 0, qi, qi - 1)
        for hd, i in chains:
            acc = accumulate(hd, i, values(hd, j_last))
            o_ref[i * sub:(i + 1) * sub, head_cols(hd)] = (
                acc[:, :HEAD_DIM] / acc[:, HEAD_DIM:HEAD_DIM + 1]).astype(o_ref.dtype)

    odd = qi % 2

    @pl.when(qi == 0)
    def _no_past():
        diagonal()
        finish()

    @pl.when(odd == 1)
    def _odd_past():
        diagonal()
        trip(0)

    @pl.when((odd == 0) & (qi > 0))
    def _even_past():
        diagonal()
        trip(0)
        trip(1)

    first = 2 - odd

    def pair(t, carry):
        trip(first + 2 * t)
        trip(first + 2 * t + 1)
        return carry

    lax.fori_loop(0, (qi - first) // 2, pair, 0)
    pl.when(qi > 0)(finish)


def _stage_plan(cols_used, col_start, out_shape, steps):
    rows_out, cols_out = out_shape
    for col_blocks in (1, 2, 4):
        row_blocks = steps // col_blocks
        rb = rows_out // row_blocks
        if steps % col_blocks or rows_out % row_blocks or rb % BF16_ROWS:
            continue
        cb_in = cols_used if col_blocks == 1 else cols_out // col_blocks
        if col_blocks > 1 and (cols_used != cols_out or cb_in % LANES):
            continue
        if col_start % cb_in:
            continue
        return rb, col_blocks, cb_in
    raise ValueError(f"no staging plan for {cols_used} columns at {col_start} -> {out_shape} in {steps} steps")


def _stage_block(step, src_ref, dst_ref, *, rows, rows_out, col_blocks):
    rb, cb_in = src_ref.shape
    x = src_ref[...]
    if rows_out > rows:
        row = (step // col_blocks) * rb + lax.broadcasted_iota(jnp.int32, x.shape, 0)
        x = jnp.where(row < rows, x, 0.0)
    dst_ref[:, :cb_in] = x.astype(dst_ref.dtype)
    if dst_ref.shape[1] > cb_in:
        dst_ref[:, cb_in:] = jnp.zeros((rb, dst_ref.shape[1] - cb_in), dst_ref.dtype)


def _moba(z, *, batch, seq, stage=(), tile=1024, n_sub=4, heads=2):
    t = z.shape[0]
    n_blocks = seq // MOBA_BLOCK
    tiles = seq // tile
    groups = MOBA_HEADS // heads
    width = heads * HEAD_DIM
    steps = batch * groups * tiles

    def step_of(b, g, i):
        return (b * groups + g) * tiles + i

    stage_in, stage_out, stage_shapes, stage_args = [], [], [], []
    for w, out_shape, col_start in stage:
        cols_used = min(w.shape[1] - col_start, out_shape[1])
        rb, col_blocks, cb_in = _stage_plan(cols_used, col_start, out_shape, steps)
        cb_out = out_shape[1] // col_blocks
        last = (w.shape[0] - 1) // rb
        stage_in.append(pl.BlockSpec(
            (rb, cb_in), lambda b, g, i, cbs=col_blocks, last=last, c0=col_start // cb_in:
            (jnp.minimum(step_of(b, g, i) // cbs, last), c0 + step_of(b, g, i) % cbs)))
        stage_out.append(pl.BlockSpec(
            (rb, cb_out), lambda b, g, i, cbs=col_blocks: (step_of(b, g, i) // cbs, step_of(b, g, i) % cbs)))
        stage_shapes.append(jax.ShapeDtypeStruct(out_shape, BF16))
        stage_args.append(dict(rows=w.shape[0], rows_out=out_shape[0], col_blocks=col_blocks))

    kern = functools.partial(_moba_kernel, tile=tile, n_blocks=n_blocks, n_sub=n_sub, heads=heads,
                             stage_args=tuple(stage_args), groups=groups, tiles=tiles)
    outs = pl.pallas_call(
        kern,
        grid=(batch, groups, tiles),
        in_specs=[pl.BlockSpec((tile, width), lambda b, g, i: (b * tiles + i, g)),
                  pl.BlockSpec((seq, width), lambda b, g, i: (b, groups + g)),
                  pl.BlockSpec((seq, width), lambda b, g, i: (b, 2 * groups + g))] + stage_in,
        out_specs=[pl.BlockSpec((tile, width), lambda b, g, i: (b * tiles + i, g))] + stage_out,
        out_shape=[jax.ShapeDtypeStruct((t, MOBA_WIDTH), BF16)] + stage_shapes,
        scratch_shapes=[pltpu.VMEM((seq, LANES), BF16),
                        pltpu.VMEM((tile, LANES), BF16),
                        pltpu.VMEM((heads, LANES, HEAD_DIM), F32),
                        pltpu.VMEM((heads, tile, tile), BF16),
                        pltpu.VMEM((heads, tile, 1), F32),
                        pltpu.VMEM((heads, tile, 1), F32),
                        pltpu.VMEM((heads, tile, 2 * HEAD_DIM), F32)],
        compiler_params=_cparams(("arbitrary", "arbitrary", "arbitrary")),
        name="moba_attention",
    )(z, z, z, *[item[0] for item in stage])
    return outs[0] if not stage else tuple(outs)


def _sgu_kernel(u_ref, v_ref, g_ref, b_ref, w_ref, bs_ref, o_ref, *, rows):
    v = v_ref[...].astype(F32)
    mu = jnp.mean(v, axis=-1, keepdims=True)
    var = jnp.mean(jnp.square(v - mu), axis=-1, keepdims=True)
    vln = ((v - mu) * lax.rsqrt(var + LN_EPS) * g_ref[...] + b_ref[...]).astype(BF16)
    t_idx = lax.broadcasted_iota(jnp.int32, (SGU_CHUNK, SGU_CHUNK), 0)
    s_idx = lax.broadcasted_iota(jnp.int32, (SGU_CHUNK, SGU_CHUNK), 1)
    causal = s_idx <= t_idx
    for g in range(SGU_GROUPS):
        w = jnp.where(causal, w_ref[g], 0.0).astype(BF16)
        bias = bs_ref[:, g:g + 1]
        cols = slice(g * LANES, (g + 1) * LANES)
        for ch in range(rows // SGU_CHUNK):
            rws = slice(ch * SGU_CHUNK, (ch + 1) * SGU_CHUNK)
            mixed = jnp.dot(w, vln[rws, cols], preferred_element_type=F32) + bias
            o_ref[rws, cols] = (u_ref[rws, cols].astype(F32) * mixed).astype(o_ref.dtype)


def _sgu(z, ln_g, ln_b, w_s, b_s_t, *, width, u_col, rows=1024):
    t = z.shape[0]
    u_blk = u_col // width
    kern = functools.partial(_sgu_kernel, rows=rows)
    return pl.pallas_call(
        kern,
        grid=(t // rows,),
        in_specs=[pl.BlockSpec((rows, width), lambda i: (i, u_blk)),
                  pl.BlockSpec((rows, width), lambda i: (i, u_blk + 1)),
                  pl.BlockSpec((1, width), lambda i: (0, 0)),
                  pl.BlockSpec((1, width), lambda i: (0, 0)),
                  pl.BlockSpec((SGU_GROUPS, SGU_CHUNK, SGU_CHUNK), lambda i: (0, 0, 0)),
                  pl.BlockSpec((SGU_CHUNK, SGU_GROUPS), lambda i: (0, 0))],
        out_specs=pl.BlockSpec((rows, width), lambda i: (i, 0)),
        out_shape=jax.ShapeDtypeStruct((t, width), BF16),
        compiler_params=_cparams(("arbitrary",)),
        name="spatial_gating",
    )(z, z, ln_g.reshape(1, width), ln_b.reshape(1, width), w_s, b_s_t)


def _merge_kernel(oa_ref, ob_ref, pa_ref, pb_ref, ga_ref, gb_ref, o_ref):
    oa, ob = oa_ref[...], ob_ref[...]
    for c in range(o_ref.shape[1] // PROJ_COLS):
        cols = slice(c * PROJ_COLS, (c + 1) * PROJ_COLS)
        a = jnp.dot(oa, pa_ref[:, cols], preferred_element_type=F32)
        b = jnp.dot(ob, pb_ref[:, cols], preferred_element_type=F32)
        o_ref[:, cols] = (ga_ref[:, cols].astype(F32) * a + gb_ref[:, cols].astype(F32) * b).astype(o_ref.dtype)


def _merge(o_a, o_b, p_a, p_b, z, *, gate_col, bm=1024, bn=1024):
    t = o_a.shape[0]
    d = p_a.shape[1]
    ga_blk = gate_col // bn
    gb_blk = (gate_col + d) // bn
    return pl.pallas_call(
        _merge_kernel,
        grid=(t // bm, d // bn),
        in_specs=[pl.BlockSpec((bm, o_a.shape[1]), lambda m, n: (m, 0)),
                  pl.BlockSpec((bm, o_b.shape[1]), lambda m, n: (m, 0)),
                  pl.BlockSpec((p_a.shape[0], bn), lambda m, n: (0, n)),
                  pl.BlockSpec((p_b.shape[0], bn), lambda m, n: (0, n)),
                  pl.BlockSpec((bm, bn), lambda m, n: (m, ga_blk + n)),
                  pl.BlockSpec((bm, bn), lambda m, n: (m, gb_blk + n))],
        out_specs=pl.BlockSpec((bm, bn), lambda m, n: (m, n)),
        out_shape=jax.ShapeDtypeStruct((t, d), BF16),
        compiler_params=_cparams(("arbitrary", "arbitrary")),
        name="branch_merge",
    )(o_a, o_b, p_a, p_b, z, z)


def _outproj_kernel(a_ref, w_ref, x_ref, o_ref):
    o_ref[...] = x_ref[...] + jnp.dot(a_ref[...], w_ref[...], preferred_element_type=F32)


def _outproj(a, w, x, *, bm=1024, bn=1024):
    t, k = a.shape
    d = w.shape[1]
    return pl.pallas_call(
        _outproj_kernel,
        grid=(t // bm, d // bn),
        in_specs=[pl.BlockSpec((bm, k), lambda m, n: (m, 0)),
                  pl.BlockSpec((k, bn), lambda m, n: (0, n)),
                  pl.BlockSpec((bm, bn), lambda m, n: (m, n))],
        out_specs=pl.BlockSpec((bm, bn), lambda m, n: (m, n)),
        out_shape=jax.ShapeDtypeStruct((t, d), F32),
        compiler_params=_cparams(("arbitrary", "arbitrary")),
        name="out_proj",
    )(a, w, x)


def _memkv_kernel(mem_ref, g_ref, w_ref, o_ref):
    mem_n = _rmsnorm_rows(mem_ref[...], g_ref[...]).astype(BF16)
    o_ref[...] = jnp.dot(mem_n, w_ref[...], preferred_element_type=F32).astype(o_ref.dtype)


def _memkv(mem, g, w, *, n_mem):
    t, d = mem.shape
    cols = w.shape[1]
    return pl.pallas_call(
        _memkv_kernel,
        grid=(t // n_mem,),
        in_specs=[pl.BlockSpec((n_mem, d), lambda i: (i, 0)),
                  pl.BlockSpec((1, d), lambda i: (0, 0)),
                  pl.BlockSpec((d, cols), lambda i: (0, 0))],
        out_specs=pl.BlockSpec((n_mem, cols), lambda i: (i, 0)),
        out_shape=jax.ShapeDtypeStruct((t, cols), BF16),
        compiler_params=_cparams(("arbitrary",)),
        name="mem_kv",
    )(mem, g.reshape(1, d), w)


def _xattn_kernel(x_ref, gx_ref, wq_ref, kv_ref, wo_ref, gf_ref, x_out_ref, h_out_ref, *, chain_rows):
    chains = [slice(c * chain_rows, (c + 1) * chain_rows) for c in range(x_ref.shape[0] // chain_rows)]

    def attend(q):
        outs = []
        for hh in range(XATTN_HEADS):
            cols = slice(hh * HEAD_DIM, (hh + 1) * HEAD_DIM)
            k = kv_ref[:, hh * HEAD_DIM:(hh + 1) * HEAD_DIM]
            v = kv_ref[:, XATTN_WIDTH + hh * HEAD_DIM:XATTN_WIDTH + (hh + 1) * HEAD_DIM]
            s = lax.dot_general(q[:, cols], k, (((1,), (1,)), ((), ())), preferred_element_type=F32)
            s = s - jnp.max(s, axis=-1, keepdims=True)
            p = jnp.exp(s)
            p = p / jnp.sum(p, axis=-1, keepdims=True)
            outs.append(jnp.dot(p.astype(BF16), v, preferred_element_type=F32).astype(BF16))
        return jnp.concatenate(outs, axis=1)

    hs = [_rmsnorm_rows(x_ref[rows, :], gx_ref[...]).astype(BF16) for rows in chains]
    qs = [(jnp.dot(h, wq_ref[...], preferred_element_type=F32) * (HEAD_DIM ** -0.5)).astype(BF16) for h in hs]
    os = [attend(q) for q in qs]
    for rows, o in zip(chains, os):
        x_out_ref[rows, :] = x_ref[rows, :] + jnp.dot(o, wo_ref[...], preferred_element_type=F32)
    for rows in chains:
        h_out_ref[rows, :] = _rmsnorm_rows(x_out_ref[rows, :], gf_ref[...]).astype(h_out_ref.dtype)


def _xattn(x, g_x, w_q, kv, w_o, g_ffn, *, seq, n_mem, bm=512, chain_rows=256):
    t, d = x.shape
    per_batch = seq // bm
    resident = pl.Buffered(1)
    return pl.pallas_call(
        functools.partial(_xattn_kernel, chain_rows=chain_rows),
        grid=(t // bm,),
        in_specs=[pl.BlockSpec((bm, d), lambda i: (i, 0)),
                  pl.BlockSpec((1, d), lambda i: (0, 0)),
                  pl.BlockSpec(w_q.shape, lambda i: (0, 0), pipeline_mode=resident),
                  pl.BlockSpec((n_mem, kv.shape[1]), lambda i: (i // per_batch, 0)),
                  pl.BlockSpec(w_o.shape, lambda i: (0, 0), pipeline_mode=resident),
                  pl.BlockSpec((1, d), lambda i: (0, 0))],
        out_specs=[pl.BlockSpec((bm, d), lambda i: (i, 0)),
                   pl.BlockSpec((bm, d), lambda i: (i, 0))],
        out_shape=[jax.ShapeDtypeStruct((t, d), F32),
                   jax.ShapeDtypeStruct((t, d), BF16)],
        compiler_params=_cparams(("arbitrary",)),
        name="mem_cross_attention",
    )(x, g_x.reshape(1, d), w_q, kv, w_o, g_ffn.reshape(1, d))


def _ffn_up_kernel(h_ref, wg_ref, wu_ref, o_ref, *, last_cols):
    n = pl.program_id(1)
    bn = o_ref.shape[1]

    def swiglu(cols):
        h = h_ref[...]
        for c in range(cols // PROJ_COLS):
            sl = slice(c * PROJ_COLS, (c + 1) * PROJ_COLS)
            g = jnp.dot(h, wg_ref[:, sl], preferred_element_type=F32)
            u = jnp.dot(h, wu_ref[:, sl], preferred_element_type=F32)
            o_ref[:, sl] = (_silu(g) * u).astype(o_ref.dtype)
        if cols < bn:
            o_ref[:, cols:] = jnp.zeros((o_ref.shape[0], bn - cols), o_ref.dtype)

    last = pl.num_programs(1) - 1
    pl.when(n < last)(functools.partial(swiglu, bn))
    pl.when(n == last)(functools.partial(swiglu, last_cols))


def _ffn_up(h, w_g, w_u, *, d_ff, bm=1024, bn=1024):
    t, d = h.shape
    f = w_g.shape[1]
    assert f % bn == 0 and (d_ff - (f // bn - 1) * bn) % PROJ_COLS == 0
    return pl.pallas_call(
        functools.partial(_ffn_up_kernel, last_cols=d_ff - (f // bn - 1) * bn),
        grid=(t // bm, f // bn),
        in_specs=[pl.BlockSpec((bm, d), lambda m, n: (m, 0)),
                  pl.BlockSpec((d, bn), lambda m, n: (0, n)),
                  pl.BlockSpec((d, bn), lambda m, n: (0, n))],
        out_specs=pl.BlockSpec((bm, bn), lambda m, n: (m, n)),
        out_shape=jax.ShapeDtypeStruct((t, f), BF16),
        compiler_params=_cparams(("arbitrary", "arbitrary")),
        name="ffn_up",
    )(h, w_g, w_u)


def _ffn_down_kernel(a_ref, w_ref, x_ref, g_ref, o_ref, *, final_norm, last_rows):
    k = pl.program_id(1)
    last = pl.num_programs(1) - 1
    bm, d = o_ref.shape

    def accumulate(base_ref, depth):
        a = a_ref[:, :depth]
        for c in range(d // DOWN_COLS):
            cols = slice(c * DOWN_COLS, (c + 1) * DOWN_COLS)
            o_ref[:, cols] = base_ref[:, cols] + jnp.dot(a, w_ref[:depth, cols], preferred_element_type=F32)

    pl.when(k == 0)(functools.partial(accumulate, x_ref, a_ref.shape[1]))
    pl.when((k > 0) & (k < last))(functools.partial(accumulate, o_ref, a_ref.shape[1]))
    pl.when(k == last)(functools.partial(accumulate, o_ref, last_rows))

    if final_norm:
        @pl.when(k == last)
        def _final_norm():
            for r in range(bm // NORM_ROWS):
                rows = slice(r * NORM_ROWS, (r + 1) * NORM_ROWS)
                o_ref[rows, :] = _rmsnorm_rows(o_ref[rows, :], g_ref[...])


def _ffn_down(a, w, x, g, *, final_norm, d_ff, bm=512, bk=1024):
    t, f = a.shape
    d = w.shape[1]
    assert f // bk >= 2, "first and last contraction steps must differ"
    return pl.pallas_call(
        functools.partial(_ffn_down_kernel, final_norm=final_norm, last_rows=d_ff - (f // bk - 1) * bk),
        grid=(t // bm, f // bk),
        in_specs=[pl.BlockSpec((bm, bk), lambda m, k: (m, k)),
                  pl.BlockSpec((bk, d), lambda m, k: (k, 0)),
                  pl.BlockSpec((bm, d), lambda m, k: (m, 0)),
                  pl.BlockSpec((1, d), lambda m, k: (0, 0))],
        out_specs=pl.BlockSpec((bm, d), lambda m, k: (m, 0)),
        out_shape=jax.ShapeDtypeStruct((t, d), F32),
        compiler_params=_cparams(("arbitrary", "arbitrary")),
        name="ffn_down",
    )(a, w, x, g.reshape(1, d))


@jax.jit
def kernel(x, mem, norm_mix_g, w_in, sgu_ln_g, sgu_ln_b, w_sgu, b_sgu, w_branch_a, w_branch_b, w_out, norm_xattn_g, norm_mem_g, w_xq, w_xkv, w_xo, norm_ffn_g, w_ff_gate, w_ff_up, w_ff_down, norm_final_g):
    batch, seq, d = x.shape
    n_mem = mem.shape[1]
    depth = w_in.shape[0]
    sgu_width = sgu_ln_g.shape[1]
    d_ff = w_ff_gate.shape[2]
    ff_pad = (-d_ff) % FF_PAD

    xt = x.reshape(batch * seq, d)
    memt = mem.reshape(batch * n_mem, d)
    cos, sin = _rope_table(seq)

    for l in range(depth):
        h = _rmsnorm(xt, norm_mix_g[l], BF16)
        qkv_cols = 3 * MOBA_WIDTH
        blocks = MOBA_WIDTH // INPROJ_COLS
        z_qkv = _inproj(h, w_in[l][:, :qkv_cols].astype(BF16), (cos, sin), seq=seq, name="in_proj_qkv",
                        segments=(("rope_q", blocks), ("rope", blocks), ("plain", blocks)),
                        bm=INPROJ_ROWS, bn=INPROJ_COLS)
        f_pad = d_ff + ff_pad
        rest_cols = w_in.shape[2] - qkv_cols
        o_a, w_rest, w_g, w_u, w_d, w_pa, w_pb, w_o, w_q, w_kv, w_xout = _moba(
            z_qkv, batch=batch, seq=seq, stage=(
                (w_in[l], (d, rest_cols), qkv_cols),
                (w_ff_gate[l], (d, f_pad), 0), (w_ff_up[l], (d, f_pad), 0), (w_ff_down[l], (f_pad, d), 0),
                (w_branch_a[l], w_branch_a[l].shape, 0), (w_branch_b[l], w_branch_b[l].shape, 0),
                (w_out[l], w_out[l].shape, 0), (w_xq[l], w_xq[l].shape, 0), (w_xkv[l], w_xkv[l].shape, 0),
                (w_xo[l], w_xo[l].shape, 0)))
        z = _inproj(h, w_rest, (), seq=seq, name="in_proj_rest", bm=INPROJ_ROWS, bn=INPROJ_COLS,
                    segments=(("gelu", 2 * sgu_width // INPROJ_COLS), ("sigmoid", 2 * d // INPROJ_COLS)))
        o_b = _sgu(z, sgu_ln_g[l], sgu_ln_b[l], w_sgu[l], jnp.transpose(b_sgu[l]), width=sgu_width, u_col=0)
        merged = _merge(o_a, o_b, w_pa, w_pb, z, gate_col=2 * sgu_width)
        xt = _outproj(merged, w_o, xt)

        kv = _memkv(memt, norm_mem_g[l], w_kv, n_mem=n_mem)
        xt, h = _xattn(xt, norm_xattn_g[l], w_q, kv, w_xout, norm_ffn_g[l], seq=seq, n_mem=n_mem)

        a = _ffn_up(h, w_g, w_u, d_ff=d_ff)
        xt = _ffn_down(a, w_d, xt, norm_final_g, final_norm=(l == depth - 1), d_ff=d_ff)
    return xt.reshape(batch, seq, d)
```

```python
import functools
import math

import jax
import jax.numpy as jnp
from jax import lax
from jax.experimental import pallas as pl
from jax.experimental.pallas import tpu as pltpu

F32 = jnp.float32
BF16 = jnp.bfloat16

HEAD_DIM = 128
MOBA_HEADS = 16
MOBA_WIDTH = MOBA_HEADS * HEAD_DIM
MOBA_BLOCK = 256
MOBA_TOPK = 3
SGU_CHUNK = 128
SGU_GROUPS = 16
XATTN_HEADS = 4
XATTN_WIDTH = XATTN_HEADS * HEAD_DIM
RMS_EPS = 1e-6
LN_EPS = 1e-5
ROPE_THETA = 10000.0
NEG_INF = -1e30
REMOVED = -3e38
Q_SCALE = HEAD_DIM ** -0.5 * math.log2(math.e)

LANES = 128
BF16_ROWS = 16
FF_PAD = 1024
INPROJ_ROWS = 1024
INPROJ_COLS = 1024
PROJ_COLS = 256
DOWN_COLS = 1024
NORM_ROWS = 64
VMEM_LIMIT = 60 * 1024 * 1024


def _cparams(sem):
    return pltpu.CompilerParams(dimension_semantics=sem, vmem_limit_bytes=VMEM_LIMIT)


def _rmsnorm_rows(x, g):
    ms = jnp.mean(x * x, axis=-1, keepdims=True)
    return x * lax.rsqrt(ms + RMS_EPS) * g


def _rmsnorm_kernel(x_ref, g_ref, o_ref):
    o_ref[...] = _rmsnorm_rows(x_ref[...], g_ref[...]).astype(o_ref.dtype)


def _rmsnorm(x, g, out_dtype, bm=512):
    t, d = x.shape
    return pl.pallas_call(
        _rmsnorm_kernel,
        grid=(t // bm,),
        in_specs=[pl.BlockSpec((bm, d), lambda i: (i, 0)),
                  pl.BlockSpec((1, d), lambda i: (0, 0))],
        out_specs=pl.BlockSpec((bm, d), lambda i: (i, 0)),
        out_shape=jax.ShapeDtypeStruct((t, d), out_dtype),
        compiler_params=_cparams(("arbitrary",)),
        name="rmsnorm",
    )(x, g.reshape(1, d))


def _rope_kernel(freq_ref, cos_ref, sin_ref, *, rows):
    i = pl.program_id(0)
    pos = (i * rows + lax.broadcasted_iota(jnp.int32, (rows, LANES), 0)).astype(F32)
    lane = lax.broadcasted_iota(jnp.int32, (rows, LANES), 1)
    ang = pos * freq_ref[...]
    cos_ref[...] = jnp.cos(ang)
    s = jnp.sin(ang)
    sin_ref[...] = jnp.where(lane < HEAD_DIM // 2, -s, s)


def _rope_table(seq, rows=1024):
    half = HEAD_DIM // 2
    inv_freq = jnp.power(ROPE_THETA, -(jnp.arange(half, dtype=F32) * 2.0 / HEAD_DIM))
    freq = jnp.concatenate([inv_freq, inv_freq]).reshape(1, HEAD_DIM)
    return pl.pallas_call(
        functools.partial(_rope_kernel, rows=rows),
        grid=(seq // rows,),
        in_specs=[pl.BlockSpec((1, HEAD_DIM), lambda i: (0, 0))],
        out_specs=[pl.BlockSpec((rows, HEAD_DIM), lambda i: (i, 0))] * 2,
        out_shape=[jax.ShapeDtypeStruct((seq, HEAD_DIM), F32)] * 2,
        compiler_params=_cparams(("arbitrary",)),
        name="rope_table",
    )(freq)


def _gelu_tanh(x):
    c = math.sqrt(2.0 / math.pi)
    half = 0.5 * x
    return half + half * jnp.tanh(x * (c + (c * 0.044715) * (x * x)))


def _silu(x):
    half = 0.5 * x
    return half + half * jnp.tanh(half)


def _inproj_kernel(h_ref, w_ref, *refs, bn, segments):
    *rope_refs, z_ref = refs
    n = pl.program_id(1)

    def project(epilogue):
        h = h_ref[...]
        for c in range(bn // PROJ_COLS):
            cols = slice(c * PROJ_COLS, (c + 1) * PROJ_COLS)
            blk = jnp.dot(h, w_ref[:, cols], preferred_element_type=F32)
            z_ref[:, cols] = epilogue(blk).astype(z_ref.dtype)

    def rotary(scale):
        def rotate(blk):
            cos, sin = rope_refs[0][...], rope_refs[1][...]
            heads = []
            for hh in range(PROJ_COLS // HEAD_DIM):
                x = blk[:, hh * HEAD_DIM:(hh + 1) * HEAD_DIM]
                heads.append(x * cos + pltpu.roll(x, HEAD_DIM // 2, axis=1) * sin)
            out = jnp.concatenate(heads, axis=1)
            return out if scale is None else out * scale
        return rotate

    epilogues = {
        "rope_q": rotary(Q_SCALE),
        "rope": rotary(None),
        "plain": lambda blk: blk,
        "gelu": _gelu_tanh,
        "sigmoid": lambda blk: 0.5 + 0.5 * jnp.tanh(0.5 * blk),
    }
    start = 0
    for kind, count in segments:
        pl.when((n >= start) & (n < start + count))(functools.partial(project, epilogues[kind]))
        start += count


def _inproj(h, w, rope, *, seq, segments, name, bm, bn, h_mode=None):
    t, d = h.shape
    cols = w.shape[1]
    assert sum(count for _, count in segments) * bn == cols
    seq_blocks = seq // bm
    rope_spec = pl.BlockSpec((bm, HEAD_DIM), lambda m, n: (m % seq_blocks, 0))
    return pl.pallas_call(
        functools.partial(_inproj_kernel, bn=bn, segments=segments),
        grid=(t // bm, cols // bn),
        in_specs=[pl.BlockSpec((bm, d), lambda m, n: (m, 0), pipeline_mode=h_mode),
                  pl.BlockSpec((d, bn), lambda m, n: (0, n))] + [rope_spec] * len(rope),
        out_specs=pl.BlockSpec((bm, bn), lambda m, n: (m, n)),
        out_shape=jax.ShapeDtypeStruct((t, cols), BF16),
        compiler_params=_cparams(("arbitrary", "arbitrary")),
        name=name,
    )(h, w, *rope)


def _moba_kernel(q_ref, k_ref, v_ref, *refs, tile, n_blocks, n_sub, heads, stage_args, groups, tiles):
    n_stage = len(stage_args)
    stage_src, o_ref, stage_dst = refs[:n_stage], refs[n_stage], refs[n_stage + 1:2 * n_stage + 1]
    onehot_ref, ones_ref, kmean_ref, p_ref, m_ref, alpha_ref, acc_ref = refs[2 * n_stage + 1:]
    _moba_body(q_ref, k_ref, v_ref, o_ref, onehot_ref, ones_ref, kmean_ref, p_ref, m_ref, alpha_ref, acc_ref,
               tile=tile, n_blocks=n_blocks, n_sub=n_sub, heads=heads)
    step = (pl.program_id(0) * groups + pl.program_id(1)) * tiles + pl.program_id(2)
    for src, dst, args in zip(stage_src, stage_dst, stage_args):
        _stage_block(step, src, dst, **args)


def _moba_body(q_ref, k_ref, v_ref, o_ref, onehot_ref, ones_ref, kmean_ref, p_ref, m_ref, alpha_ref, acc_ref,
               *, tile, n_blocks, n_sub, heads):
    qi = pl.program_id(2)
    blocks_per_tile = tile // MOBA_BLOCK
    contract_last = (((1,), (1,)), ((), ()))
    sub = tile // n_sub

    chains = [(hd, i) for hd in range(heads) for i in range(n_sub)]

    def head_cols(hd):
        return slice(hd * HEAD_DIM, (hd + 1) * HEAD_DIM)

    @pl.when(qi == 0)
    def _prepare_keys():
        kmean_ref[...] = jnp.zeros_like(kmean_ref)
        lane = lax.broadcasted_iota(jnp.int32, (MOBA_BLOCK, LANES), 1)
        ones_ref[...] = (lax.broadcasted_iota(jnp.int32, ones_ref.shape, 1) == 0).astype(BF16)
        for j in range(n_blocks):
            rows = pl.ds(j * MOBA_BLOCK, MOBA_BLOCK)
            onehot_ref[rows, :] = (lane == j).astype(BF16)
            for hd in range(heads):
                kb = k_ref[rows, head_cols(hd)]
                kmean_ref[hd, j:j + 1, :] = jnp.mean(kb.astype(F32), axis=0, keepdims=True)

    blk = lax.broadcasted_iota(jnp.int32, (n_blocks, tile), 0)
    col = lax.broadcasted_iota(jnp.int32, (n_blocks, tile), 1)
    own_block = qi * blocks_per_tile + col // MOBA_BLOCK
    q_ext = []
    for hd in range(heads):
        q = q_ref[:, head_cols(hd)]
        gate = lax.dot_general(kmean_ref[hd].astype(BF16), q, contract_last,
                               preferred_element_type=F32)[:n_blocks, :]
        gate = jnp.where(blk < own_block, gate, NEG_INF)
        allowed = blk == own_block
        for _ in range(MOBA_TOPK):
            best = jnp.max(gate, axis=0, keepdims=True)
            first = jnp.min(jnp.where(gate == best, blk, n_blocks), axis=0, keepdims=True)
            hit = blk == first
            allowed = allowed | (hit & (best > 0.5 * NEG_INF))
            gate = jnp.where(hit, REMOVED, gate)
        bias_t = jnp.where(allowed, 0.0, NEG_INF)
        bias_t = jnp.concatenate([bias_t, jnp.zeros((LANES - n_blocks, tile), F32)], axis=0)
        bias = jnp.transpose(bias_t).astype(BF16)
        q_ext.append(jnp.concatenate([q, bias], axis=1))

    def keys(hd, j, rows=tile):
        at = pl.ds(pl.multiple_of(j * tile, tile), rows)
        return jnp.concatenate([k_ref[at, head_cols(hd)], onehot_ref[at, :]], axis=1)

    def values(hd, j):
        at = pl.ds(pl.multiple_of(j * tile, tile), tile)
        return jnp.concatenate([v_ref[at, head_cols(hd)], ones_ref[...]], axis=1)

    def scores(hd, i, k_tile):
        return lax.dot_general(q_ext[hd][i * sub:(i + 1) * sub], k_tile, contract_last,
                               preferred_element_type=F32)

    def weighted_values(hd, i, v_tile):
        return jnp.dot(p_ref[hd, i * sub:(i + 1) * sub, :], v_tile, preferred_element_type=F32)

    for hd, i in chains:
        rows = slice(i * sub, (i + 1) * sub)
        seen = (i + 1) * sub
        r = i * sub + lax.broadcasted_iota(jnp.int32, (sub, seen), 0)
        c = lax.broadcasted_iota(jnp.int32, (sub, seen), 1)
        s = jnp.where(c <= r, scores(hd, i, keys(hd, qi, seen)), NEG_INF)
        m0 = jnp.max(s, axis=1, keepdims=True)
        p_ref[hd, rows, :seen] = jnp.exp2(s - m0).astype(BF16)
        if seen < tile:
            p_ref[hd, rows, seen:] = jnp.zeros((sub, tile - seen), BF16)
        m_ref[hd, rows, :] = m0
        alpha_ref[hd, rows, :] = jnp.ones((sub, 1), F32)
        acc_ref[hd, rows, :] = jnp.zeros((sub, 2 * HEAD_DIM), F32)

    def accumulate(hd, i, v_tile):
        rows = slice(i * sub, (i + 1) * sub)
        return alpha_ref[hd, rows, :] * acc_ref[hd, rows, :] + weighted_values(hd, i, v_tile)

    def trip(j):
        j_prev = jnp.where(j == 0, qi, j - 1)
        k_tiles = [keys(hd, j) for hd in range(heads)]
        v_tiles = [values(hd, j_prev) for hd in range(heads)]
        for hd, i in chains:
            rows = slice(i * sub, (i + 1) * sub)
            acc_ref[hd, rows, :] = accumulate(hd, i, v_tiles[hd])
            s = scores(hd, i, k_tiles[hd])
            m = m_ref[hd, rows, :]
            m_new = jnp.maximum(m, jnp.max(s, axis=1, keepdims=True))
            p_ref[hd, rows, :] = jnp.exp2(s - m_new).astype(BF16)
            alpha_ref[hd, rows, :] = jnp.exp2(m - m_new)
            m_ref[hd, rows, :] = m_new

    odd = qi % 2
    pl.when(odd == 1)(functools.partial(trip, 0))

    def pair(t, carry):
        trip(odd + 2 * t)
        trip(odd + 2 * t + 1)
        return carry

    lax.fori_loop(0, qi // 2, pair, 0)
    j_last = jnp.where(qi == 0, qi, qi - 1)
    for hd, i in chains:
        acc = accumulate(hd, i, values(hd, j_last))
        o_ref[i * sub:(i + 1) * sub, head_cols(hd)] = (
            acc[:, :HEAD_DIM] / acc[:, HEAD_DIM:HEAD_DIM + 1]).astype(o_ref.dtype)


def _stage_plan(cols_used, col_start, out_shape, steps):
    rows_out, cols_out = out_shape
    for col_blocks in (1, 2, 4):
        row_blocks = steps // col_blocks
        rb = rows_out // row_blocks
        if steps % col_blocks or rows_out % row_blocks or rb % BF16_ROWS:
            continue
        cb_in = cols_used if col_blocks == 1 else cols_out // col_blocks
        if col_blocks > 1 and (cols_used != cols_out or cb_in % LANES):
            continue
        if col_start % cb_in:
            continue
        return rb, col_blocks, cb_in
    raise ValueError(f"no staging plan for {cols_used} columns at {col_start} -> {out_shape} in {steps} steps")


def _stage_block(step, src_ref, dst_ref, *, rows, rows_out, col_blocks):
    rb, cb_in = src_ref.shape
    x = src_ref[...]
    if rows_out > rows:
        row = (step // col_blocks) * rb + lax.broadcasted_iota(jnp.int32, x.shape, 0)
        x = jnp.where(row < rows, x, 0.0)
    dst_ref[:, :cb_in] = x.astype(dst_ref.dtype)
    if dst_ref.shape[1] > cb_in:
        dst_ref[:, cb_in:] = jnp.zeros((rb, dst_ref.shape[1] - cb_in), dst_ref.dtype)


def _moba(z, *, batch, seq, stage=(), tile=1024, n_sub=4, heads=2):
    t = z.shape[0]
    n_blocks = seq // MOBA_BLOCK
    tiles = seq // tile
    groups = MOBA_HEADS // heads
    width = heads * HEAD_DIM
    steps = batch * groups * tiles

    def step_of(b, g, i):
        return (b * groups + g) * tiles + i

    stage_in, stage_out, stage_shapes, stage_args = [], [], [], []
    for w, out_shape, col_start in stage:
        cols_used = min(w.shape[1] - col_start, out_shape[1])
        rb, col_blocks, cb_in = _stage_plan(cols_used, col_start, out_shape, steps)
        cb_out = out_shape[1] // col_blocks
        last = (w.shape[0] - 1) // rb
        stage_in.append(pl.BlockSpec(
            (rb, cb_in), lambda b, g, i, cbs=col_blocks, last=last, c0=col_start // cb_in:
            (jnp.minimum(step_of(b, g, i) // cbs, last), c0 + step_of(b, g, i) % cbs)))
        stage_out.append(pl.BlockSpec(
            (rb, cb_out), lambda b, g, i, cbs=col_blocks: (step_of(b, g, i) // cbs, step_of(b, g, i) % cbs)))
        stage_shapes.append(jax.ShapeDtypeStruct(out_shape, BF16))
        stage_args.append(dict(rows=w.shape[0], rows_out=out_shape[0], col_blocks=col_blocks))

    kern = functools.partial(_moba_kernel, tile=tile, n_blocks=n_blocks, n_sub=n_sub, heads=heads,
                             stage_args=tuple(stage_args), groups=groups, tiles=tiles)
    outs = pl.pallas_call(
        kern,
        grid=(batch, groups, tiles),
        in_specs=[pl.BlockSpec((tile, width), lambda b, g, i: (b * tiles + i, g)),
                  pl.BlockSpec((seq, width), lambda b, g, i: (b, groups + g)),
                  pl.BlockSpec((seq, width), lambda b, g, i: (b, 2 * groups + g))] + stage_in,
        out_specs=[pl.BlockSpec((tile, width), lambda b, g, i: (b * tiles + i, g))] + stage_out,
        out_shape=[jax.ShapeDtypeStruct((t, MOBA_WIDTH), BF16)] + stage_shapes,
        scratch_shapes=[pltpu.VMEM((seq, LANES), BF16),
                        pltpu.VMEM((tile, LANES), BF16),
                        pltpu.VMEM((heads, LANES, HEAD_DIM), F32),
                        pltpu.VMEM((heads, tile, tile), BF16),
                        pltpu.VMEM((heads, tile, 1), F32),
                        pltpu.VMEM((heads, tile, 1), F32),
                        pltpu.VMEM((heads, tile, 2 * HEAD_DIM), F32)],
        compiler_params=_cparams(("arbitrary", "arbitrary", "arbitrary")),
        name="moba_attention",
    )(z, z, z, *[item[0] for item in stage])
    return outs[0] if not stage else tuple(outs)


def _sgu_kernel(u_ref, v_ref, g_ref, b_ref, w_ref, bs_ref, o_ref, *, rows):
    v = v_ref[...].astype(F32)
    mu = jnp.mean(v, axis=-1, keepdims=True)
    var = jnp.mean(jnp.square(v - mu), axis=-1, keepdims=True)
    vln = ((v - mu) * lax.rsqrt(var + LN_EPS) * g_ref[...] + b_ref[...]).astype(BF16)
    t_idx = lax.broadcasted_iota(jnp.int32, (SGU_CHUNK, SGU_CHUNK), 0)
    s_idx = lax.broadcasted_iota(jnp.int32, (SGU_CHUNK, SGU_CHUNK), 1)
    causal = s_idx <= t_idx
    for g in range(SGU_GROUPS):
        w = jnp.where(causal, w_ref[g], 0.0).astype(BF16)
        bias = bs_ref[:, g:g + 1]
        cols = slice(g * LANES, (g + 1) * LANES)
        for ch in range(rows // SGU_CHUNK):
            rws = slice(ch * SGU_CHUNK, (ch + 1) * SGU_CHUNK)
            mixed = jnp.dot(w, vln[rws, cols], preferred_element_type=F32) + bias
            o_ref[rws, cols] = (u_ref[rws, cols].astype(F32) * mixed).astype(o_ref.dtype)


def _sgu(z, ln_g, ln_b, w_s, b_s_t, *, width, u_col, rows=1024):
    t = z.shape[0]
    u_blk = u_col // width
    kern = functools.partial(_sgu_kernel, rows=rows)
    return pl.pallas_call(
        kern,
        grid=(t // rows,),
        in_specs=[pl.BlockSpec((rows, width), lambda i: (i, u_blk)),
                  pl.BlockSpec((rows, width), lambda i: (i, u_blk + 1)),
                  pl.BlockSpec((1, width), lambda i: (0, 0)),
                  pl.BlockSpec((1, width), lambda i: (0, 0)),
                  pl.BlockSpec((SGU_GROUPS, SGU_CHUNK, SGU_CHUNK), lambda i: (0, 0, 0)),
                  pl.BlockSpec((SGU_CHUNK, SGU_GROUPS), lambda i: (0, 0))],
        out_specs=pl.BlockSpec((rows, width), lambda i: (i, 0)),
        out_shape=jax.ShapeDtypeStruct((t, width), BF16),
        compiler_params=_cparams(("arbitrary",)),
        name="spatial_gating",
    )(z, z, ln_g.reshape(1, width), ln_b.reshape(1, width), w_s, b_s_t)


def _merge_kernel(oa_ref, ob_ref, pa_ref, pb_ref, ga_ref, gb_ref, o_ref):
    oa, ob = oa_ref[...], ob_ref[...]
    for c in range(o_ref.shape[1] // PROJ_COLS):
        cols = slice(c * PROJ_COLS, (c + 1) * PROJ_COLS)
        a = jnp.dot(oa, pa_ref[:, cols], preferred_element_type=F32)
        b = jnp.dot(ob, pb_ref[:, cols], preferred_element_type=F32)
        o_ref[:, cols] = (ga_ref[:, cols].astype(F32) * a + gb_ref[:, cols].astype(F32) * b).astype(o_ref.dtype)


def _merge(o_a, o_b, p_a, p_b, z, *, gate_col, bm=1024, bn=1024):
    t = o_a.shape[0]
    d = p_a.shape[1]
    ga_blk = gate_col // bn
    gb_blk = (gate_col + d) // bn
    return pl.pallas_call(
        _merge_kernel,
        grid=(t // bm, d // bn),
        in_specs=[pl.BlockSpec((bm, o_a.shape[1]), lambda m, n: (m, 0)),
                  pl.BlockSpec((bm, o_b.shape[1]), lambda m, n: (m, 0)),
                  pl.BlockSpec((p_a.shape[0], bn), lambda m, n: (0, n)),
                  pl.BlockSpec((p_b.shape[0], bn), lambda m, n: (0, n)),
                  pl.BlockSpec((bm, bn), lambda m, n: (m, ga_blk + n)),
                  pl.BlockSpec((bm, bn), lambda m, n: (m, gb_blk + n))],
        out_specs=pl.BlockSpec((bm, bn), lambda m, n: (m, n)),
        out_shape=jax.ShapeDtypeStruct((t, d), BF16),
        compiler_params=_cparams(("arbitrary", "arbitrary")),
        name="branch_merge",
    )(o_a, o_b, p_a, p_b, z, z)


def _outproj_kernel(a_ref, w_ref, x_ref, o_ref):
    o_ref[...] = x_ref[...] + jnp.dot(a_ref[...], w_ref[...], preferred_element_type=F32)


def _outproj(a, w, x, *, bm=1024, bn=1024):
    t, k = a.shape
    d = w.shape[1]
    return pl.pallas_call(
        _outproj_kernel,
        grid=(t // bm, d // bn),
        in_specs=[pl.BlockSpec((bm, k), lambda m, n: (m, 0)),
                  pl.BlockSpec((k, bn), lambda m, n: (0, n)),
                  pl.BlockSpec((bm, bn), lambda m, n: (m, n))],
        out_specs=pl.BlockSpec((bm, bn), lambda m, n: (m, n)),
        out_shape=jax.ShapeDtypeStruct((t, d), F32),
        compiler_params=_cparams(("arbitrary", "arbitrary")),
        name="out_proj",
    )(a, w, x)


def _memkv_kernel(mem_ref, g_ref, w_ref, o_ref):
    mem_n = _rmsnorm_rows(mem_ref[...], g_ref[...]).astype(BF16)
    o_ref[...] = jnp.dot(mem_n, w_ref[...], preferred_element_type=F32).astype(o_ref.dtype)


def _memkv(mem, g, w, *, n_mem):
    t, d = mem.shape
    cols = w.shape[1]
    return pl.pallas_call(
        _memkv_kernel,
        grid=(t // n_mem,),
        in_specs=[pl.BlockSpec((n_mem, d), lambda i: (i, 0)),
                  pl.BlockSpec((1, d), lambda i: (0, 0)),
                  pl.BlockSpec((d, cols), lambda i: (0, 0))],
        out_specs=pl.BlockSpec((n_mem, cols), lambda i: (i, 0)),
        out_shape=jax.ShapeDtypeStruct((t, cols), BF16),
        compiler_params=_cparams(("arbitrary",)),
        name="mem_kv",
    )(mem, g.reshape(1, d), w)


def _xattn_kernel(x_ref, gx_ref, wq_ref, kv_ref, wo_ref, gf_ref, x_out_ref, h_out_ref, *, chain_rows):
    chains = [slice(c * chain_rows, (c + 1) * chain_rows) for c in range(x_ref.shape[0] // chain_rows)]

    def attend(q):
        outs = []
        for hh in range(XATTN_HEADS):
            cols = slice(hh * HEAD_DIM, (hh + 1) * HEAD_DIM)
            k = kv_ref[:, hh * HEAD_DIM:(hh + 1) * HEAD_DIM]
            v = kv_ref[:, XATTN_WIDTH + hh * HEAD_DIM:XATTN_WIDTH + (hh + 1) * HEAD_DIM]
            s = lax.dot_general(q[:, cols], k, (((1,), (1,)), ((), ())), preferred_element_type=F32)
            s = s - jnp.max(s, axis=-1, keepdims=True)
            p = jnp.exp(s)
            p = p / jnp.sum(p, axis=-1, keepdims=True)
            outs.append(jnp.dot(p.astype(BF16), v, preferred_element_type=F32).astype(BF16))
        return jnp.concatenate(outs, axis=1)

    hs = [_rmsnorm_rows(x_ref[rows, :], gx_ref[...]).astype(BF16) for rows in chains]
    qs = [(jnp.dot(h, wq_ref[...], preferred_element_type=F32) * (HEAD_DIM ** -0.5)).astype(BF16) for h in hs]
    os = [attend(q) for q in qs]
    for rows, o in zip(chains, os):
        x_out_ref[rows, :] = x_ref[rows, :] + jnp.dot(o, wo_ref[...], preferred_element_type=F32)
    for rows in chains:
        h_out_ref[rows, :] = _rmsnorm_rows(x_out_ref[rows, :], gf_ref[...]).astype(h_out_ref.dtype)


def _xattn(x, g_x, w_q, kv, w_o, g_ffn, *, seq, n_mem, bm=512, chain_rows=256):
    t, d = x.shape
    per_batch = seq // bm
    resident = pl.Buffered(1)
    return pl.pallas_call(
        functools.partial(_xattn_kernel, chain_rows=chain_rows),
        grid=(t // bm,),
        in_specs=[pl.BlockSpec((bm, d), lambda i: (i, 0)),
                  pl.BlockSpec((1, d), lambda i: (0, 0)),
                  pl.BlockSpec(w_q.shape, lambda i: (0, 0), pipeline_mode=resident),
                  pl.BlockSpec((n_mem, kv.shape[1]), lambda i: (i // per_batch, 0)),
                  pl.BlockSpec(w_o.shape, lambda i: (0, 0), pipeline_mode=resident),
                  pl.BlockSpec((1, d), lambda i: (0, 0))],
        out_specs=[pl.BlockSpec((bm, d), lambda i: (i, 0)),
                   pl.BlockSpec((bm, d), lambda i: (i, 0))],
        out_shape=[jax.ShapeDtypeStruct((t, d), F32),
                   jax.ShapeDtypeStruct((t, d), BF16)],
        compiler_params=_cparams(("arbitrary",)),
        name="mem_cross_attention",
    )(x, g_x.reshape(1, d), w_q, kv, w_o, g_ffn.reshape(1, d))


def _ffn_up_kernel(h_ref, wg_ref, wu_ref, o_ref, *, last_cols):
    n = pl.program_id(1)
    bn = o_ref.shape[1]

    def swiglu(cols):
        h = h_ref[...]
        for c in range(cols // PROJ_COLS):
            sl = slice(c * PROJ_COLS, (c + 1) * PROJ_COLS)
            g = jnp.dot(h, wg_ref[:, sl], preferred_element_type=F32)
            u = jnp.dot(h, wu_ref[:, sl], preferred_element_type=F32)
            o_ref[:, sl] = (_silu(g) * u).astype(o_ref.dtype)
        if cols < bn:
            o_ref[:, cols:] = jnp.zeros((o_ref.shape[0], bn - cols), o_ref.dtype)

    last = pl.num_programs(1) - 1
    pl.when(n < last)(functools.partial(swiglu, bn))
    pl.when(n == last)(functools.partial(swiglu, last_cols))


def _ffn_up(h, w_g, w_u, *, d_ff, bm=1024, bn=1024):
    t, d = h.shape
    f = w_g.shape[1]
    assert f % bn == 0 and (d_ff - (f // bn - 1) * bn) % PROJ_COLS == 0
    return pl.pallas_call(
        functools.partial(_ffn_up_kernel, last_cols=d_ff - (f // bn - 1) * bn),
        grid=(t // bm, f // bn),
        in_specs=[pl.BlockSpec((bm, d), lambda m, n: (m, 0)),
                  pl.BlockSpec((d, bn), lambda m, n: (0, n)),
                  pl.BlockSpec((d, bn), lambda m, n: (0, n))],
        out_specs=pl.BlockSpec((bm, bn), lambda m, n: (m, n)),
        out_shape=jax.ShapeDtypeStruct((t, f), BF16),
        compiler_params=_cparams(("arbitrary", "arbitrary")),
        name="ffn_up",
    )(h, w_g, w_u)


def _ffn_down_kernel(a_ref, w_ref, x_ref, g_ref, o_ref, *, final_norm, last_rows):
    k = pl.program_id(1)
    last = pl.num_programs(1) - 1
    bm, d = o_ref.shape

    def accumulate(base_ref, depth):
        a = a_ref[:, :depth]
        for c in range(d // DOWN_COLS):
            cols = slice(c * DOWN_COLS, (c + 1) * DOWN_COLS)
            o_ref[:, cols] = base_ref[:, cols] + jnp.dot(a, w_ref[:depth, cols], preferred_element_type=F32)

    pl.when(k == 0)(functools.partial(accumulate, x_ref, a_ref.shape[1]))
    pl.when((k > 0) & (k < last))(functools.partial(accumulate, o_ref, a_ref.shape[1]))
    pl.when(k == last)(functools.partial(accumulate, o_ref, last_rows))

    if final_norm:
        @pl.when(k == last)
        def _final_norm():
            for r in range(bm // NORM_ROWS):
                rows = slice(r * NORM_ROWS, (r + 1) * NORM_ROWS)
                o_ref[rows, :] = _rmsnorm_rows(o_ref[rows, :], g_ref[...])


def _ffn_down(a, w, x, g, *, final_norm, d_ff, bm=512, bk=1024):
    t, f = a.shape
    d = w.shape[1]
    assert f // bk >= 2, "first and last contraction steps must differ"
    return pl.pallas_call(
        functools.partial(_ffn_down_kernel, final_norm=final_norm, last_rows=d_ff - (f // bk - 1) * bk),
        grid=(t // bm, f // bk),
        in_specs=[pl.BlockSpec((bm, bk), lambda m, k: (m, k)),
                  pl.BlockSpec((bk, d), lambda m, k: (k, 0)),
                  pl.BlockSpec((bm, d), lambda m, k: (m, 0)),
                  pl.BlockSpec((1, d), lambda m, k: (0, 0))],
        out_specs=pl.BlockSpec((bm, d), lambda m, k: (m, 0)),
        out_shape=jax.ShapeDtypeStruct((t, d), F32),
        compiler_params=_cparams(("arbitrary", "arbitrary")),
        name="ffn_down",
    )(a, w, x, g.reshape(1, d))


@jax.jit
def kernel(x, mem, norm_mix_g, w_in, sgu_ln_g, sgu_ln_b, w_sgu, b_sgu, w_branch_a, w_branch_b, w_out, norm_xattn_g, norm_mem_g, w_xq, w_xkv, w_xo, norm_ffn_g, w_ff_gate, w_ff_up, w_ff_down, norm_final_g):
    batch, seq, d = x.shape
    n_mem = mem.shape[1]
    depth = w_in.shape[0]
    sgu_width = sgu_ln_g.shape[1]
    d_ff = w_ff_gate.shape[2]
    ff_pad = (-d_ff) % FF_PAD

    xt = x.reshape(batch * seq, d)
    memt = mem.reshape(batch * n_mem, d)
    cos, sin = _rope_table(seq)

    for l in range(depth):
        h = _rmsnorm(xt, norm_mix_g[l], BF16)
        qkv_cols = 3 * MOBA_WIDTH
        blocks = MOBA_WIDTH // INPROJ_COLS
        z_qkv = _inproj(h, w_in[l][:, :qkv_cols].astype(BF16), (cos, sin), seq=seq, name="in_proj_qkv",
                        segments=(("rope_q", blocks), ("rope", blocks), ("plain", blocks)),
                        bm=INPROJ_ROWS, bn=INPROJ_COLS)
        f_pad = d_ff + ff_pad
        rest_cols = w_in.shape[2] - qkv_cols
        o_a, w_rest, w_g, w_u, w_d, w_pa, w_pb, w_o, w_q, w_kv, w_xout = _moba(
            z_qkv, batch=batch, seq=seq, stage=(
                (w_in[l], (d, rest_cols), qkv_cols),
                (w_ff_gate[l], (d, f_pad), 0), (w_ff_up[l], (d, f_pad), 0), (w_ff_down[l], (f_pad, d), 0),
                (w_branch_a[l], w_branch_a[l].shape, 0), (w_branch_b[l], w_branch_b[l].shape, 0),
                (w_out[l], w_out[l].shape, 0), (w_xq[l], w_xq[l].shape, 0), (w_xkv[l], w_xkv[l].shape, 0),
                (w_xo[l], w_xo[l].shape, 0)))
        z = _inproj(h, w_rest, (), seq=seq, name="in_proj_rest", bm=INPROJ_ROWS, bn=2 * INPROJ_COLS,
                    h_mode=pl.Buffered(1),
                    segments=(("gelu", sgu_width // INPROJ_COLS), ("sigmoid", d // INPROJ_COLS)))
        o_b = _sgu(z, sgu_ln_g[l], sgu_ln_b[l], w_sgu[l], jnp.transpose(b_sgu[l]), width=sgu_width, u_col=0)
        merged = _merge(o_a, o_b, w_pa, w_pb, z, gate_col=2 * sgu_width)
        xt = _outproj(merged, w_o, xt)

        kv = _memkv(memt, norm_mem_g[l], w_kv, n_mem=n_mem)
        xt, h = _xattn(xt, norm_xattn_g[l], w_q, kv, w_xout, norm_ffn_g[l], seq=seq, n_mem=n_mem)

        a = _ffn_up(h, w_g, w_u, d_ff=d_ff)
        xt = _ffn_down(a, w_d, xt, norm_final_g, final_norm=(l == depth - 1), d_ff=d_ff)
    return xt.reshape(batch, seq, d)
```
